```python
import math
import jax, jax.numpy as jnp
from jax import lax
import numpy as np

D_MODEL = 2048
BATCH = 2
SEQ = 4096
DEPTH = 4

EXPAND = 2
D_INNER = EXPAND * D_MODEL
HEAD_DIM = 128
D_A = D_INNER // 2
D_B = D_INNER - D_A
H_A = D_A // HEAD_DIM
H_B = D_B // HEAD_DIM
DIFF_QK_DIM = HEAD_DIM // 2
H_C = D_INNER // HEAD_DIM
P_IN = 4 * D_INNER
DILATED_PATTERNS = ((128, 1), (512, 4), (2048, 16))
N_BUCKETS = 32
MAX_DISTANCE = 2048
H_BIAS = H_A + H_B
Q_BLOCK = 128
A_Q_BLOCK = 16
N_EVEN = (DEPTH + 1) // 2
EPS = 1e-6

kernel_name = "hybrid_dilated_diff_stickbreak_trunk"


def rms_norm(x, g):
    xf = x.astype(jnp.float32)
    y = xf * lax.rsqrt(jnp.mean(xf * xf, axis=-1, keepdims=True) + EPS)
    return (y * g.astype(jnp.float32)).astype(x.dtype)


def t5_bucket(dist):
    max_exact = N_BUCKETS // 2
    d = jnp.maximum(dist, 1).astype(jnp.float32)
    large = max_exact + (jnp.log(d / max_exact) / math.log(MAX_DISTANCE / max_exact)
                         * (N_BUCKETS - max_exact)).astype(jnp.int32)
    large = jnp.minimum(large, N_BUCKETS - 1)
    return jnp.where(dist < max_exact, dist, large)


def _blocked(fn, n_blocks):
    out = lax.map(fn, jnp.arange(n_blocks))
    nb, b, qb, h, dh = out.shape
    return jnp.transpose(out, (1, 0, 2, 3, 4)).reshape(b, nb * qb, h, dh)


def dilated_attention(q, k, v, bias_table):
    s = q.shape[1]
    scale = HEAD_DIM ** -0.5
    dist = jnp.asarray(np.stack([np.arange(w // d + 1) * d for w, d in DILATED_PATTERNS]), dtype=jnp.int32)
    bias = jnp.transpose(bias_table[t5_bucket(dist)], (2, 0, 1))

    def block(i):
        start = i * A_Q_BLOCK
        t = start + jnp.arange(A_Q_BLOCK)
        kpos = t[:, None, None] - dist[None]
        valid = kpos >= 0
        kidx = jnp.maximum(kpos, 0)
        qb = lax.dynamic_slice_in_dim(q, start, A_Q_BLOCK, axis=1)
        kb = k[:, kidx]
        vb = v[:, kidx]
        logits = jnp.einsum('bqhd,bqgjhd->bhqgj', qb, kb) * scale + bias[:, None]
        logits = jnp.where(valid, logits, -jnp.inf)
        lse = jax.nn.logsumexp(logits, axis=-1)
        p = jnp.exp(logits - lse[..., None])
        o_g = jnp.einsum('bhqgj,bqgjhd->bqhgd', p, vb)
        w = jax.nn.softmax(lse, axis=-1)
        return jnp.einsum('bqhgd,bhqg->bqhd', o_g, w)

    return _blocked(block, s // A_Q_BLOCK)


def diff_attention(q1, q2, k1, k2, v, bias_table, lam):
    s = q1.shape[1]
    scale = DIFF_QK_DIM ** -0.5
    kpos = jnp.arange(s)

    def block(i):
        start = i * Q_BLOCK
        t = start + jnp.arange(Q_BLOCK)
        rel = t[:, None] - kpos[None, :]
        causal = rel >= 0
        bias = jnp.transpose(bias_table[t5_bucket(jnp.maximum(rel, 0))], (2, 0, 1))

        def softmax_map(q, k):
            qb = lax.dynamic_slice_in_dim(q, start, Q_BLOCK, axis=1)
            logits = jnp.einsum('bqhd,bshd->bhqs', qb, k) * scale + bias
            return jax.nn.softmax(jnp.where(causal, logits, -jnp.inf), axis=-1)

        a = softmax_map(q1, k1) - lam * softmax_map(q2, k2)
        return jnp.einsum('bhqs,bshd->bqhd', a, v)

    return _blocked(block, s // Q_BLOCK)


def stick_breaking_attention(q, k, v):
    s = q.shape[1]
    scale = HEAD_DIM ** -0.5
    kpos = jnp.arange(s)

    def block(i):
        start = i * Q_BLOCK
        t = start + jnp.arange(Q_BLOCK)
        strict = kpos[None, :] < t[:, None]
        qb = lax.dynamic_slice_in_dim(q, start, Q_BLOCK, axis=1)
        z = jnp.einsum('bqhd,bshd->bhqs', qb, k) * scale
        log_beta = jax.nn.log_sigmoid(z)
        log_1m_beta = jnp.where(strict, jax.nn.log_sigmoid(-z), 0.0)
        tail = lax.cumsum(log_1m_beta, axis=3, reverse=True) - log_1m_beta
        a = jnp.where(strict, jnp.exp(log_beta + tail), 0.0)
        return jnp.einsum('bhqs,bshd->bqhd', a, v)

    return _blocked(block, s // Q_BLOCK)


def even_mixer(mix_in, rel_bias, lam_vec, subln_g, lam_init):
    b, s = mix_in.shape[:2]
    sizes = [D_A, D_A, D_A, D_B // 2, D_B // 2, D_B // 2, D_B // 2]
    qa, ka, va, q1, q2, k1, k2, vb = jnp.split(mix_in.astype(jnp.float32), list(np.cumsum(sizes)), axis=-1)
    heads = lambda t, h, d: t.reshape(b, s, h, d)
    o_a = dilated_attention(heads(qa, H_A, HEAD_DIM), heads(ka, H_A, HEAD_DIM), heads(va, H_A, HEAD_DIM),
                            rel_bias[:, :H_A])
    lam_vec = lam_vec.astype(jnp.float32)
    lam = (jnp.exp(jnp.sum(lam_vec[0] * lam_vec[1])) - jnp.exp(jnp.sum(lam_vec[2] * lam_vec[3])) + lam_init)
    o_b = diff_attention(heads(q1, H_B, DIFF_QK_DIM), heads(q2, H_B, DIFF_QK_DIM),
                         heads(k1, H_B, DIFF_QK_DIM), heads(k2, H_B, DIFF_QK_DIM),
                         heads(vb, H_B, HEAD_DIM), rel_bias[:, H_A:], lam)
    o_b = rms_norm(o_b, subln_g) * (1.0 - lam_init)
    out = jnp.concatenate([o_a.reshape(b, s, D_A), o_b.reshape(b, s, D_B)], axis=-1)
    return out.astype(mix_in.dtype)


def odd_mixer(mix_in):
    b, s = mix_in.shape[:2]
    q, k, v = jnp.split(mix_in.astype(jnp.float32), 3, axis=-1)
    heads = lambda t: t.reshape(b, s, H_C, HEAD_DIM)
    out = stick_breaking_attention(heads(q), heads(k), heads(v))
    return out.reshape(b, s, D_INNER).astype(mix_in.dtype)


def setup_inputs(seed: int = 0) -> dict:
    key = jax.random.key(seed)
    ks = jax.random.split(key, 11)
    nrm = lambda k, shape, sc: jax.random.normal(k, shape, jnp.float32) * sc
    return {
        "x": nrm(ks[0], (BATCH, SEQ, D_MODEL), 1.0),
        "c": nrm(ks[1], (BATCH, D_MODEL), 1.0),
        "norm_g": 1.0 + nrm(ks[2], (DEPTH, D_MODEL), 0.02),
        "w_mod": nrm(ks[3], (DEPTH, D_MODEL, 3 * D_MODEL), 0.5 * D_MODEL ** -0.5),
        "b_mod": nrm(ks[4], (DEPTH, 3 * D_MODEL), 0.02),
        "w_in": nrm(ks[5], (DEPTH, D_MODEL, P_IN), D_MODEL ** -0.5),
        "w_out": nrm(ks[6], (DEPTH, D_INNER, D_MODEL), D_INNER ** -0.5),
        "rel_bias": nrm(ks[7], (N_BUCKETS, H_BIAS), 0.5),
        "diff_lambda": nrm(ks[8], (N_EVEN, 4, DIFF_QK_DIM), 0.1),
        "diff_subln_g": 1.0 + nrm(ks[9], (N_EVEN, HEAD_DIM), 0.02),
        "final_norm_g": 1.0 + nrm(ks[10], (D_MODEL,), 0.02),
    }


def reference(x, c, norm_g, w_mod, b_mod, w_in, w_out, rel_bias, diff_lambda, diff_subln_g, final_norm_g):
    h = x
    c_act = jax.nn.silu(c)
    for layer in range(DEPTH):
        mod = c_act @ w_mod[layer] + b_mod[layer]
        shift, scale, gate = jnp.split(mod, 3, axis=-1)
        u = rms_norm(h, norm_g[layer]) * (1.0 + scale[:, None]) + shift[:, None]
        proj = u @ w_in[layer]
        mix_in, z = proj[..., :3 * D_INNER], proj[..., 3 * D_INNER:]
        if layer % 2 == 0:
            e = layer // 2
            lam_init = 0.8 - 0.6 * math.exp(-0.3 * layer)
            mixed = even_mixer(mix_in, rel_bias, diff_lambda[e], diff_subln_g[e], lam_init)
        else:
            mixed = odd_mixer(mix_in)
        y = (mixed * jax.nn.silu(z)) @ w_out[layer]
        h = h + gate[:, None] * y
    return rms_norm(h, final_norm_g)
```

```python
import functools
import math

import jax
import jax.numpy as jnp
from jax import lax
from jax.experimental import pallas as pl
from jax.experimental.pallas import tpu as pltpu

HEAD_DIM = 128
DIFF_QK_DIM = 64
EXPAND = 2
N_BUCKETS = 32
MAX_DISTANCE = 2048
DILATED_PATTERNS = ((128, 1), (512, 4), (2048, 16))
EPS = 1e-6
ATT_BLOCK = 256
NEG_BIG = -1e30
VMEM_LIMIT = 56 * 1024 * 1024

_BF = jnp.bfloat16
_F32 = jnp.float32


def _cparams(sem):
    return pltpu.CompilerParams(dimension_semantics=sem, vmem_limit_bytes=VMEM_LIMIT)


def _dot(a, b):
    return jnp.dot(a, b, preferred_element_type=_F32)


def _dot_nt(a, b):
    return lax.dot_general(a, b, (((1,), (1,)), ((), ())), preferred_element_type=_F32)


def _split_hi_lo(x):
    hi = x.astype(_BF)
    lo = (x - hi.astype(_F32)).astype(_BF)
    return hi, lo


def _silu(z):
    return z / (1.0 + jnp.exp(-z))


def _mod_kernel(c_ref, w_ref, b_ref, o_ref):
    a_hi, a_lo = _split_hi_lo(_silu(c_ref[...]))
    w_hi, w_lo = _split_hi_lo(w_ref[0])
    acc = _dot(a_hi, w_hi) + _dot(a_lo, w_hi) + _dot(a_hi, w_lo)
    o_ref[0] = acc + b_ref[0]


def _modulation(c, w_mod, b_mod):
    depth, d, n = w_mod.shape
    bsz = c.shape[0]
    rows = 8
    tn = min(1024, n)
    c_pad = jnp.zeros((rows, d), _F32).at[:bsz].set(c)
    out = pl.pallas_call(
        _mod_kernel,
        grid=(depth, n // tn),
        in_specs=[
            pl.BlockSpec((rows, d), lambda l, j: (0, 0)),
            pl.BlockSpec((1, d, tn), lambda l, j: (l, 0, j)),
            pl.BlockSpec((1, 1, tn), lambda l, j: (l, 0, j)),
        ],
        out_specs=pl.BlockSpec((1, rows, tn), lambda l, j: (l, 0, j)),
        out_shape=jax.ShapeDtypeStruct((depth, rows, n), _F32),
        compiler_params=_cparams(("arbitrary", "arbitrary")),
        name="modulation",
    )(c_pad, w_mod, b_mod.reshape(depth, 1, n))
    return out[:, :bsz]


def _norm_mod_kernel(h_ref, g_ref, scale_ref, shift_ref, o_ref):
    h = h_ref[...]
    inv = lax.rsqrt(jnp.mean(h * h, axis=-1, keepdims=True) + EPS)
    gain = g_ref[...] * (1.0 + scale_ref[0])
    o_ref[...] = (h * inv * gain + shift_ref[0]).astype(o_ref.dtype)


def _norm_mod(h, g, scale, shift, seq):
    m, d = h.shape
    tm = 512
    per_b = seq // tm
    return pl.pallas_call(
        _norm_mod_kernel,
        grid=(m // tm,),
        in_specs=[
            pl.BlockSpec((tm, d), lambda i: (i, 0)),
            pl.BlockSpec((1, d), lambda i: (0, 0)),
            pl.BlockSpec((1, 1, d), lambda i: (i // per_b, 0, 0)),
            pl.BlockSpec((1, 1, d), lambda i: (i // per_b, 0, 0)),
        ],
        out_specs=pl.BlockSpec((tm, d), lambda i: (i, 0)),
        out_shape=jax.ShapeDtypeStruct((m, d), _BF),
        compiler_params=_cparams(("arbitrary",)),
        name="norm_mod",
    )(h, g.reshape(1, d), scale, shift)


def _final_norm_kernel(h_ref, g_ref, o_ref):
    h = h_ref[...]
    inv = lax.rsqrt(jnp.mean(h * h, axis=-1, keepdims=True) + EPS)
    o_ref[...] = h * inv * g_ref[...]


def _final_norm(h, g):
    m, d = h.shape
    tm = 512
    return pl.pallas_call(
        _final_norm_kernel,
        grid=(m // tm,),
        in_specs=[pl.BlockSpec((tm, d), lambda i: (i, 0)), pl.BlockSpec((1, d), lambda i: (0, 0))],
        out_specs=pl.BlockSpec((tm, d), lambda i: (i, 0)),
        out_shape=jax.ShapeDtypeStruct((m, d), _F32),
        compiler_params=_cparams(("arbitrary",)),
        name="final_norm",
    )(h, g.reshape(1, d))


def _proj_kernel(u_ref, w_ref, o_ref, wbf_ref):
    @pl.when(pl.program_id(1) == 0)
    def _():
        wbf_ref[...] = w_ref[0].astype(_BF)

    o_ref[...] = _dot(u_ref[...], wbf_ref[...]).astype(o_ref.dtype)


def _in_proj(u, w_in, layer):
    m, d = u.shape
    p = w_in.shape[2]
    tm, tn = min(1024, m), min(1024, p)
    return pl.pallas_call(
        _proj_kernel,
        grid=(p // tn, m // tm),
        in_specs=[
            pl.BlockSpec((tm, d), lambda j, i: (i, 0)),
            pl.BlockSpec((1, d, tn), lambda j, i: (layer, 0, j)),
        ],
        out_specs=pl.BlockSpec((tm, tn), lambda j, i: (i, j)),
        out_shape=jax.ShapeDtypeStruct((m, p), _BF),
        scratch_shapes=[pltpu.VMEM((d, tn), _BF)],
        compiler_params=_cparams(("arbitrary", "arbitrary")),
        name="in_proj",
    )(u, w_in)


def _out_kernel(a_ref, w_ref, h_ref, gate_ref, o_ref, wbf_ref):
    @pl.when(pl.program_id(1) == 0)
    def _():
        wbf_ref[...] = w_ref[0].astype(_BF)

    y = _dot(a_ref[...], wbf_ref[...])
    o_ref[...] = h_ref[...] + gate_ref[0] * y


def _out_proj(a, w_out, layer, h, gate, seq):
    m, k = a.shape
    d = w_out.shape[2]
    tm, tn = 512, min(512, d)
    per_b = seq // tm
    return pl.pallas_call(
        _out_kernel,
        grid=(d // tn, m // tm),
        in_specs=[
            pl.BlockSpec((tm, k), lambda j, i: (i, 0)),
            pl.BlockSpec((1, k, tn), lambda j, i: (layer, 0, j)),
            pl.BlockSpec((tm, tn), lambda j, i: (i, j)),
            pl.BlockSpec((1, 1, tn), lambda j, i: (i // per_b, 0, j)),
        ],
        out_specs=pl.BlockSpec((tm, tn), lambda j, i: (i, j)),
        out_shape=jax.ShapeDtypeStruct((m, d), _F32),
        scratch_shapes=[pltpu.VMEM((k, tn), _BF)],
        compiler_params=_cparams(("arbitrary", "arbitrary")),
        name="out_proj",
    )(a, w_out, h, gate)


def _t5_bucket(dist):
    max_exact = N_BUCKETS // 2
    d = jnp.maximum(dist, 1).astype(_F32)
    large = max_exact + (jnp.log(d / max_exact) / math.log(MAX_DISTANCE / max_exact)
                         * (N_BUCKETS - max_exact)).astype(jnp.int32)
    large = jnp.minimum(large, N_BUCKETS - 1)
    return jnp.where(dist < max_exact, dist, large)


def _dilated_log_multiplicity(n):
    delta = jnp.arange(n)
    mult = jnp.zeros((n,), _F32)
    for window, dil in DILATED_PATTERNS:
        mult = mult + ((delta % dil == 0) & (delta <= window)).astype(_F32)
    return jnp.where(mult > 0, jnp.log(jnp.maximum(mult, 1.0)), NEG_BIG)


def _toeplitz_blocks(bias_vec, n_off):
    t = ATT_BLOCK
    h = bias_vec.shape[0]
    c = n_off * t
    length = c + t + 1
    y = jnp.arange(length)
    rr = jnp.take(bias_vec, jnp.clip(c - y, 0, c - 1), axis=1)
    skew = jnp.tile(rr, (1, t))[:, : t * (length - 1)].reshape(h, t, length - 1)
    blocks = [skew[:, :, c - o * t: c - o * t + t] for o in range(n_off)]
    return jnp.stack(blocks, axis=1)


def _softmax_block(q, k_blk, bias, scale, causal):
    s = _dot_nt(q, k_blk) * scale + bias
    if causal:
        t = ATT_BLOCK
        row = lax.broadcasted_iota(jnp.int32, (t, t), 0)
        col = lax.broadcasted_iota(jnp.int32, (t, t), 1)
        s = jnp.where(col <= row, s, NEG_BIG)
    return s


def _flash_update(s, v_blk, m, l, acc):
    m_new = jnp.maximum(m, jnp.max(s, axis=1, keepdims=True))
    alpha = jnp.exp(m - m_new)
    p = jnp.exp(s - m_new)
    l_new = l * alpha + jnp.sum(p, axis=1, keepdims=True)
    acc_new = acc * alpha + _dot(p.astype(_BF), v_blk)
    return m_new, l_new, acc_new


def _dilated_kernel(q_ref, k_ref, v_ref, z_ref, bias_ref, o_ref, *, n_off):
    t = ATT_BLOCK
    qi = pl.program_id(2)
    q = q_ref[...]
    scale = HEAD_DIM ** -0.5

    def step(ki, carry, causal):
        m, l, acc = carry
        start = pl.multiple_of(ki * t, t)
        k_blk = k_ref[pl.ds(start, t), :]
        v_blk = v_ref[pl.ds(start, t), :]
        s = _softmax_block(q, k_blk, bias_ref[0, qi - ki], scale, causal)
        return _flash_update(s, v_blk, m, l, acc)

    init = (jnp.full((t, 1), NEG_BIG, _F32), jnp.zeros((t, 1), _F32), jnp.zeros((t, HEAD_DIM), _F32))
    carry = step(qi, init, True)
    lo = jnp.maximum(qi - (n_off - 1), 0)
    carry = lax.fori_loop(lo, qi, lambda ki, c: step(ki, c, False), carry)
    _, l, acc = carry
    o_ref[...] = (acc / l * _silu(z_ref[...].astype(_F32))).astype(o_ref.dtype)


def _diff_kernel(q1_ref, q2_ref, k1_ref, k2_ref, v_ref, z_ref, bias_ref, lam_ref, g_ref, o_ref, *, lam_init):
    t = ATT_BLOCK
    h = pl.program_id(1)
    qi = pl.program_id(2)
    scale = DIFF_QK_DIM ** -0.5
    lane = lax.broadcasted_iota(jnp.int32, (t, HEAD_DIM), 1)
    mine = (lane // DIFF_QK_DIM) == (h % 2)
    q1 = jnp.where(mine, q1_ref[...], jnp.zeros_like(q1_ref[...]))
    q2 = jnp.where(mine, q2_ref[...], jnp.zeros_like(q2_ref[...]))

    def step(ki, carry, causal):
        m1, l1, a1, m2, l2, a2 = carry
        start = pl.multiple_of(ki * t, t)
        v_blk = v_ref[pl.ds(start, t), :]
        bias = bias_ref[0, qi - ki]
        s1 = _softmax_block(q1, k1_ref[pl.ds(start, t), :], bias, scale, causal)
        m1, l1, a1 = _flash_update(s1, v_blk, m1, l1, a1)
        s2 = _softmax_block(q2, k2_ref[pl.ds(start, t), :], bias, scale, causal)
        m2, l2, a2 = _flash_update(s2, v_blk, m2, l2, a2)
        return m1, l1, a1, m2, l2, a2

    neg = jnp.full((t, 1), NEG_BIG, _F32)
    zero1 = jnp.zeros((t, 1), _F32)
    zacc = jnp.zeros((t, HEAD_DIM), _F32)
    carry = step(qi, (neg, zero1, zacc, neg, zero1, zacc), True)
    carry = lax.fori_loop(0, qi, lambda ki, c: step(ki, c, False), carry)
    _, l1, a1, _, l2, a2 = carry

    lv = lam_ref[...]
    lam = (jnp.exp(jnp.sum(lv[0:1] * lv[1:2], axis=1, keepdims=True))
           - jnp.exp(jnp.sum(lv[2:3] * lv[3:4], axis=1, keepdims=True)) + lam_init)
    o = a1 / l1 - lam * (a2 / l2)
    o = o * lax.rsqrt(jnp.mean(o * o, axis=-1, keepdims=True) + EPS) * g_ref[...] * (1.0 - lam_init)
    o_ref[...] = (o * _silu(z_ref[...].astype(_F32))).astype(o_ref.dtype)


def _stick_kernel(q_ref, k_ref, v_ref, z_ref, o_ref):
    t = ATT_BLOCK
    qi = pl.program_id(2)
    q = q_ref[...]
    scale = HEAD_DIM ** -0.5
    row = lax.broadcasted_iota(jnp.int32, (t, t), 0)
    col = lax.broadcasted_iota(jnp.int32, (t, t), 1)
    later = jnp.where(row > col, 1.0, 0.0).astype(_BF)
    strict = col < row

    def step(ki, carry, diagonal):
        acc, tail0 = carry
        start = pl.multiple_of(ki * t, t)
        k_blk = k_ref[pl.ds(start, t), :]
        v_blk = v_ref[pl.ds(start, t), :]
        z = _dot_nt(q, k_blk) * scale
        softplus = jnp.log(1.0 + jnp.exp(-jnp.abs(z)))
        log_beta = jnp.minimum(z, 0.0) - softplus
        log_1m = log_beta - z
        if diagonal:
            log_1m = jnp.where(strict, log_1m, 0.0)
        hi, lo = _split_hi_lo(log_1m)
        tail = tail0 + _dot(hi, later) + _dot(lo, later)
        a = jnp.exp(log_beta + tail)
        if diagonal:
            a = jnp.where(strict, a, 0.0)
        acc = acc + _dot(a.astype(_BF), v_blk)
        tail0 = tail0 + jnp.sum(log_1m, axis=1, keepdims=True)
        return acc, tail0

    carry = step(qi, (jnp.zeros((t, HEAD_DIM), _F32), jnp.zeros((t, 1), _F32)), True)
    carry = lax.fori_loop(0, qi, lambda i, c: step(qi - 1 - i, c, False), carry)
    acc, _ = carry
    o_ref[...] = (acc * _silu(z_ref[...].astype(_F32))).astype(o_ref.dtype)


def _head_spec(col0, nq):
    return pl.BlockSpec((ATT_BLOCK, HEAD_DIM), lambda b, h, qi: (b * nq + qi, col0 + h))


def _seq_spec(seq, col0, per_block=1):
    return pl.BlockSpec((seq, HEAD_DIM), lambda b, h, qi: (b, col0 + h // per_block))


def _odd_mixer(proj, bsz, seq, d_inner):
    n_heads = d_inner // HEAD_DIM
    nq = seq // ATT_BLOCK
    return pl.pallas_call(
        _stick_kernel,
        grid=(bsz, n_heads, nq),
        in_specs=[
            _head_spec(0, nq),
            _seq_spec(seq, n_heads),
            _seq_spec(seq, 2 * n_heads),
            _head_spec(3 * n_heads, nq),
        ],
        out_specs=_head_spec(0, nq),
        out_shape=jax.ShapeDtypeStruct((bsz * seq, d_inner), _BF),
        compiler_params=_cparams(("arbitrary", "arbitrary", "arbitrary")),
        name="stick_breaking",
    )(proj, proj, proj, proj)


def _even_mixer(proj, bias_a, bias_b, lam_vec, subln_g, lam_init, bsz, seq, d_inner):
    d_a = d_inner // 2
    h_a = d_a // HEAD_DIM
    h_b = (d_inner - d_a) // HEAD_DIM
    nq = seq // ATT_BLOCK
    n_off_a = bias_a.shape[1]
    n_off_b = bias_b.shape[1]
    t = ATT_BLOCK
    out_shape = jax.ShapeDtypeStruct((bsz * seq, d_a), _BF)
    params = _cparams(("arbitrary", "arbitrary", "arbitrary"))
    z0 = 3 * d_inner // HEAD_DIM

    o_a = pl.pallas_call(
        functools.partial(_dilated_kernel, n_off=n_off_a),
        grid=(bsz, h_a, nq),
        in_specs=[
            _head_spec(0, nq),
            _seq_spec(seq, h_a),
            _seq_spec(seq, 2 * h_a),
            _head_spec(z0, nq),
            pl.BlockSpec((1, n_off_a, t, t), lambda b, h, qi: (h, 0, 0, 0)),
        ],
        out_specs=_head_spec(0, nq),
        out_shape=out_shape,
        compiler_params=params,
        name="dilated_attention",
    )(proj, proj, proj, proj, bias_a)

    c0 = 3 * h_a
    half = h_b // 2
    o_b = pl.pallas_call(
        functools.partial(_diff_kernel, lam_init=lam_init),
        grid=(bsz, h_b, nq),
        in_specs=[
            pl.BlockSpec((t, HEAD_DIM), lambda b, h, qi: (b * nq + qi, c0 + h // 2)),
            pl.BlockSpec((t, HEAD_DIM), lambda b, h, qi: (b * nq + qi, c0 + half + h // 2)),
            _seq_spec(seq, c0 + 2 * half, 2),
            _seq_spec(seq, c0 + 3 * half, 2),
            _seq_spec(seq, c0 + 4 * half),
            _head_spec(z0 + h_a, nq),
            pl.BlockSpec((1, n_off_b, t, t), lambda b, h, qi: (h, 0, 0, 0)),
            pl.BlockSpec((4, DIFF_QK_DIM), lambda b, h, qi: (0, 0)),
            pl.BlockSpec((1, HEAD_DIM), lambda b, h, qi: (0, 0)),
        ],
        out_specs=_head_spec(0, nq),
        out_shape=out_shape,
        compiler_params=params,
        name="diff_attention",
    )(proj, proj, proj, proj, proj, proj, bias_b, lam_vec, subln_g.reshape(1, HEAD_DIM))
    return jnp.concatenate([o_a, o_b], axis=1)


def kernel(x, c, norm_g, w_mod, b_mod, w_in, w_out, rel_bias, diff_lambda, diff_subln_g, final_norm_g):
    bsz, seq, d = x.shape
    depth = w_in.shape[0]
    d_inner = w_out.shape[1]
    h_a = d_inner // 2 // HEAD_DIM
    n_blk = seq // ATT_BLOCK

    mod = _modulation(c, w_mod, b_mod)

    table = rel_bias[_t5_bucket(jnp.arange(seq))].T
    n_off_a = min(n_blk, DILATED_PATTERNS[-1][0] // ATT_BLOCK + 1)
    vec_a = table[:h_a, : n_off_a * ATT_BLOCK] + _dilated_log_multiplicity(n_off_a * ATT_BLOCK)[None]
    bias_a = _toeplitz_blocks(vec_a, n_off_a)
    bias_b = _toeplitz_blocks(table[h_a:], n_blk)

    h = x.reshape(bsz * seq, d)
    for layer in range(depth):
        shift = mod[layer, :, 0:d].reshape(bsz, 1, d)
        scale = mod[layer, :, d:2 * d].reshape(bsz, 1, d)
        gate = mod[layer, :, 2 * d:].reshape(bsz, 1, d)
        u = _norm_mod(h, norm_g[layer], scale, shift, seq)
        proj = _in_proj(u, w_in, layer)
        if layer % 2 == 0:
            e = layer // 2
            lam_init = 0.8 - 0.6 * math.exp(-0.3 * layer)
            mixed = _even_mixer(proj, bias_a, bias_b, diff_lambda[e], diff_subln_g[e], lam_init,
                                bsz, seq, d_inner)
        else:
            mixed = _odd_mixer(proj, bsz, seq, d_inner)
        h = _out_proj(mixed, w_out, layer, h, gate, seq)
    return _final_norm(h, final_norm_g).reshape(bsz, seq, d)
```

```python
import functools
import math

import jax
import jax.numpy as jnp
from jax import lax
from jax.experimental import pallas as pl
from jax.experimental.pallas import tpu as pltpu

HEAD_DIM = 128
DIFF_QK_DIM = 64
EXPAND = 2
N_BUCKETS = 32
MAX_DISTANCE = 2048
DILATED_PATTERNS = ((128, 1), (512, 4), (2048, 16))
EPS = 1e-6
SM_BLOCK = 512
STICK_BLOCK = 256
STICK_SKIP_LOG = -104.0
NEG_BIG = -1e30
VMEM_LIMIT = 56 * 1024 * 1024

_BF = jnp.bfloat16
_F32 = jnp.float32


def _cparams(sem):
    return pltpu.CompilerParams(dimension_semantics=sem, vmem_limit_bytes=VMEM_LIMIT)


def _dot(a, b):
    return jnp.dot(a, b, preferred_element_type=_F32)


def _dot_nt(a, b):
    return lax.dot_general(a, b, (((1,), (1,)), ((), ())), preferred_element_type=_F32)


def _split_hi_lo(x):
    hi = x.astype(_BF)
    lo = (x - hi.astype(_F32)).astype(_BF)
    return hi, lo


def _silu(z):
    return z / (1.0 + jnp.exp(-z))


def _mod_kernel(c_ref, w_ref, b_ref, o_ref):
    a_hi, a_lo = _split_hi_lo(_silu(c_ref[...]))
    w_hi, w_lo = _split_hi_lo(w_ref[0])
    acc = _dot(a_hi, w_hi) + _dot(a_lo, w_hi) + _dot(a_hi, w_lo)
    o_ref[0] = acc + b_ref[0]


def _modulation(c, w_mod, b_mod):
    depth, d, n = w_mod.shape
    bsz = c.shape[0]
    rows = 8
    tn = min(1024, n)
    c_pad = jnp.zeros((rows, d), _F32).at[:bsz].set(c)
    out = pl.pallas_call(
        _mod_kernel,
        grid=(depth, n // tn),
        in_specs=[
            pl.BlockSpec((rows, d), lambda l, j: (0, 0)),
            pl.BlockSpec((1, d, tn), lambda l, j: (l, 0, j)),
            pl.BlockSpec((1, 1, tn), lambda l, j: (l, 0, j)),
        ],
        out_specs=pl.BlockSpec((1, rows, tn), lambda l, j: (l, 0, j)),
        out_shape=jax.ShapeDtypeStruct((depth, rows, n), _F32),
        compiler_params=_cparams(("arbitrary", "arbitrary")),
        name="modulation",
    )(c_pad, w_mod, b_mod.reshape(depth, 1, n))
    return out[:, :bsz]


def _norm_mod_kernel(h_ref, g_ref, scale_ref, shift_ref, o_ref):
    h = h_ref[...]
    inv = lax.rsqrt(jnp.mean(h * h, axis=-1, keepdims=True) + EPS)
    gain = g_ref[...] * (1.0 + scale_ref[0])
    o_ref[...] = (h * inv * gain + shift_ref[0]).astype(o_ref.dtype)


def _norm_mod(h, g, scale, shift, seq):
    m, d = h.shape
    tm = 512
    per_b = seq // tm
    return pl.pallas_call(
        _norm_mod_kernel,
        grid=(m // tm,),
        in_specs=[
            pl.BlockSpec((tm, d), lambda i: (i, 0)),
            pl.BlockSpec((1, d), lambda i: (0, 0)),
            pl.BlockSpec((1, 1, d), lambda i: (i // per_b, 0, 0)),
            pl.BlockSpec((1, 1, d), lambda i: (i // per_b, 0, 0)),
        ],
        out_specs=pl.BlockSpec((tm, d), lambda i: (i, 0)),
        out_shape=jax.ShapeDtypeStruct((m, d), _BF),
        compiler_params=_cparams(("arbitrary",)),
        name="norm_mod",
    )(h, g.reshape(1, d), scale, shift)


def _final_norm_kernel(h_ref, g_ref, o_ref):
    h = h_ref[...]
    inv = lax.rsqrt(jnp.mean(h * h, axis=-1, keepdims=True) + EPS)
    o_ref[...] = h * inv * g_ref[...]


def _final_norm(h, g):
    m, d = h.shape
    tm = 512
    return pl.pallas_call(
        _final_norm_kernel,
        grid=(m // tm,),
        in_specs=[pl.BlockSpec((tm, d), lambda i: (i, 0)), pl.BlockSpec((1, d), lambda i: (0, 0))],
        out_specs=pl.BlockSpec((tm, d), lambda i: (i, 0)),
        out_shape=jax.ShapeDtypeStruct((m, d), _F32),
        compiler_params=_cparams(("arbitrary",)),
        name="final_norm",
    )(h, g.reshape(1, d))


def _proj_kernel(u_ref, w_ref, o_ref, wbf_ref):
    @pl.when(pl.program_id(1) == 0)
    def _():
        wbf_ref[...] = w_ref[0].astype(_BF)

    o_ref[...] = _dot(u_ref[...], wbf_ref[...]).astype(o_ref.dtype)


def _in_proj(u, w_in, layer):
    m, d = u.shape
    p = w_in.shape[2]
    tm, tn = min(1024, m), min(1024, p)
    return pl.pallas_call(
        _proj_kernel,
        grid=(p // tn, m // tm),
        in_specs=[
            pl.BlockSpec((tm, d), lambda j, i: (i, 0)),
            pl.BlockSpec((1, d, tn), lambda j, i: (layer, 0, j)),
        ],
        out_specs=pl.BlockSpec((tm, tn), lambda j, i: (i, j)),
        out_shape=jax.ShapeDtypeStruct((m, p), _BF),
        scratch_shapes=[pltpu.VMEM((d, tn), _BF)],
        compiler_params=_cparams(("arbitrary", "arbitrary")),
        name="in_proj",
    )(u, w_in)


def _out_kernel(a_ref, w_ref, h_ref, gate_ref, o_ref, wbf_ref):
    @pl.when(pl.program_id(1) == 0)
    def _():
        wbf_ref[...] = w_ref[0].astype(_BF)

    y = _dot(a_ref[...], wbf_ref[...])
    o_ref[...] = h_ref[...] + gate_ref[0] * y


def _out_proj(a, w_out, layer, h, gate, seq):
    m, k = a.shape
    d = w_out.shape[2]
    tm, tn = 512, min(512, d)
    per_b = seq // tm
    return pl.pallas_call(
        _out_kernel,
        grid=(d // tn, m // tm),
        in_specs=[
            pl.BlockSpec((tm, k), lambda j, i: (i, 0)),
            pl.BlockSpec((1, k, tn), lambda j, i: (layer, 0, j)),
            pl.BlockSpec((tm, tn), lambda j, i: (i, j)),
            pl.BlockSpec((1, 1, tn), lambda j, i: (i // per_b, 0, j)),
        ],
        out_specs=pl.BlockSpec((tm, tn), lambda j, i: (i, j)),
        out_shape=jax.ShapeDtypeStruct((m, d), _F32),
        scratch_shapes=[pltpu.VMEM((k, tn), _BF)],
        compiler_params=_cparams(("arbitrary", "arbitrary")),
        name="out_proj",
    )(a, w_out, h, gate)


def _t5_bucket(dist):
    max_exact = N_BUCKETS // 2
    d = jnp.maximum(dist, 1).astype(_F32)
    large = max_exact + (jnp.log(d / max_exact) / math.log(MAX_DISTANCE / max_exact)
                         * (N_BUCKETS - max_exact)).astype(jnp.int32)
    large = jnp.minimum(large, N_BUCKETS - 1)
    return jnp.where(dist < max_exact, dist, large)


def _dilated_log_multiplicity(n):
    delta = jnp.arange(n)
    mult = jnp.zeros((n,), _F32)
    for window, dil in DILATED_PATTERNS:
        mult = mult + ((delta % dil == 0) & (delta <= window)).astype(_F32)
    return jnp.where(mult > 0, jnp.log(jnp.maximum(mult, 1.0)), NEG_BIG)


def _skewed_bias(bias_vec, n_off):
    t = SM_BLOCK
    h = bias_vec.shape[0]
    c = n_off * t
    length = c + t + 1
    y = jnp.arange(length)
    rr = jnp.take(bias_vec, jnp.clip(c - y, 0, c - 1), axis=1)
    return jnp.tile(rr, (1, t))[:, : t * (length - 1)].reshape(h, t, length - 1)


def _softmax_block(q, k_blk, bias, scale, causal):
    s = _dot_nt(q, k_blk) * scale + bias
    if causal:
        t = SM_BLOCK
        row = lax.broadcasted_iota(jnp.int32, (t, t), 0)
        col = lax.broadcasted_iota(jnp.int32, (t, t), 1)
        s = jnp.where(col <= row, s, NEG_BIG)
    return s


def _flash_update(s, v_blk, m, l, acc):
    m_new = jnp.maximum(m, jnp.max(s, axis=1, keepdims=True))
    alpha = jnp.exp(m - m_new)
    p = jnp.exp(s - m_new)
    l_new = l * alpha + jnp.sum(p, axis=1, keepdims=True)
    acc_new = acc * alpha + _dot(p.astype(_BF), v_blk)
    return m_new, l_new, acc_new


def _bias_tile(bias_ref, n_off, off):
    t = SM_BLOCK
    return bias_ref[0, :, pl.ds(pl.multiple_of((n_off - off) * t, t), t)]


def _dilated_kernel(q_ref, k_ref, v_ref, z_ref, bias_ref, o_ref, *, n_off):
    t = SM_BLOCK
    qi = pl.program_id(2)
    q = q_ref[...]
    scale = HEAD_DIM ** -0.5

    def step(ki, carry, causal):
        m, l, acc = carry
        start = pl.multiple_of(ki * t, t)
        k_blk = k_ref[pl.ds(start, t), :]
        v_blk = v_ref[pl.ds(start, t), :]
        s = _softmax_block(q, k_blk, _bias_tile(bias_ref, n_off, qi - ki), scale, causal)
        return _flash_update(s, v_blk, m, l, acc)

    init = (jnp.full((t, 1), NEG_BIG, _F32), jnp.zeros((t, 1), _F32), jnp.zeros((t, HEAD_DIM), _F32))
    carry = step(qi, init, True)
    lo = jnp.maximum(qi - (n_off - 1), 0)
    carry = lax.fori_loop(lo, qi, lambda ki, c: step(ki, c, False), carry)
    _, l, acc = carry
    o_ref[...] = (acc / l * _silu(z_ref[...].astype(_F32))).astype(o_ref.dtype)


def _diff_kernel(q1_ref, q2_ref, k1_ref, k2_ref, v_ref, z_ref, bias_ref, lam_ref, g_ref, o_ref, *, lam_init, n_off):
    t = SM_BLOCK
    h = pl.program_id(1)
    qi = pl.program_id(2)
    scale = DIFF_QK_DIM ** -0.5
    lane = lax.broadcasted_iota(jnp.int32, (t, HEAD_DIM), 1)
    mine = (lane // DIFF_QK_DIM) == (h % 2)
    q1 = jnp.where(mine, q1_ref[...], jnp.zeros_like(q1_ref[...]))
    q2 = jnp.where(mine, q2_ref[...], jnp.zeros_like(q2_ref[...]))

    def step(ki, carry, causal):
        m1, l1, a1, m2, l2, a2 = carry
        start = pl.multiple_of(ki * t, t)
        v_blk = v_ref[pl.ds(start, t), :]
        bias = _bias_tile(bias_ref, n_off, qi - ki)
        s1 = _softmax_block(q1, k1_ref[pl.ds(start, t), :], bias, scale, causal)
        m1, l1, a1 = _flash_update(s1, v_blk, m1, l1, a1)
        s2 = _softmax_block(q2, k2_ref[pl.ds(start, t), :], bias, scale, causal)
        m2, l2, a2 = _flash_update(s2, v_blk, m2, l2, a2)
        return m1, l1, a1, m2, l2, a2

    neg = jnp.full((t, 1), NEG_BIG, _F32)
    zero1 = jnp.zeros((t, 1), _F32)
    zacc = jnp.zeros((t, HEAD_DIM), _F32)
    carry = step(qi, (neg, zero1, zacc, neg, zero1, zacc), True)
    carry = lax.fori_loop(0, qi, lambda ki, c: step(ki, c, False), carry)
    _, l1, a1, _, l2, a2 = carry

    lv = lam_ref[...]
    lam = (jnp.exp(jnp.sum(lv[0:1] * lv[1:2], axis=1, keepdims=True))
           - jnp.exp(jnp.sum(lv[2:3] * lv[3:4], axis=1, keepdims=True)) + lam_init)
    o = a1 / l1 - lam * (a2 / l2)
    o = o * lax.rsqrt(jnp.mean(o * o, axis=-1, keepdims=True) + EPS) * g_ref[...] * (1.0 - lam_init)
    o_ref[...] = (o * _silu(z_ref[...].astype(_F32))).astype(o_ref.dtype)


def _stick_kernel(q_ref, k_ref, v_ref, z_ref, o_ref):
    t = STICK_BLOCK
    qi = pl.program_id(2)
    q = q_ref[...]
    scale = HEAD_DIM ** -0.5
    row = lax.broadcasted_iota(jnp.int32, (t, t), 0)
    col = lax.broadcasted_iota(jnp.int32, (t, t), 1)
    later = jnp.where(row > col, 1.0, 0.0).astype(_BF)
    strict = col < row

    def step(ki, carry, diagonal):
        acc, tail0 = carry
        start = pl.multiple_of(ki * t, t)
        k_blk = k_ref[pl.ds(start, t), :]
        v_blk = v_ref[pl.ds(start, t), :]
        z = _dot_nt(q, k_blk) * scale
        softplus = jnp.log(1.0 + jnp.exp(-jnp.abs(z)))
        log_beta = jnp.minimum(z, 0.0) - softplus
        log_1m = log_beta - z
        if diagonal:
            log_1m = jnp.where(strict, log_1m, 0.0)
        hi, lo = _split_hi_lo(log_1m)
        tail = tail0 + _dot(hi, later) + _dot(lo, later)
        a = jnp.exp(log_beta + tail)
        if diagonal:
            a = jnp.where(strict, a, 0.0)
        acc = acc + _dot(a.astype(_BF), v_blk)
        tail0 = tail0 + jnp.sum(log_1m, axis=1, keepdims=True)
        return acc, tail0

    acc, tail0 = step(qi, (jnp.zeros((t, HEAD_DIM), _F32), jnp.zeros((t, 1), _F32)), True)

    def live(tail0):
        return (jnp.max(tail0) > STICK_SKIP_LOG).astype(jnp.int32)

    def cond(state):
        i, _, _, alive = state
        return jnp.logical_and(i < qi, alive > 0)

    def body(state):
        i, acc, tail0, _ = state
        acc, tail0 = step(qi - 1 - i, (acc, tail0), False)
        return i + 1, acc, tail0, live(tail0)

    _, acc, _, _ = lax.while_loop(cond, body, (jnp.int32(0), acc, tail0, live(tail0)))
    o_ref[...] = (acc * _silu(z_ref[...].astype(_F32))).astype(o_ref.dtype)


def _head_spec(t, col0, nq):
    return pl.BlockSpec((t, HEAD_DIM), lambda b, h, qi: (b * nq + qi, col0 + h))


def _seq_spec(seq, col0, per_block=1):
    return pl.BlockSpec((seq, HEAD_DIM), lambda b, h, qi: (b, col0 + h // per_block))


def _odd_mixer(proj, bsz, seq, d_inner):
    n_heads = d_inner // HEAD_DIM
    t = STICK_BLOCK
    nq = seq // t
    return pl.pallas_call(
        _stick_kernel,
        grid=(bsz, n_heads, nq),
        in_specs=[
            _head_spec(t, 0, nq),
            _seq_spec(seq, n_heads),
            _seq_spec(seq, 2 * n_heads),
            _head_spec(t, 3 * n_heads, nq),
        ],
        out_specs=_head_spec(t, 0, nq),
        out_shape=jax.ShapeDtypeStruct((bsz * seq, d_inner), _BF),
        compiler_params=_cparams(("arbitrary", "arbitrary", "arbitrary")),
        name="stick_breaking",
    )(proj, proj, proj, proj)


def _even_mixer(proj, bias_a, bias_b, lam_vec, subln_g, lam_init, bsz, seq, d_inner):
    d_a = d_inner // 2
    h_a = d_a // HEAD_DIM
    h_b = (d_inner - d_a) // HEAD_DIM
    t = SM_BLOCK
    nq = seq // t
    n_off_a = bias_a.shape[2] // t - 1
    n_off_b = bias_b.shape[2] // t - 1
    out_shape = jax.ShapeDtypeStruct((bsz * seq, d_a), _BF)
    params = _cparams(("arbitrary", "arbitrary", "arbitrary"))
    z0 = 3 * d_inner // HEAD_DIM

    o_a = pl.pallas_call(
        functools.partial(_dilated_kernel, n_off=n_off_a),
        grid=(bsz, h_a, nq),
        in_specs=[
            _head_spec(t, 0, nq),
            _seq_spec(seq, h_a),
            _seq_spec(seq, 2 * h_a),
            _head_spec(t, z0, nq),
            pl.BlockSpec((1, t, (n_off_a + 1) * t), lambda b, h, qi: (h, 0, 0)),
        ],
        out_specs=_head_spec(t, 0, nq),
        out_shape=out_shape,
        compiler_params=params,
        name="dilated_attention",
    )(proj, proj, proj, proj, bias_a)

    c0 = 3 * h_a
    half = h_b // 2
    o_b = pl.pallas_call(
        functools.partial(_diff_kernel, lam_init=lam_init, n_off=n_off_b),
        grid=(bsz, h_b, nq),
        in_specs=[
            pl.BlockSpec((t, HEAD_DIM), lambda b, h, qi: (b * nq + qi, c0 + h // 2)),
            pl.BlockSpec((t, HEAD_DIM), lambda b, h, qi: (b * nq + qi, c0 + half + h // 2)),
            _seq_spec(seq, c0 + 2 * half, 2),
            _seq_spec(seq, c0 + 3 * half, 2),
            _seq_spec(seq, c0 + 4 * half),
            _head_spec(t, z0 + h_a, nq),
            pl.BlockSpec((1, t, (n_off_b + 1) * t), lambda b, h, qi: (h, 0, 0)),
            pl.BlockSpec((4, DIFF_QK_DIM), lambda b, h, qi: (0, 0)),
            pl.BlockSpec((1, HEAD_DIM), lambda b, h, qi: (0, 0)),
        ],
        out_specs=_head_spec(t, 0, nq),
        out_shape=out_shape,
        compiler_params=params,
        name="diff_attention",
    )(proj, proj, proj, proj, proj, proj, bias_b, lam_vec, subln_g.reshape(1, HEAD_DIM))
    return jnp.concatenate([o_a, o_b], axis=1)


def kernel(x, c, norm_g, w_mod, b_mod, w_in, w_out, rel_bias, diff_lambda, diff_subln_g, final_norm_g):
    bsz, seq, d = x.shape
    depth = w_in.shape[0]
    d_inner = w_out.shape[1]
    h_a = d_inner // 2 // HEAD_DIM
    n_blk = seq // SM_BLOCK

    mod = _modulation(c, w_mod, b_mod)

    table = rel_bias[_t5_bucket(jnp.arange(seq))].T
    n_off_a = min(n_blk, DILATED_PATTERNS[-1][0] // SM_BLOCK + 1)
    vec_a = table[:h_a, : n_off_a * SM_BLOCK] + _dilated_log_multiplicity(n_off_a * SM_BLOCK)[None]
    bias_a = _skewed_bias(vec_a, n_off_a)
    bias_b = _skewed_bias(table[h_a:], n_blk)

    h = x.reshape(bsz * seq, d)
    for layer in range(depth):
        shift = mod[layer, :, 0:d].reshape(bsz, 1, d)
        scale = mod[layer, :, d:2 * d].reshape(bsz, 1, d)
        gate = mod[layer, :, 2 * d:].reshape(bsz, 1, d)
        u = _norm_mod(h, norm_g[layer], scale, shift, seq)
        proj = _in_proj(u, w_in, layer)
        if layer % 2 == 0:
            e = layer // 2
            lam_init = 0.8 - 0.6 * math.exp(-0.3 * layer)
            mixed = _even_mixer(proj, bias_a, bias_b, diff_lambda[e], diff_subln_g[e], lam_init,
                                bsz, seq, d_inner)
        else:
            mixed = _odd_mixer(proj, bsz, seq, d_inner)
        h = _out_proj(mixed, w_out, layer, h, gate, seq)
    return _final_norm(h, final_norm_g).reshape(bsz, seq, d)
```

```python
import functools
import math

import jax
import jax.numpy as jnp
from jax import lax
from jax.experimental import pallas as pl
from jax.experimental.pallas import tpu as pltpu

HEAD_DIM = 128
DIFF_QK_DIM = 64
EXPAND = 2
N_BUCKETS = 32
MAX_DISTANCE = 2048
DILATED_PATTERNS = ((128, 1), (512, 4), (2048, 16))
EPS = 1e-6
SM_BLOCK = 512
STICK_BLOCK = 256
STICK_SKIP_LOG = -104.0
NEG_BIG = -1e30
VMEM_LIMIT = 56 * 1024 * 1024

_BF = jnp.bfloat16
_F32 = jnp.float32


def _cparams(sem):
    return pltpu.CompilerParams(dimension_semantics=sem, vmem_limit_bytes=VMEM_LIMIT)


def _dot(a, b):
    return jnp.dot(a, b, preferred_element_type=_F32)


def _dot_nt(a, b):
    return lax.dot_general(a, b, (((1,), (1,)), ((), ())), preferred_element_type=_F32)


def _split_hi_lo(x):
    hi = x.astype(_BF)
    lo = (x - hi.astype(_F32)).astype(_BF)
    return hi, lo


def _silu(z):
    return z / (1.0 + jnp.exp(-z))


def _mod_kernel(c_ref, w_ref, b_ref, o_ref):
    a_hi, a_lo = _split_hi_lo(_silu(c_ref[...]))
    w_hi, w_lo = _split_hi_lo(w_ref[0])
    acc = _dot(a_hi, w_hi) + _dot(a_lo, w_hi) + _dot(a_hi, w_lo)
    o_ref[0] = acc + b_ref[0]


def _modulation(c, w_mod, b_mod):
    depth, d, n = w_mod.shape
    bsz = c.shape[0]
    rows = 8
    tn = min(1024, n)
    c_pad = jnp.zeros((rows, d), _F32).at[:bsz].set(c)
    out = pl.pallas_call(
        _mod_kernel,
        grid=(depth, n // tn),
        in_specs=[
            pl.BlockSpec((rows, d), lambda l, j: (0, 0)),
            pl.BlockSpec((1, d, tn), lambda l, j: (l, 0, j)),
            pl.BlockSpec((1, 1, tn), lambda l, j: (l, 0, j)),
        ],
        out_specs=pl.BlockSpec((1, rows, tn), lambda l, j: (l, 0, j)),
        out_shape=jax.ShapeDtypeStruct((depth, rows, n), _F32),
        compiler_params=_cparams(("arbitrary", "arbitrary")),
        name="modulation",
    )(c_pad, w_mod, b_mod.reshape(depth, 1, n))
    return out[:, :bsz]


def _norm_mod_kernel(h_ref, g_ref, scale_ref, shift_ref, o_ref):
    h = h_ref[...]
    inv = lax.rsqrt(jnp.mean(h * h, axis=-1, keepdims=True) + EPS)
    gain = g_ref[...] * (1.0 + scale_ref[0])
    o_ref[...] = (h * inv * gain + shift_ref[0]).astype(o_ref.dtype)


def _norm_mod(h, g, scale, shift, seq):
    m, d = h.shape
    tm = 512
    per_b = seq // tm
    return pl.pallas_call(
        _norm_mod_kernel,
        grid=(m // tm,),
        in_specs=[
            pl.BlockSpec((tm, d), lambda i: (i, 0)),
            pl.BlockSpec((1, d), lambda i: (0, 0)),
            pl.BlockSpec((1, 1, d), lambda i: (i // per_b, 0, 0)),
            pl.BlockSpec((1, 1, d), lambda i: (i // per_b, 0, 0)),
        ],
        out_specs=pl.BlockSpec((tm, d), lambda i: (i, 0)),
        out_shape=jax.ShapeDtypeStruct((m, d), _BF),
        compiler_params=_cparams(("arbitrary",)),
        name="norm_mod",
    )(h, g.reshape(1, d), scale, shift)


def _final_norm_kernel(h_ref, g_ref, o_ref):
    h = h_ref[...]
    inv = lax.rsqrt(jnp.mean(h * h, axis=-1, keepdims=True) + EPS)
    o_ref[...] = h * inv * g_ref[...]


def _final_norm(h, g):
    m, d = h.shape
    tm = 512
    return pl.pallas_call(
        _final_norm_kernel,
        grid=(m // tm,),
        in_specs=[pl.BlockSpec((tm, d), lambda i: (i, 0)), pl.BlockSpec((1, d), lambda i: (0, 0))],
        out_specs=pl.BlockSpec((tm, d), lambda i: (i, 0)),
        out_shape=jax.ShapeDtypeStruct((m, d), _F32),
        compiler_params=_cparams(("arbitrary",)),
        name="final_norm",
    )(h, g.reshape(1, d))


def _proj_kernel(u_ref, w_ref, o_ref, wbf_ref):
    @pl.when(pl.program_id(1) == 0)
    def _():
        wbf_ref[...] = w_ref[0].astype(_BF)

    o_ref[...] = _dot(u_ref[...], wbf_ref[...]).astype(o_ref.dtype)


def _in_proj(u, w_in, layer):
    m, d = u.shape
    p = w_in.shape[2]
    tm, tn = min(1024, m), min(1024, p)
    return pl.pallas_call(
        _proj_kernel,
        grid=(p // tn, m // tm),
        in_specs=[
            pl.BlockSpec((tm, d), lambda j, i: (i, 0)),
            pl.BlockSpec((1, d, tn), lambda j, i: (layer, 0, j)),
        ],
        out_specs=pl.BlockSpec((tm, tn), lambda j, i: (i, j)),
        out_shape=jax.ShapeDtypeStruct((m, p), _BF),
        scratch_shapes=[pltpu.VMEM((d, tn), _BF)],
        compiler_params=_cparams(("arbitrary", "arbitrary")),
        name="in_proj",
    )(u, w_in)


def _out_kernel(a_ref, w_ref, h_ref, gate_ref, o_ref, wbf_ref):
    @pl.when(pl.program_id(1) == 0)
    def _():
        wbf_ref[...] = w_ref[0].astype(_BF)

    y = _dot(a_ref[...], wbf_ref[...])
    o_ref[...] = h_ref[...] + gate_ref[0] * y


def _out_proj(a, w_out, layer, h, gate, seq):
    m, k = a.shape
    d = w_out.shape[2]
    tm, tn = 512, min(512, d)
    per_b = seq // tm
    return pl.pallas_call(
        _out_kernel,
        grid=(d // tn, m // tm),
        in_specs=[
            pl.BlockSpec((tm, k), lambda j, i: (i, 0)),
            pl.BlockSpec((1, k, tn), lambda j, i: (layer, 0, j)),
            pl.BlockSpec((tm, tn), lambda j, i: (i, j)),
            pl.BlockSpec((1, 1, tn), lambda j, i: (i // per_b, 0, j)),
        ],
        out_specs=pl.BlockSpec((tm, tn), lambda j, i: (i, j)),
        out_shape=jax.ShapeDtypeStruct((m, d), _F32),
        scratch_shapes=[pltpu.VMEM((k, tn), _BF)],
        compiler_params=_cparams(("arbitrary", "arbitrary")),
        name="out_proj",
    )(a, w_out, h, gate)


def _t5_bucket(dist):
    max_exact = N_BUCKETS // 2
    d = jnp.maximum(dist, 1).astype(_F32)
    large = max_exact + (jnp.log(d / max_exact) / math.log(MAX_DISTANCE / max_exact)
                         * (N_BUCKETS - max_exact)).astype(jnp.int32)
    large = jnp.minimum(large, N_BUCKETS - 1)
    return jnp.where(dist < max_exact, dist, large)


def _dilated_log_multiplicity(n):
    delta = jnp.arange(n)
    mult = jnp.zeros((n,), _F32)
    for window, dil in DILATED_PATTERNS:
        mult = mult + ((delta % dil == 0) & (delta <= window)).astype(_F32)
    return jnp.where(mult > 0, jnp.log(jnp.maximum(mult, 1.0)), NEG_BIG)


def _skewed_bias(bias_vec, n_off):
    t = SM_BLOCK
    h = bias_vec.shape[0]
    c = n_off * t
    width = c + t
    rr = jnp.take(bias_vec, jnp.clip(c - jnp.arange(width), 0, c - 1), axis=1)
    return pl.pallas_call(
        _skew_kernel,
        grid=(h,),
        in_specs=[pl.BlockSpec((1, 1, width), lambda i: (i, 0, 0))],
        out_specs=pl.BlockSpec((1, t, width), lambda i: (i, 0, 0)),
        out_shape=jax.ShapeDtypeStruct((h, t, width), _F32),
        compiler_params=_cparams(("arbitrary",)),
        name="bias_skew",
    )(rr.reshape(h, 1, width))


def _skew_kernel(rr_ref, o_ref):
    _, t, width = o_ref.shape
    rows = 8

    def body(g, carry):
        r0 = pl.multiple_of(g * rows, rows)
        x = jnp.broadcast_to(rr_ref[0], (rows, width))
        o_ref[0, pl.ds(r0, rows), :] = pltpu.roll(x, r0, 1, stride=1, stride_axis=0)
        return carry

    lax.fori_loop(0, t // rows, body, 0)


def _softmax_block(q, k_blk, bias, scale, causal):
    s = _dot_nt(q, k_blk) * scale + bias
    if causal:
        t = SM_BLOCK
        row = lax.broadcasted_iota(jnp.int32, (t, t), 0)
        col = lax.broadcasted_iota(jnp.int32, (t, t), 1)
        s = jnp.where(col <= row, s, NEG_BIG)
    return s


def _flash_update(s, v_blk, m, l, acc):
    m_new = jnp.maximum(m, jnp.max(s, axis=1, keepdims=True))
    alpha = jnp.exp(m - m_new)
    p = jnp.exp(s - m_new)
    l_new = l * alpha + jnp.sum(p, axis=1, keepdims=True)
    acc_new = acc * alpha + _dot(p.astype(_BF), v_blk)
    return m_new, l_new, acc_new


def _bias_tile(bias_ref, n_off, off):
    t = SM_BLOCK
    return bias_ref[0, :, pl.ds(pl.multiple_of((n_off - off) * t, t), t)]


def _dilated_kernel(q_ref, k_ref, v_ref, z_ref, bias_ref, o_ref, *, n_off):
    t = SM_BLOCK
    qi = pl.program_id(2)
    q = q_ref[...]
    scale = HEAD_DIM ** -0.5

    def step(ki, carry, causal):
        m, l, acc = carry
        start = pl.multiple_of(ki * t, t)
        k_blk = k_ref[pl.ds(start, t), :]
        v_blk = v_ref[pl.ds(start, t), :]
        s = _softmax_block(q, k_blk, _bias_tile(bias_ref, n_off, qi - ki), scale, causal)
        return _flash_update(s, v_blk, m, l, acc)

    init = (jnp.full((t, 1), NEG_BIG, _F32), jnp.zeros((t, 1), _F32), jnp.zeros((t, HEAD_DIM), _F32))
    carry = step(qi, init, True)
    lo = jnp.maximum(qi - (n_off - 1), 0)
    carry = lax.fori_loop(lo, qi, lambda ki, c: step(ki, c, False), carry)
    _, l, acc = carry
    o_ref[...] = (acc / l * _silu(z_ref[...].astype(_F32))).astype(o_ref.dtype)


def _diff_kernel(q1_ref, q2_ref, k1_ref, k2_ref, v_ref, z_ref, bias_ref, lam_ref, g_ref, o_ref, *, lam_init, n_off):
    t = SM_BLOCK
    h = pl.program_id(1)
    qi = pl.program_id(2)
    scale = DIFF_QK_DIM ** -0.5
    lane = lax.broadcasted_iota(jnp.int32, (t, HEAD_DIM), 1)
    mine = (lane // DIFF_QK_DIM) == (h % 2)
    q1 = jnp.where(mine, q1_ref[...], jnp.zeros_like(q1_ref[...]))
    q2 = jnp.where(mine, q2_ref[...], jnp.zeros_like(q2_ref[...]))

    def step(ki, carry, causal):
        m1, l1, a1, m2, l2, a2 = carry
        start = pl.multiple_of(ki * t, t)
        v_blk = v_ref[pl.ds(start, t), :]
        bias = _bias_tile(bias_ref, n_off, qi - ki)
        s1 = _softmax_block(q1, k1_ref[pl.ds(start, t), :], bias, scale, causal)
        m1, l1, a1 = _flash_update(s1, v_blk, m1, l1, a1)
        s2 = _softmax_block(q2, k2_ref[pl.ds(start, t), :], bias, scale, causal)
        m2, l2, a2 = _flash_update(s2, v_blk, m2, l2, a2)
        return m1, l1, a1, m2, l2, a2

    neg = jnp.full((t, 1), NEG_BIG, _F32)
    zero1 = jnp.zeros((t, 1), _F32)
    zacc = jnp.zeros((t, HEAD_DIM), _F32)
    carry = step(qi, (neg, zero1, zacc, neg, zero1, zacc), True)
    carry = lax.fori_loop(0, qi, lambda ki, c: step(ki, c, False), carry)
    _, l1, a1, _, l2, a2 = carry

    lv = lam_ref[...]
    lam = (jnp.exp(jnp.sum(lv[0:1] * lv[1:2], axis=1, keepdims=True))
           - jnp.exp(jnp.sum(lv[2:3] * lv[3:4], axis=1, keepdims=True)) + lam_init)
    o = a1 / l1 - lam * (a2 / l2)
    o = o * lax.rsqrt(jnp.mean(o * o, axis=-1, keepdims=True) + EPS) * g_ref[...] * (1.0 - lam_init)
    o_ref[...] = (o * _silu(z_ref[...].astype(_F32))).astype(o_ref.dtype)


def _stick_kernel(q_ref, k_ref, v_ref, z_ref, o_ref):
    t = STICK_BLOCK
    qi = pl.program_id(2)
    q = q_ref[...]
    scale = HEAD_DIM ** -0.5
    row = lax.broadcasted_iota(jnp.int32, (t, t), 0)
    col = lax.broadcasted_iota(jnp.int32, (t, t), 1)
    later = jnp.where(row > col, 1.0, 0.0).astype(_BF)
    strict = col < row

    def step(ki, carry, diagonal):
        acc, tail0 = carry
        start = pl.multiple_of(ki * t, t)
        k_blk = k_ref[pl.ds(start, t), :]
        v_blk = v_ref[pl.ds(start, t), :]
        z = _dot_nt(q, k_blk) * scale
        softplus = jnp.log(1.0 + jnp.exp(-jnp.abs(z)))
        log_beta = jnp.minimum(z, 0.0) - softplus
        log_1m = log_beta - z
        if diagonal:
            log_1m = jnp.where(strict, log_1m, 0.0)
        hi, lo = _split_hi_lo(log_1m)
        tail = tail0 + _dot(hi, later) + _dot(lo, later)
        a = jnp.exp(log_beta + tail)
        if diagonal:
            a = jnp.where(strict, a, 0.0)
        acc = acc + _dot(a.astype(_BF), v_blk)
        tail0 = tail0 + jnp.sum(log_1m, axis=1, keepdims=True)
        return acc, tail0

    acc, tail0 = step(qi, (jnp.zeros((t, HEAD_DIM), _F32), jnp.zeros((t, 1), _F32)), True)

    def live(tail0):
        return (jnp.max(tail0) > STICK_SKIP_LOG).astype(jnp.int32)

    def cond(state):
        i, _, _, alive = state
        return jnp.logical_and(i < qi, alive > 0)

    def body(state):
        i, acc, tail0, _ = state
        acc, tail0 = step(qi - 1 - i, (acc, tail0), False)
        return i + 1, acc, tail0, live(tail0)

    _, acc, _, _ = lax.while_loop(cond, body, (jnp.int32(0), acc, tail0, live(tail0)))
    o_ref[...] = (acc * _silu(z_ref[...].astype(_F32))).astype(o_ref.dtype)


def _head_spec(t, col0, nq):
    return pl.BlockSpec((t, HEAD_DIM), lambda b, h, qi: (b * nq + qi, col0 + h))


def _seq_spec(seq, col0, per_block=1):
    return pl.BlockSpec((seq, HEAD_DIM), lambda b, h, qi: (b, col0 + h // per_block))


def _odd_mixer(proj, bsz, seq, d_inner):
    n_heads = d_inner // HEAD_DIM
    t = STICK_BLOCK
    nq = seq // t
    return pl.pallas_call(
        _stick_kernel,
        grid=(bsz, n_heads, nq),
        in_specs=[
            _head_spec(t, 0, nq),
            _seq_spec(seq, n_heads),
            _seq_spec(seq, 2 * n_heads),
            _head_spec(t, 3 * n_heads, nq),
        ],
        out_specs=_head_spec(t, 0, nq),
        out_shape=jax.ShapeDtypeStruct((bsz * seq, d_inner), _BF),
        compiler_params=_cparams(("arbitrary", "arbitrary", "arbitrary")),
        name="stick_breaking",
    )(proj, proj, proj, proj)


def _even_mixer(proj, bias_a, bias_b, lam_vec, subln_g, lam_init, bsz, seq, d_inner):
    d_a = d_inner // 2
    h_a = d_a // HEAD_DIM
    h_b = (d_inner - d_a) // HEAD_DIM
    t = SM_BLOCK
    nq = seq // t
    n_off_a = bias_a.shape[2] // t - 1
    n_off_b = bias_b.shape[2] // t - 1
    out_shape = jax.ShapeDtypeStruct((bsz * seq, d_a), _BF)
    params = _cparams(("arbitrary", "arbitrary", "arbitrary"))
    z0 = 3 * d_inner // HEAD_DIM

    o_a = pl.pallas_call(
        functools.partial(_dilated_kernel, n_off=n_off_a),
        grid=(bsz, h_a, nq),
        in_specs=[
            _head_spec(t, 0, nq),
            _seq_spec(seq, h_a),
            _seq_spec(seq, 2 * h_a),
            _head_spec(t, z0, nq),
            pl.BlockSpec((1, t, (n_off_a + 1) * t), lambda b, h, qi: (h, 0, 0)),
        ],
        out_specs=_head_spec(t, 0, nq),
        out_shape=out_shape,
        compiler_params=params,
        name="dilated_attention",
    )(proj, proj, proj, proj, bias_a)

    c0 = 3 * h_a
    half = h_b // 2
    o_b = pl.pallas_call(
        functools.partial(_diff_kernel, lam_init=lam_init, n_off=n_off_b),
        grid=(bsz, h_b, nq),
        in_specs=[
            pl.BlockSpec((t, HEAD_DIM), lambda b, h, qi: (b * nq + qi, c0 + h // 2)),
            pl.BlockSpec((t, HEAD_DIM), lambda b, h, qi: (b * nq + qi, c0 + half + h // 2)),
            _seq_spec(seq, c0 + 2 * half, 2),
            _seq_spec(seq, c0 + 3 * half, 2),
            _seq_spec(seq, c0 + 4 * half),
            _head_spec(t, z0 + h_a, nq),
            pl.BlockSpec((1, t, (n_off_b + 1) * t), lambda b, h, qi: (h, 0, 0)),
            pl.BlockSpec((4, DIFF_QK_DIM), lambda b, h, qi: (0, 0)),
            pl.BlockSpec((1, HEAD_DIM), lambda b, h, qi: (0, 0)),
        ],
        out_specs=_head_spec(t, 0, nq),
        out_shape=out_shape,
        compiler_params=params,
        name="diff_attention",
    )(proj, proj, proj, proj, proj, proj, bias_b, lam_vec, subln_g.reshape(1, HEAD_DIM))
    return jnp.concatenate([o_a, o_b], axis=1)


def kernel(x, c, norm_g, w_mod, b_mod, w_in, w_out, rel_bias, diff_lambda, diff_subln_g, final_norm_g):
    bsz, seq, d = x.shape
    depth = w_in.shape[0]
    d_inner = w_out.shape[1]
    h_a = d_inner // 2 // HEAD_DIM
    n_blk = seq // SM_BLOCK

    mod = _modulation(c, w_mod, b_mod)

    table = rel_bias[_t5_bucket(jnp.arange(seq))].T
    n_off_a = min(n_blk, DILATED_PATTERNS[-1][0] // SM_BLOCK + 1)
    vec_a = table[:h_a, : n_off_a * SM_BLOCK] + _dilated_log_multiplicity(n_off_a * SM_BLOCK)[None]
    bias_a = _skewed_bias(vec_a, n_off_a)
    bias_b = _skewed_bias(table[h_a:], n_blk)

    h = x.reshape(bsz * seq, d)
    for layer in range(depth):
        shift = mod[layer, :, 0:d].reshape(bsz, 1, d)
        scale = mod[layer, :, d:2 * d].reshape(bsz, 1, d)
        gate = mod[layer, :, 2 * d:].reshape(bsz, 1, d)
        u = _norm_mod(h, norm_g[layer], scale, shift, seq)
        proj = _in_proj(u, w_in, layer)
        if layer % 2 == 0:
            e = layer // 2
            lam_init = 0.8 - 0.6 * math.exp(-0.3 * layer)
            mixed = _even_mixer(proj, bias_a, bias_b, diff_lambda[e], diff_subln_g[e], lam_init,
                                bsz, seq, d_inner)
        else:
            mixed = _odd_mixer(proj, bsz, seq, d_inner)
        h = _out_proj(mixed, w_out, layer, h, gate, seq)
    return _final_norm(h, final_norm_g).reshape(bsz, seq, d)
```

```python
import functools
import math

import jax
import jax.numpy as jnp
from jax import lax
from jax.experimental import pallas as pl
from jax.experimental.pallas import tpu as pltpu

HEAD_DIM = 128
DIFF_QK_DIM = 64
EXPAND = 2
N_BUCKETS = 32
MAX_DISTANCE = 2048
DILATED_PATTERNS = ((128, 1), (512, 4), (2048, 16))
EPS = 1e-6
SM_BLOCK = 512
STICK_BLOCK = 256
STICK_SKIP_LOG = -104.0
NEG_BIG = -1e30
LOG2E = math.log2(math.e)
VMEM_LIMIT = 56 * 1024 * 1024

_BF = jnp.bfloat16
_F32 = jnp.float32


def _cparams(sem):
    return pltpu.CompilerParams(dimension_semantics=sem, vmem_limit_bytes=VMEM_LIMIT)


def _dot(a, b):
    return jnp.dot(a, b, preferred_element_type=_F32)


def _dot_nt(a, b):
    return lax.dot_general(a, b, (((1,), (1,)), ((), ())), preferred_element_type=_F32)


def _split_hi_lo(x):
    hi = x.astype(_BF)
    lo = (x - hi.astype(_F32)).astype(_BF)
    return hi, lo


def _silu(z):
    return z / (1.0 + jnp.exp(-z))


def _mod_kernel(c_ref, w_ref, b_ref, o_ref):
    a_hi, a_lo = _split_hi_lo(_silu(c_ref[...]))
    w_hi, w_lo = _split_hi_lo(w_ref[0])
    acc = _dot(a_hi, w_hi) + _dot(a_lo, w_hi) + _dot(a_hi, w_lo)
    o_ref[0] = acc + b_ref[0]


def _modulation(c, w_mod, b_mod):
    depth, d, n = w_mod.shape
    bsz = c.shape[0]
    rows = 8
    tn = min(1024, n)
    c_pad = jnp.zeros((rows, d), _F32).at[:bsz].set(c)
    out = pl.pallas_call(
        _mod_kernel,
        grid=(depth, n // tn),
        in_specs=[
            pl.BlockSpec((rows, d), lambda l, j: (0, 0)),
            pl.BlockSpec((1, d, tn), lambda l, j: (l, 0, j)),
            pl.BlockSpec((1, 1, tn), lambda l, j: (l, 0, j)),
        ],
        out_specs=pl.BlockSpec((1, rows, tn), lambda l, j: (l, 0, j)),
        out_shape=jax.ShapeDtypeStruct((depth, rows, n), _F32),
        compiler_params=_cparams(("arbitrary", "arbitrary")),
        name="modulation",
    )(c_pad, w_mod, b_mod.reshape(depth, 1, n))
    return out[:, :bsz]


def _norm_mod_kernel(h_ref, g_ref, scale_ref, shift_ref, o_ref):
    h = h_ref[...]
    inv = lax.rsqrt(jnp.mean(h * h, axis=-1, keepdims=True) + EPS)
    gain = g_ref[...] * (1.0 + scale_ref[0])
    o_ref[...] = (h * inv * gain + shift_ref[0]).astype(o_ref.dtype)


def _norm_mod(h, g, scale, shift, seq):
    m, d = h.shape
    tm = 512
    per_b = seq // tm
    return pl.pallas_call(
        _norm_mod_kernel,
        grid=(m // tm,),
        in_specs=[
            pl.BlockSpec((tm, d), lambda i: (i, 0)),
            pl.BlockSpec((1, d), lambda i: (0, 0)),
            pl.BlockSpec((1, 1, d), lambda i: (i // per_b, 0, 0)),
            pl.BlockSpec((1, 1, d), lambda i: (i // per_b, 0, 0)),
        ],
        out_specs=pl.BlockSpec((tm, d), lambda i: (i, 0)),
        out_shape=jax.ShapeDtypeStruct((m, d), _BF),
        compiler_params=_cparams(("arbitrary",)),
        name="norm_mod",
    )(h, g.reshape(1, d), scale, shift)


def _final_norm_kernel(h_ref, g_ref, o_ref):
    h = h_ref[...]
    inv = lax.rsqrt(jnp.mean(h * h, axis=-1, keepdims=True) + EPS)
    o_ref[...] = h * inv * g_ref[...]


def _final_norm(h, g):
    m, d = h.shape
    tm = 512
    return pl.pallas_call(
        _final_norm_kernel,
        grid=(m // tm,),
        in_specs=[pl.BlockSpec((tm, d), lambda i: (i, 0)), pl.BlockSpec((1, d), lambda i: (0, 0))],
        out_specs=pl.BlockSpec((tm, d), lambda i: (i, 0)),
        out_shape=jax.ShapeDtypeStruct((m, d), _F32),
        compiler_params=_cparams(("arbitrary",)),
        name="final_norm",
    )(h, g.reshape(1, d))


def _proj_kernel(u_ref, w_ref, o_ref, wbf_ref):
    @pl.when(pl.program_id(1) == 0)
    def _():
        wbf_ref[...] = w_ref[0].astype(_BF)

    o_ref[...] = _dot(u_ref[...], wbf_ref[...]).astype(o_ref.dtype)


def _in_proj(u, w_in, layer):
    m, d = u.shape
    p = w_in.shape[2]
    tm, tn = min(1024, m), min(1024, p)
    return pl.pallas_call(
        _proj_kernel,
        grid=(p // tn, m // tm),
        in_specs=[
            pl.BlockSpec((tm, d), lambda j, i: (i, 0)),
            pl.BlockSpec((1, d, tn), lambda j, i: (layer, 0, j)),
        ],
        out_specs=pl.BlockSpec((tm, tn), lambda j, i: (i, j)),
        out_shape=jax.ShapeDtypeStruct((m, p), _BF),
        scratch_shapes=[pltpu.VMEM((d, tn), _BF)],
        compiler_params=_cparams(("arbitrary", "arbitrary")),
        name="in_proj",
    )(u, w_in)


def _out_kernel(a_ref, w_ref, h_ref, gate_ref, o_ref, wbf_ref):
    @pl.when(pl.program_id(1) == 0)
    def _():
        wbf_ref[...] = w_ref[0].astype(_BF)

    y = _dot(a_ref[...], wbf_ref[...])
    o_ref[...] = h_ref[...] + gate_ref[0] * y


def _out_proj(a, w_out, layer, h, gate, seq):
    m, k = a.shape
    d = w_out.shape[2]
    tm, tn = 512, min(512, d)
    per_b = seq // tm
    return pl.pallas_call(
        _out_kernel,
        grid=(d // tn, m // tm),
        in_specs=[
            pl.BlockSpec((tm, k), lambda j, i: (i, 0)),
            pl.BlockSpec((1, k, tn), lambda j, i: (layer, 0, j)),
            pl.BlockSpec((tm, tn), lambda j, i: (i, j)),
            pl.BlockSpec((1, 1, tn), lambda j, i: (i // per_b, 0, j)),
        ],
        out_specs=pl.BlockSpec((tm, tn), lambda j, i: (i, j)),
        out_shape=jax.ShapeDtypeStruct((m, d), _F32),
        scratch_shapes=[pltpu.VMEM((k, tn), _BF)],
        compiler_params=_cparams(("arbitrary", "arbitrary")),
        name="out_proj",
    )(a, w_out, h, gate)


def _t5_bucket(dist):
    max_exact = N_BUCKETS // 2
    d = jnp.maximum(dist, 1).astype(_F32)
    large = max_exact + (jnp.log(d / max_exact) / math.log(MAX_DISTANCE / max_exact)
                         * (N_BUCKETS - max_exact)).astype(jnp.int32)
    large = jnp.minimum(large, N_BUCKETS - 1)
    return jnp.where(dist < max_exact, dist, large)


def _dilated_log_multiplicity(n):
    delta = jnp.arange(n)
    mult = jnp.zeros((n,), _F32)
    for window, dil in DILATED_PATTERNS:
        mult = mult + ((delta % dil == 0) & (delta <= window)).astype(_F32)
    return jnp.where(mult > 0, jnp.log(jnp.maximum(mult, 1.0)), NEG_BIG)


def _skewed_bias(bias_vec, n_off):
    t = SM_BLOCK
    h = bias_vec.shape[0]
    c = n_off * t
    width = c + t
    rr = jnp.take(bias_vec, jnp.clip(jnp.arange(width) - t, 0, c - 1), axis=1)
    return pl.pallas_call(
        _skew_kernel,
        grid=(h,),
        in_specs=[pl.BlockSpec((1, 1, width), lambda i: (i, 0, 0))],
        out_specs=pl.BlockSpec((1, t, width), lambda i: (i, 0, 0)),
        out_shape=jax.ShapeDtypeStruct((h, t, width), _F32),
        compiler_params=_cparams(("arbitrary",)),
        name="bias_skew",
    )(rr.reshape(h, 1, width))


def _skew_kernel(rr_ref, o_ref):
    _, t, width = o_ref.shape
    rows = 8

    def body(g, carry):
        r0 = pl.multiple_of(g * rows, rows)
        x = jnp.broadcast_to(rr_ref[0], (rows, width))
        o_ref[0, pl.ds(r0, rows), :] = pltpu.roll(x, r0, 1, stride=1, stride_axis=0)
        return carry

    lax.fori_loop(0, t // rows, body, 0)


def _flash_step(qt, k_blk, vt_blk, bias, scale2, causal, m, l, acc):
    s = _dot(k_blk, qt) * scale2 + bias
    if causal:
        t = SM_BLOCK
        key = lax.broadcasted_iota(jnp.int32, (t, t), 0)
        qry = lax.broadcasted_iota(jnp.int32, (t, t), 1)
        s = jnp.where(key <= qry, s, NEG_BIG)
    m_new = jnp.maximum(m, jnp.max(s, axis=0, keepdims=True))
    alpha = jnp.exp2(m - m_new)
    p = jnp.exp2(s - m_new)
    l_new = l * alpha + jnp.sum(p, axis=0, keepdims=True)
    acc_new = acc * alpha + _dot(vt_blk, p.astype(_BF))
    return m_new, l_new, acc_new


def _flash_init():
    t = SM_BLOCK
    return jnp.full((1, t), NEG_BIG, _F32), jnp.zeros((1, t), _F32), jnp.zeros((HEAD_DIM, t), _F32)


def _bias_tile(bias_ref, off):
    t = SM_BLOCK
    return bias_ref[0, :, pl.ds(pl.multiple_of((off + 1) * t, t), t)]


def _fill_v_transposed(v_ref, vt_ref):
    t = SM_BLOCK
    for c in range(v_ref.shape[0] // t):
        vt_ref[:, c * t:(c + 1) * t] = v_ref[c * t:(c + 1) * t, :].astype(_F32).T.astype(_BF)


def _dilated_kernel(q_ref, k_ref, v_ref, z_ref, bias_ref, o_ref, vt_ref, *, n_off):
    t = SM_BLOCK
    qi = pl.program_id(2)
    qt = q_ref[...].astype(_F32).T.astype(_BF)
    scale2 = HEAD_DIM ** -0.5 * LOG2E

    @pl.when(qi == 0)
    def _():
        _fill_v_transposed(v_ref, vt_ref)

    def step(ki, carry, causal):
        start = pl.multiple_of(ki * t, t)
        return _flash_step(qt, k_ref[pl.ds(start, t), :], vt_ref[:, pl.ds(start, t)],
                           _bias_tile(bias_ref, qi - ki), scale2, causal, *carry)

    carry = step(qi, _flash_init(), True)
    lo = jnp.maximum(qi - (n_off - 1), 0)
    carry = lax.fori_loop(lo, qi, lambda ki, c: step(ki, c, False), carry)
    _, l, acc = carry
    o = (acc / l).T
    o_ref[...] = (o * _silu(z_ref[...].astype(_F32))).astype(o_ref.dtype)


def _diff_kernel(q1_ref, q2_ref, k1_ref, k2_ref, v_ref, z_ref, bias_ref, lam_ref, g_ref, o_ref, vt_ref, *, lam_init):
    t = SM_BLOCK
    h = pl.program_id(1)
    qi = pl.program_id(2)
    scale2 = DIFF_QK_DIM ** -0.5 * LOG2E
    dim = lax.broadcasted_iota(jnp.int32, (HEAD_DIM, t), 0)
    mine = (dim // DIFF_QK_DIM) == (h % 2)
    q1t = jnp.where(mine, q1_ref[...].astype(_F32).T, 0.0).astype(_BF)
    q2t = jnp.where(mine, q2_ref[...].astype(_F32).T, 0.0).astype(_BF)

    @pl.when(qi == 0)
    def _():
        _fill_v_transposed(v_ref, vt_ref)

    def step(ki, carry, causal):
        start = pl.multiple_of(ki * t, t)
        vt_blk = vt_ref[:, pl.ds(start, t)]
        bias = _bias_tile(bias_ref, qi - ki)
        c1 = _flash_step(q1t, k1_ref[pl.ds(start, t), :], vt_blk, bias, scale2, causal, *carry[:3])
        c2 = _flash_step(q2t, k2_ref[pl.ds(start, t), :], vt_blk, bias, scale2, causal, *carry[3:])
        return c1 + c2

    carry = step(qi, _flash_init() + _flash_init(), True)
    carry = lax.fori_loop(0, qi, lambda ki, c: step(ki, c, False), carry)
    _, l1, a1, _, l2, a2 = carry

    lv = lam_ref[...]
    lam = (jnp.exp(jnp.sum(lv[0:1] * lv[1:2], axis=1, keepdims=True))
           - jnp.exp(jnp.sum(lv[2:3] * lv[3:4], axis=1, keepdims=True)) + lam_init)
    o = a1 / l1 - lam * (a2 / l2)
    o = (o * lax.rsqrt(jnp.mean(o * o, axis=0, keepdims=True) + EPS)).T
    o = o * (g_ref[...] * (1.0 - lam_init))
    o_ref[...] = (o * _silu(z_ref[...].astype(_F32))).astype(o_ref.dtype)


def _stick_kernel(q_ref, k_ref, v_ref, z_ref, o_ref):
    t = STICK_BLOCK
    qi = pl.program_id(2)
    q = q_ref[...]
    scale = HEAD_DIM ** -0.5
    row = lax.broadcasted_iota(jnp.int32, (t, t), 0)
    col = lax.broadcasted_iota(jnp.int32, (t, t), 1)
    later = jnp.where(row > col, 1.0, 0.0).astype(_BF)
    strict = col < row

    def step(ki, carry, diagonal):
        acc, tail0 = carry
        start = pl.multiple_of(ki * t, t)
        k_blk = k_ref[pl.ds(start, t), :]
        v_blk = v_ref[pl.ds(start, t), :]
        z = _dot_nt(q, k_blk) * scale
        softplus = jnp.log(1.0 + jnp.exp(-jnp.abs(z)))
        log_beta = jnp.minimum(z, 0.0) - softplus
        log_1m = log_beta - z
        if diagonal:
            log_1m = jnp.where(strict, log_1m, 0.0)
        hi, lo = _split_hi_lo(log_1m)
        tail = tail0 + _dot(hi, later) + _dot(lo, later)
        a = jnp.exp(log_beta + tail)
        if diagonal:
            a = jnp.where(strict, a, 0.0)
        acc = acc + _dot(a.astype(_BF), v_blk)
        tail0 = tail0 + jnp.sum(log_1m, axis=1, keepdims=True)
        return acc, tail0

    acc, tail0 = step(qi, (jnp.zeros((t, HEAD_DIM), _F32), jnp.zeros((t, 1), _F32)), True)

    def live(tail0):
        return (jnp.max(tail0) > STICK_SKIP_LOG).astype(jnp.int32)

    def cond(state):
        i, _, _, alive = state
        return jnp.logical_and(i < qi, alive > 0)

    def body(state):
        i, acc, tail0, _ = state
        acc, tail0 = step(qi - 1 - i, (acc, tail0), False)
        return i + 1, acc, tail0, live(tail0)

    _, acc, _, _ = lax.while_loop(cond, body, (jnp.int32(0), acc, tail0, live(tail0)))
    o_ref[...] = (acc * _silu(z_ref[...].astype(_F32))).astype(o_ref.dtype)


def _head_spec(t, col0, nq):
    return pl.BlockSpec((t, HEAD_DIM), lambda b, h, qi: (b * nq + qi, col0 + h))


def _seq_spec(seq, col0, per_block=1):
    return pl.BlockSpec((seq, HEAD_DIM), lambda b, h, qi: (b, col0 + h // per_block))


def _odd_mixer(proj, bsz, seq, d_inner):
    n_heads = d_inner // HEAD_DIM
    t = STICK_BLOCK
    nq = seq // t
    return pl.pallas_call(
        _stick_kernel,
        grid=(bsz, n_heads, nq),
        in_specs=[
            _head_spec(t, 0, nq),
            _seq_spec(seq, n_heads),
            _seq_spec(seq, 2 * n_heads),
            _head_spec(t, 3 * n_heads, nq),
        ],
        out_specs=_head_spec(t, 0, nq),
        out_shape=jax.ShapeDtypeStruct((bsz * seq, d_inner), _BF),
        compiler_params=_cparams(("arbitrary", "arbitrary", "arbitrary")),
        name="stick_breaking",
    )(proj, proj, proj, proj)


def _even_mixer(proj, bias_a, bias_b, lam_vec, subln_g, lam_init, bsz, seq, d_inner):
    d_a = d_inner // 2
    h_a = d_a // HEAD_DIM
    h_b = (d_inner - d_a) // HEAD_DIM
    t = SM_BLOCK
    nq = seq // t
    n_off_a = bias_a.shape[2] // t - 1
    n_off_b = bias_b.shape[2] // t - 1
    out_shape = jax.ShapeDtypeStruct((bsz * seq, d_a), _BF)
    params = _cparams(("arbitrary", "arbitrary", "arbitrary"))
    z0 = 3 * d_inner // HEAD_DIM

    o_a = pl.pallas_call(
        functools.partial(_dilated_kernel, n_off=n_off_a),
        grid=(bsz, h_a, nq),
        in_specs=[
            _head_spec(t, 0, nq),
            _seq_spec(seq, h_a),
            _seq_spec(seq, 2 * h_a),
            _head_spec(t, z0, nq),
            pl.BlockSpec((1, t, (n_off_a + 1) * t), lambda b, h, qi: (h, 0, 0)),
        ],
        out_specs=_head_spec(t, 0, nq),
        out_shape=out_shape,
        scratch_shapes=[pltpu.VMEM((HEAD_DIM, seq), _BF)],
        compiler_params=params,
        name="dilated_attention",
    )(proj, proj, proj, proj, bias_a)

    c0 = 3 * h_a
    half = h_b // 2
    o_b = pl.pallas_call(
        functools.partial(_diff_kernel, lam_init=lam_init),
        grid=(bsz, h_b, nq),
        in_specs=[
            pl.BlockSpec((t, HEAD_DIM), lambda b, h, qi: (b * nq + qi, c0 + h // 2)),
            pl.BlockSpec((t, HEAD_DIM), lambda b, h, qi: (b * nq + qi, c0 + half + h // 2)),
            _seq_spec(seq, c0 + 2 * half, 2),
            _seq_spec(seq, c0 + 3 * half, 2),
            _seq_spec(seq, c0 + 4 * half),
            _head_spec(t, z0 + h_a, nq),
            pl.BlockSpec((1, t, (n_off_b + 1) * t), lambda b, h, qi: (h, 0, 0)),
            pl.BlockSpec((4, DIFF_QK_DIM), lambda b, h, qi: (0, 0)),
            pl.BlockSpec((1, HEAD_DIM), lambda b, h, qi: (0, 0)),
        ],
        out_specs=_head_spec(t, 0, nq),
        out_shape=out_shape,
        scratch_shapes=[pltpu.VMEM((HEAD_DIM, seq), _BF)],
        compiler_params=params,
        name="diff_attention",
    )(proj, proj, proj, proj, proj, proj, bias_b, lam_vec, subln_g.reshape(1, HEAD_DIM))
    return jnp.concatenate([o_a, o_b], axis=1)


def kernel(x, c, norm_g, w_mod, b_mod, w_in, w_out, rel_bias, diff_lambda, diff_subln_g, final_norm_g):
    bsz, seq, d = x.shape
    depth = w_in.shape[0]
    d_inner = w_out.shape[1]
    h_a = d_inner // 2 // HEAD_DIM
    n_blk = seq // SM_BLOCK

    mod = _modulation(c, w_mod, b_mod)

    table = rel_bias[_t5_bucket(jnp.arange(seq))].T
    n_off_a = min(n_blk, DILATED_PATTERNS[-1][0] // SM_BLOCK + 1)
    vec_a = table[:h_a, : n_off_a * SM_BLOCK] + _dilated_log_multiplicity(n_off_a * SM_BLOCK)[None]
    bias_a = _skewed_bias(vec_a * LOG2E, n_off_a)
    bias_b = _skewed_bias(table[h_a:] * LOG2E, n_blk)

    h = x.reshape(bsz * seq, d)
    for layer in range(depth):
        shift = mod[layer, :, 0:d].reshape(bsz, 1, d)
        scale = mod[layer, :, d:2 * d].reshape(bsz, 1, d)
        gate = mod[layer, :, 2 * d:].reshape(bsz, 1, d)
        u = _norm_mod(h, norm_g[layer], scale, shift, seq)
        proj = _in_proj(u, w_in, layer)
        if layer % 2 == 0:
            e = layer // 2
            lam_init = 0.8 - 0.6 * math.exp(-0.3 * layer)
            mixed = _even_mixer(proj, bias_a, bias_b, diff_lambda[e], diff_subln_g[e], lam_init,
                                bsz, seq, d_inner)
        else:
            mixed = _odd_mixer(proj, bsz, seq, d_inner)
        h = _out_proj(mixed, w_out, layer, h, gate, seq)
    return _final_norm(h, final_norm_g).reshape(bsz, seq, d)
```

```python
import functools
import math

import jax
import jax.numpy as jnp
from jax import lax
from jax.experimental import pallas as pl
from jax.experimental.pallas import tpu as pltpu

HEAD_DIM = 128
DIFF_QK_DIM = 64
EXPAND = 2
N_BUCKETS = 32
MAX_DISTANCE = 2048
DILATED_PATTERNS = ((128, 1), (512, 4), (2048, 16))
EPS = 1e-6
SM_BLOCK = 512
STICK_BLOCK = 256
STICK_SKIP_LOG = -104.0
NEG_BIG = -1e30
LOG2E = math.log2(math.e)
VMEM_LIMIT = 56 * 1024 * 1024

_BF = jnp.bfloat16
_F32 = jnp.float32


def _cparams(sem):
    return pltpu.CompilerParams(dimension_semantics=sem, vmem_limit_bytes=VMEM_LIMIT)


def _dot(a, b):
    return jnp.dot(a, b, preferred_element_type=_F32)


def _dot_nt(a, b):
    return lax.dot_general(a, b, (((1,), (1,)), ((), ())), preferred_element_type=_F32)


def _split_hi_lo(x):
    hi = x.astype(_BF)
    lo = (x - hi.astype(_F32)).astype(_BF)
    return hi, lo


def _silu(z):
    return z / (1.0 + jnp.exp(-z))


def _mod_kernel(c_ref, w_ref, b_ref, o_ref):
    a_hi, a_lo = _split_hi_lo(_silu(c_ref[...]))
    w_hi, w_lo = _split_hi_lo(w_ref[0])
    acc = _dot(a_hi, w_hi) + _dot(a_lo, w_hi) + _dot(a_hi, w_lo)
    o_ref[0] = acc + b_ref[0]


def _modulation(c, w_mod, b_mod):
    depth, d, n = w_mod.shape
    bsz = c.shape[0]
    rows = 8
    tn = min(1024, n)
    c_pad = jnp.zeros((rows, d), _F32).at[:bsz].set(c)
    out = pl.pallas_call(
        _mod_kernel,
        grid=(depth, n // tn),
        in_specs=[
            pl.BlockSpec((rows, d), lambda l, j: (0, 0)),
            pl.BlockSpec((1, d, tn), lambda l, j: (l, 0, j)),
            pl.BlockSpec((1, 1, tn), lambda l, j: (l, 0, j)),
        ],
        out_specs=pl.BlockSpec((1, rows, tn), lambda l, j: (l, 0, j)),
        out_shape=jax.ShapeDtypeStruct((depth, rows, n), _F32),
        compiler_params=_cparams(("arbitrary", "arbitrary")),
        name="modulation",
    )(c_pad, w_mod, b_mod.reshape(depth, 1, n))
    return out[:, :bsz]


def _norm_mod_kernel(h_ref, g_ref, scale_ref, shift_ref, o_ref):
    h = h_ref[...]
    inv = lax.rsqrt(jnp.mean(h * h, axis=-1, keepdims=True) + EPS)
    gain = g_ref[...] * (1.0 + scale_ref[0])
    o_ref[...] = (h * inv * gain + shift_ref[0]).astype(o_ref.dtype)


def _norm_mod(h, g, scale, shift, seq):
    m, d = h.shape
    tm = 512
    per_b = seq // tm
    return pl.pallas_call(
        _norm_mod_kernel,
        grid=(m // tm,),
        in_specs=[
            pl.BlockSpec((tm, d), lambda i: (i, 0)),
            pl.BlockSpec((1, d), lambda i: (0, 0)),
            pl.BlockSpec((1, 1, d), lambda i: (i // per_b, 0, 0)),
            pl.BlockSpec((1, 1, d), lambda i: (i // per_b, 0, 0)),
        ],
        out_specs=pl.BlockSpec((tm, d), lambda i: (i, 0)),
        out_shape=jax.ShapeDtypeStruct((m, d), _BF),
        compiler_params=_cparams(("arbitrary",)),
        name="norm_mod",
    )(h, g.reshape(1, d), scale, shift)


def _final_norm_kernel(h_ref, g_ref, o_ref):
    h = h_ref[...]
    inv = lax.rsqrt(jnp.mean(h * h, axis=-1, keepdims=True) + EPS)
    o_ref[...] = h * inv * g_ref[...]


def _final_norm(h, g):
    m, d = h.shape
    tm = 512
    return pl.pallas_call(
        _final_norm_kernel,
        grid=(m // tm,),
        in_specs=[pl.BlockSpec((tm, d), lambda i: (i, 0)), pl.BlockSpec((1, d), lambda i: (0, 0))],
        out_specs=pl.BlockSpec((tm, d), lambda i: (i, 0)),
        out_shape=jax.ShapeDtypeStruct((m, d), _F32),
        compiler_params=_cparams(("arbitrary",)),
        name="final_norm",
    )(h, g.reshape(1, d))


def _proj_kernel(u_ref, w_ref, o_ref, wbf_ref):
    @pl.when(pl.program_id(1) == 0)
    def _():
        wbf_ref[...] = w_ref[0].astype(_BF)

    o_ref[...] = _dot(u_ref[...], wbf_ref[...]).astype(o_ref.dtype)


def _in_proj(u, w_in, layer):
    m, d = u.shape
    p = w_in.shape[2]
    tm, tn = min(1024, m), min(1024, p)
    return pl.pallas_call(
        _proj_kernel,
        grid=(p // tn, m // tm),
        in_specs=[
            pl.BlockSpec((tm, d), lambda j, i: (i, 0)),
            pl.BlockSpec((1, d, tn), lambda j, i: (layer, 0, j)),
        ],
        out_specs=pl.BlockSpec((tm, tn), lambda j, i: (i, j)),
        out_shape=jax.ShapeDtypeStruct((m, p), _BF),
        scratch_shapes=[pltpu.VMEM((d, tn), _BF)],
        compiler_params=_cparams(("arbitrary", "arbitrary")),
        name="in_proj",
    )(u, w_in)


def _out_kernel(a_ref, w_ref, h_ref, gate_ref, o_ref, wbf_ref):
    @pl.when(pl.program_id(1) == 0)
    def _():
        wbf_ref[...] = w_ref[0].astype(_BF)

    y = _dot(a_ref[...], wbf_ref[...])
    o_ref[...] = h_ref[...] + gate_ref[0] * y


def _out_proj(a, w_out, layer, h, gate, seq):
    m, k = a.shape
    d = w_out.shape[2]
    tm, tn = 512, min(512, d)
    per_b = seq // tm
    return pl.pallas_call(
        _out_kernel,
        grid=(d // tn, m // tm),
        in_specs=[
            pl.BlockSpec((tm, k), lambda j, i: (i, 0)),
            pl.BlockSpec((1, k, tn), lambda j, i: (layer, 0, j)),
            pl.BlockSpec((tm, tn), lambda j, i: (i, j)),
            pl.BlockSpec((1, 1, tn), lambda j, i: (i // per_b, 0, j)),
        ],
        out_specs=pl.BlockSpec((tm, tn), lambda j, i: (i, j)),
        out_shape=jax.ShapeDtypeStruct((m, d), _F32),
        scratch_shapes=[pltpu.VMEM((k, tn), _BF)],
        compiler_params=_cparams(("arbitrary", "arbitrary")),
        name="out_proj",
    )(a, w_out, h, gate)


def _t5_bucket(dist):
    max_exact = N_BUCKETS // 2
    d = jnp.maximum(dist, 1).astype(_F32)
    large = max_exact + (jnp.log(d / max_exact) / math.log(MAX_DISTANCE / max_exact)
                         * (N_BUCKETS - max_exact)).astype(jnp.int32)
    large = jnp.minimum(large, N_BUCKETS - 1)
    return jnp.where(dist < max_exact, dist, large)


def _dilated_log_multiplicity(n):
    delta = jnp.arange(n)
    mult = jnp.zeros((n,), _F32)
    for window, dil in DILATED_PATTERNS:
        mult = mult + ((delta % dil == 0) & (delta <= window)).astype(_F32)
    return jnp.where(mult > 0, jnp.log(jnp.maximum(mult, 1.0)), NEG_BIG)


def _skewed_bias(bias_vec, n_off):
    t = SM_BLOCK
    h = bias_vec.shape[0]
    c = n_off * t
    width = c + t
    rr = jnp.take(bias_vec, jnp.clip(jnp.arange(width) - t, 0, c - 1), axis=1)
    return pl.pallas_call(
        _skew_kernel,
        grid=(h,),
        in_specs=[pl.BlockSpec((1, 1, width), lambda i: (i, 0, 0))],
        out_specs=pl.BlockSpec((1, t, width), lambda i: (i, 0, 0)),
        out_shape=jax.ShapeDtypeStruct((h, t, width), _F32),
        compiler_params=_cparams(("arbitrary",)),
        name="bias_skew",
    )(rr.reshape(h, 1, width))


def _skew_kernel(rr_ref, o_ref):
    _, t, width = o_ref.shape
    rows = 8

    def body(g, carry):
        r0 = pl.multiple_of(g * rows, rows)
        x = jnp.broadcast_to(rr_ref[0], (rows, width))
        o_ref[0, pl.ds(r0, rows), :] = pltpu.roll(x, r0, 1, stride=1, stride_axis=0)
        return carry

    lax.fori_loop(0, t // rows, body, 0)


def _flash_attend(qts, k_refs, col0s, bias_ref, vt_ref, bufs, scale2, lo, qi, finish):
    t = SM_BLOCK
    raw_a, raw_b, p_a, p_b, acc_ref = bufs
    chains = range(len(qts))
    n = qts[0].shape[1]

    def scores_into(ki, raw_ref):
        start = pl.multiple_of(ki * t, t)
        for c in chains:
            raw_ref[c] = _dot(k_refs[c][pl.ds(start, t), :], qts[c])

    def softmax_from(raw_ref, ki, causal, stats, p_ref):
        bias = _bias_tile(bias_ref, qi - ki)
        new_stats, alphas, ps = [], [], []
        for c in chains:
            m, l = stats[c]
            s = raw_ref[c] * scale2 + bias[:, col0s[c]:col0s[c] + n]
            if causal:
                key = lax.broadcasted_iota(jnp.int32, s.shape, 0)
                qry = lax.broadcasted_iota(jnp.int32, s.shape, 1) + col0s[c]
                s = jnp.where(key <= qry, s, NEG_BIG)
            m_new = jnp.maximum(m, jnp.max(s, axis=0, keepdims=True))
            alphas.append(jnp.exp2(m - m_new))
            p = jnp.exp2(s - m_new)
            new_stats.append((m_new, l * alphas[c] + jnp.sum(p, axis=0, keepdims=True)))
            if p_ref is None:
                ps.append(p.astype(_BF))
            else:
                p_ref[c] = p.astype(_BF)
        return tuple(new_stats), tuple(alphas), ps

    def add_weighted_values(ki, p_src, alphas):
        vt_blk = vt_ref[:, pl.ds(pl.multiple_of(ki * t, t), t)]
        for c in chains:
            acc_ref[c] = acc_ref[c] * alphas[c] + _dot(vt_blk, p_src[c])

    def half_step(ki, raw_cur, raw_next, p_cur, p_prev, stats, alphas):
        add_weighted_values(jnp.maximum(ki - 1, lo), p_prev, alphas)
        scores_into(ki + 1, raw_next)
        stats, alphas, _ = softmax_from(raw_cur, ki, False, stats, p_cur)
        return stats, alphas

    def last_steps(raw_ref, p_prev, stats, alphas):
        add_weighted_values(jnp.maximum(qi - 1, lo), p_prev, alphas)
        stats, alphas, ps = softmax_from(raw_ref, qi, True, stats, None)
        add_weighted_values(qi, ps, alphas)
        finish(tuple((stats[c][1], acc_ref[c]) for c in chains))

    for c in chains:
        acc_ref[c] = jnp.zeros(acc_ref.shape[1:], _F32)
        p_b[c] = jnp.zeros(p_b.shape[1:], _BF)
    scores_into(lo, raw_a)
    stats = tuple((jnp.full((1, n), NEG_BIG, _F32), jnp.zeros((1, n), _F32)) for _ in chains)
    alphas = tuple(jnp.ones((1, n), _F32) for _ in chains)
    n_full = qi - lo

    def pair(it, carry):
        stats, alphas = carry
        k0 = lo + 2 * it
        stats, alphas = half_step(k0, raw_a, raw_b, p_a, p_b, stats, alphas)
        return half_step(k0 + 1, raw_b, raw_a, p_b, p_a, stats, alphas)

    stats, alphas = lax.fori_loop(0, n_full // 2, pair, (stats, alphas))

    @pl.when(n_full % 2 == 1)
    def _():
        st, al = half_step(qi - 1, raw_a, raw_b, p_a, p_b, stats, alphas)
        last_steps(raw_b, p_a, st, al)

    @pl.when(n_full % 2 == 0)
    def _():
        last_steps(raw_a, p_b, stats, alphas)


def _bias_tile(bias_ref, off):
    t = SM_BLOCK
    return bias_ref[0, :, pl.ds(pl.multiple_of((off + 1) * t, t), t)]


def _fill_v_transposed(v_ref, vt_ref):
    t = SM_BLOCK
    for c in range(v_ref.shape[0] // t):
        vt_ref[:, c * t:(c + 1) * t] = v_ref[c * t:(c + 1) * t, :].astype(_F32).T.astype(_BF)


def _dilated_kernel(q_ref, k_ref, v_ref, z_ref, bias_ref, o_ref, vt_ref, *bufs, n_off):
    t = SM_BLOCK
    qi = pl.program_id(2)
    qt = q_ref[...].astype(_F32).T.astype(_BF)
    scale2 = HEAD_DIM ** -0.5 * LOG2E

    @pl.when(qi == 0)
    def _():
        _fill_v_transposed(v_ref, vt_ref)

    half = t // 2
    lo = jnp.maximum(qi - (n_off - 1), 0)

    def finish(outs):
        o = jnp.concatenate([acc / l for l, acc in outs], axis=1).T
        o_ref[...] = (o * _silu(z_ref[...].astype(_F32))).astype(o_ref.dtype)

    _flash_attend((qt[:, :half], qt[:, half:]), (k_ref, k_ref), (0, half), bias_ref, vt_ref, bufs, scale2, lo, qi,
                  finish)


def _diff_kernel(q1_ref, q2_ref, k1_ref, k2_ref, v_ref, z_ref, bias_ref, lam_ref, g_ref, o_ref, vt_ref, *bufs,
                 lam_init):
    t = SM_BLOCK
    h = pl.program_id(1)
    qi = pl.program_id(2)
    scale2 = DIFF_QK_DIM ** -0.5 * LOG2E
    dim = lax.broadcasted_iota(jnp.int32, (HEAD_DIM, t), 0)
    mine = (dim // DIFF_QK_DIM) == (h % 2)
    q1t = jnp.where(mine, q1_ref[...].astype(_F32).T, 0.0).astype(_BF)
    q2t = jnp.where(mine, q2_ref[...].astype(_F32).T, 0.0).astype(_BF)

    @pl.when(qi == 0)
    def _():
        _fill_v_transposed(v_ref, vt_ref)

    def finish(outs):
        (l1, a1), (l2, a2) = outs
        lv = lam_ref[...]
        lam = (jnp.exp(jnp.sum(lv[0:1] * lv[1:2], axis=1, keepdims=True))
               - jnp.exp(jnp.sum(lv[2:3] * lv[3:4], axis=1, keepdims=True)) + lam_init)
        o = a1 / l1 - lam * (a2 / l2)
        o = (o * lax.rsqrt(jnp.mean(o * o, axis=0, keepdims=True) + EPS)).T
        o = o * (g_ref[...] * (1.0 - lam_init))
        o_ref[...] = (o * _silu(z_ref[...].astype(_F32))).astype(o_ref.dtype)

    _flash_attend((q1t, q2t), (k1_ref, k2_ref), (0, 0), bias_ref, vt_ref, bufs, scale2, 0, qi, finish)


def _stick_kernel(q_ref, k_ref, v_ref, z_ref, o_ref):
    t = STICK_BLOCK
    qi = pl.program_id(2)
    q = q_ref[...]
    scale = HEAD_DIM ** -0.5
    row = lax.broadcasted_iota(jnp.int32, (t, t), 0)
    col = lax.broadcasted_iota(jnp.int32, (t, t), 1)
    later = jnp.where(row > col, 1.0, 0.0).astype(_BF)
    strict = col < row

    def step(ki, carry, diagonal):
        acc, tail0 = carry
        start = pl.multiple_of(ki * t, t)
        k_blk = k_ref[pl.ds(start, t), :]
        v_blk = v_ref[pl.ds(start, t), :]
        z = _dot_nt(q, k_blk) * scale
        softplus = jnp.log(1.0 + jnp.exp(-jnp.abs(z)))
        log_beta = jnp.minimum(z, 0.0) - softplus
        log_1m = log_beta - z
        if diagonal:
            log_1m = jnp.where(strict, log_1m, 0.0)
        hi, lo = _split_hi_lo(log_1m)
        tail = tail0 + _dot(hi, later) + _dot(lo, later)
        a = jnp.exp(log_beta + tail)
        if diagonal:
            a = jnp.where(strict, a, 0.0)
        acc = acc + _dot(a.astype(_BF), v_blk)
        tail0 = tail0 + jnp.sum(log_1m, axis=1, keepdims=True)
        return acc, tail0

    acc, tail0 = step(qi, (jnp.zeros((t, HEAD_DIM), _F32), jnp.zeros((t, 1), _F32)), True)

    def live(tail0):
        return (jnp.max(tail0) > STICK_SKIP_LOG).astype(jnp.int32)

    def cond(state):
        i, _, _, alive = state
        return jnp.logical_and(i < qi, alive > 0)

    def body(state):
        i, acc, tail0, _ = state
        acc, tail0 = step(qi - 1 - i, (acc, tail0), False)
        return i + 1, acc, tail0, live(tail0)

    _, acc, _, _ = lax.while_loop(cond, body, (jnp.int32(0), acc, tail0, live(tail0)))
    o_ref[...] = (acc * _silu(z_ref[...].astype(_F32))).astype(o_ref.dtype)


def _flash_buffers(chains, n):
    t = SM_BLOCK
    return [pltpu.VMEM((chains, t, n), _F32), pltpu.VMEM((chains, t, n), _F32),
            pltpu.VMEM((chains, t, n), _BF), pltpu.VMEM((chains, t, n), _BF),
            pltpu.VMEM((chains, HEAD_DIM, n), _F32)]


def _head_spec(t, col0, nq):
    return pl.BlockSpec((t, HEAD_DIM), lambda b, h, qi: (b * nq + qi, col0 + h))


def _seq_spec(seq, col0, per_block=1):
    return pl.BlockSpec((seq, HEAD_DIM), lambda b, h, qi: (b, col0 + h // per_block))


def _odd_mixer(proj, bsz, seq, d_inner):
    n_heads = d_inner // HEAD_DIM
    t = STICK_BLOCK
    nq = seq // t
    return pl.pallas_call(
        _stick_kernel,
        grid=(bsz, n_heads, nq),
        in_specs=[
            _head_spec(t, 0, nq),
            _seq_spec(seq, n_heads),
            _seq_spec(seq, 2 * n_heads),
            _head_spec(t, 3 * n_heads, nq),
        ],
        out_specs=_head_spec(t, 0, nq),
        out_shape=jax.ShapeDtypeStruct((bsz * seq, d_inner), _BF),
        compiler_params=_cparams(("arbitrary", "arbitrary", "arbitrary")),
        name="stick_breaking",
    )(proj, proj, proj, proj)


def _even_mixer(proj, bias_a, bias_b, lam_vec, subln_g, lam_init, bsz, seq, d_inner):
    d_a = d_inner // 2
    h_a = d_a // HEAD_DIM
    h_b = (d_inner - d_a) // HEAD_DIM
    t = SM_BLOCK
    nq = seq // t
    n_off_a = bias_a.shape[2] // t - 1
    n_off_b = bias_b.shape[2] // t - 1
    out_shape = jax.ShapeDtypeStruct((bsz * seq, d_a), _BF)
    params = _cparams(("arbitrary", "arbitrary", "arbitrary"))
    z0 = 3 * d_inner // HEAD_DIM

    o_a = pl.pallas_call(
        functools.partial(_dilated_kernel, n_off=n_off_a),
        grid=(bsz, h_a, nq),
        in_specs=[
            _head_spec(t, 0, nq),
            _seq_spec(seq, h_a),
            _seq_spec(seq, 2 * h_a),
            _head_spec(t, z0, nq),
            pl.BlockSpec((1, t, (n_off_a + 1) * t), lambda b, h, qi: (h, 0, 0)),
        ],
        out_specs=_head_spec(t, 0, nq),
        out_shape=out_shape,
        scratch_shapes=[pltpu.VMEM((HEAD_DIM, seq), _BF)] + _flash_buffers(2, t // 2),
        compiler_params=params,
        name="dilated_attention",
    )(proj, proj, proj, proj, bias_a)

    c0 = 3 * h_a
    half = h_b // 2
    o_b = pl.pallas_call(
        functools.partial(_diff_kernel, lam_init=lam_init),
        grid=(bsz, h_b, nq),
        in_specs=[
            pl.BlockSpec((t, HEAD_DIM), lambda b, h, qi: (b * nq + qi, c0 + h // 2)),
            pl.BlockSpec((t, HEAD_DIM), lambda b, h, qi: (b * nq + qi, c0 + half + h // 2)),
            _seq_spec(seq, c0 + 2 * half, 2),
            _seq_spec(seq, c0 + 3 * half, 2),
            _seq_spec(seq, c0 + 4 * half),
            _head_spec(t, z0 + h_a, nq),
            pl.BlockSpec((1, t, (n_off_b + 1) * t), lambda b, h, qi: (h, 0, 0)),
            pl.BlockSpec((4, DIFF_QK_DIM), lambda b, h, qi: (0, 0)),
            pl.BlockSpec((1, HEAD_DIM), lambda b, h, qi: (0, 0)),
        ],
        out_specs=_head_spec(t, 0, nq),
        out_shape=out_shape,
        scratch_shapes=[pltpu.VMEM((HEAD_DIM, seq), _BF)] + _flash_buffers(2, t),
        compiler_params=params,
        name="diff_attention",
    )(proj, proj, proj, proj, proj, proj, bias_b, lam_vec, subln_g.reshape(1, HEAD_DIM))
    return jnp.concatenate([o_a, o_b], axis=1)


def kernel(x, c, norm_g, w_mod, b_mod, w_in, w_out, rel_bias, diff_lambda, diff_subln_g, final_norm_g):
    bsz, seq, d = x.shape
    depth = w_in.shape[0]
    d_inner = w_out.shape[1]
    h_a = d_inner // 2 // HEAD_DIM
    n_blk = seq // SM_BLOCK

    mod = _modulation(c, w_mod, b_mod)

    table = rel_bias[_t5_bucket(jnp.arange(seq))].T
    n_off_a = min(n_blk, DILATED_PATTERNS[-1][0] // SM_BLOCK + 1)
    vec_a = table[:h_a, : n_off_a * SM_BLOCK] + _dilated_log_multiplicity(n_off_a * SM_BLOCK)[None]
    bias_a = _skewed_bias(vec_a * LOG2E, n_off_a)
    bias_b = _skewed_bias(table[h_a:] * LOG2E, n_blk)

    h = x.reshape(bsz * seq, d)
    for layer in range(depth):
        shift = mod[layer, :, 0:d].reshape(bsz, 1, d)
        scale = mod[layer, :, d:2 * d].reshape(bsz, 1, d)
        gate = mod[layer, :, 2 * d:].reshape(bsz, 1, d)
        u = _norm_mod(h, norm_g[layer], scale, shift, seq)
        proj = _in_proj(u, w_in, layer)
        if layer % 2 == 0:
            e = layer // 2
            lam_init = 0.8 - 0.6 * math.exp(-0.3 * layer)
            mixed = _even_mixer(proj, bias_a, bias_b, diff_lambda[e], diff_subln_g[e], lam_init,
                                bsz, seq, d_inner)
        else:
            mixed = _odd_mixer(proj, bsz, seq, d_inner)
        h = _out_proj(mixed, w_out, layer, h, gate, seq)
    return _final_norm(h, final_norm_g).reshape(bsz, seq, d)
```

```python
import functools
import math

import jax
import jax.numpy as jnp
from jax import lax
from jax.experimental import pallas as pl
from jax.experimental.pallas import tpu as pltpu

HEAD_DIM = 128
DIFF_QK_DIM = 64
EXPAND = 2
N_BUCKETS = 32
MAX_DISTANCE = 2048
DILATED_PATTERNS = ((128, 1), (512, 4), (2048, 16))
EPS = 1e-6
SM_BLOCK = 512
STICK_BLOCK = 256
STICK_HEADS = 4
STICK_SKIP_LOG = -104.0
NEG_BIG = -1e30
LOG2E = math.log2(math.e)
VMEM_LIMIT = 56 * 1024 * 1024

_BF = jnp.bfloat16
_F32 = jnp.float32


def _cparams(sem):
    return pltpu.CompilerParams(dimension_semantics=sem, vmem_limit_bytes=VMEM_LIMIT)


def _dot(a, b):
    return jnp.dot(a, b, preferred_element_type=_F32)


def _dot_nt(a, b):
    return lax.dot_general(a, b, (((1,), (1,)), ((), ())), preferred_element_type=_F32)


def _split_hi_lo(x):
    hi = x.astype(_BF)
    lo = (x - hi.astype(_F32)).astype(_BF)
    return hi, lo


def _silu(z):
    return z / (1.0 + jnp.exp(-z))


def _mod_kernel(c_ref, w_ref, b_ref, o_ref):
    a_hi, a_lo = _split_hi_lo(_silu(c_ref[...]))
    w_hi, w_lo = _split_hi_lo(w_ref[0])
    acc = _dot(a_hi, w_hi) + _dot(a_lo, w_hi) + _dot(a_hi, w_lo)
    o_ref[0] = acc + b_ref[0]


def _modulation(c, w_mod, b_mod):
    depth, d, n = w_mod.shape
    bsz = c.shape[0]
    rows = 8
    tn = 1024 if n % 1024 == 0 else n
    c_pad = jnp.zeros((rows, d), _F32).at[:bsz].set(c)
    out = pl.pallas_call(
        _mod_kernel,
        grid=(depth, n // tn),
        in_specs=[
            pl.BlockSpec((rows, d), lambda l, j: (0, 0)),
            pl.BlockSpec((1, d, tn), lambda l, j: (l, 0, j)),
            pl.BlockSpec((1, 1, tn), lambda l, j: (l, 0, j)),
        ],
        out_specs=pl.BlockSpec((1, rows, tn), lambda l, j: (l, 0, j)),
        out_shape=jax.ShapeDtypeStruct((depth, rows, n), _F32),
        compiler_params=_cparams(("arbitrary", "arbitrary")),
        name="modulation",
    )(c_pad, w_mod, b_mod.reshape(depth, 1, n))
    return out[:, :bsz]


def _norm_mod_kernel(h_ref, g_ref, scale_ref, shift_ref, o_ref):
    h = h_ref[...]
    inv = lax.rsqrt(jnp.mean(h * h, axis=-1, keepdims=True) + EPS)
    gain = g_ref[...] * (1.0 + scale_ref[0])
    o_ref[...] = (h * inv * gain + shift_ref[0]).astype(o_ref.dtype)


def _norm_mod(h, g, scale, shift, seq):
    m, d = h.shape
    tm = 512
    per_b = seq // tm
    return pl.pallas_call(
        _norm_mod_kernel,
        grid=(m // tm,),
        in_specs=[
            pl.BlockSpec((tm, d), lambda i: (i, 0)),
            pl.BlockSpec((1, d), lambda i: (0, 0)),
            pl.BlockSpec((1, 1, d), lambda i: (i // per_b, 0, 0)),
            pl.BlockSpec((1, 1, d), lambda i: (i // per_b, 0, 0)),
        ],
        out_specs=pl.BlockSpec((tm, d), lambda i: (i, 0)),
        out_shape=jax.ShapeDtypeStruct((m, d), _BF),
        compiler_params=_cparams(("arbitrary",)),
        name="norm_mod",
    )(h, g.reshape(1, d), scale, shift)


def _final_norm_kernel(h_ref, g_ref, o_ref):
    h = h_ref[...]
    inv = lax.rsqrt(jnp.mean(h * h, axis=-1, keepdims=True) + EPS)
    o_ref[...] = h * inv * g_ref[...]


def _final_norm(h, g):
    m, d = h.shape
    tm = 512
    return pl.pallas_call(
        _final_norm_kernel,
        grid=(m // tm,),
        in_specs=[pl.BlockSpec((tm, d), lambda i: (i, 0)), pl.BlockSpec((1, d), lambda i: (0, 0))],
        out_specs=pl.BlockSpec((tm, d), lambda i: (i, 0)),
        out_shape=jax.ShapeDtypeStruct((m, d), _F32),
        compiler_params=_cparams(("arbitrary",)),
        name="final_norm",
    )(h, g.reshape(1, d))


def _proj_kernel(u_ref, w_ref, o_ref, wbf_ref):
    @pl.when(pl.program_id(1) == 0)
    def _():
        wbf_ref[...] = w_ref[0].astype(_BF)

    o_ref[...] = _dot(u_ref[...], wbf_ref[...]).astype(o_ref.dtype)


def _in_proj(u, w_in, layer):
    m, d = u.shape
    p = w_in.shape[2]
    tm, tn = min(1024, m), min(1024, p)
    return pl.pallas_call(
        _proj_kernel,
        grid=(p // tn, m // tm),
        in_specs=[
            pl.BlockSpec((tm, d), lambda j, i: (i, 0)),
            pl.BlockSpec((1, d, tn), lambda j, i: (layer, 0, j)),
        ],
        out_specs=pl.BlockSpec((tm, tn), lambda j, i: (i, j)),
        out_shape=jax.ShapeDtypeStruct((m, p), _BF),
        scratch_shapes=[pltpu.VMEM((d, tn), _BF)],
        compiler_params=_cparams(("arbitrary", "arbitrary")),
        name="in_proj",
    )(u, w_in)


def _out_kernel(a_ref, w_ref, h_ref, gate_ref, o_ref, wbf_ref):
    @pl.when(pl.program_id(1) == 0)
    def _():
        wbf_ref[...] = w_ref[0].astype(_BF)

    y = _dot(a_ref[...], wbf_ref[...])
    o_ref[...] = h_ref[...] + gate_ref[0] * y


def _out_proj(a, w_out, layer, h, gate, seq):
    m, k = a.shape
    d = w_out.shape[2]
    tm, tn = 512, min(512, d)
    per_b = seq // tm
    return pl.pallas_call(
        _out_kernel,
        grid=(d // tn, m // tm),
        in_specs=[
            pl.BlockSpec((tm, k), lambda j, i: (i, 0)),
            pl.BlockSpec((1, k, tn), lambda j, i: (layer, 0, j)),
            pl.BlockSpec((tm, tn), lambda j, i: (i, j)),
            pl.BlockSpec((1, 1, tn), lambda j, i: (i // per_b, 0, j)),
        ],
        out_specs=pl.BlockSpec((tm, tn), lambda j, i: (i, j)),
        out_shape=jax.ShapeDtypeStruct((m, d), _F32),
        scratch_shapes=[pltpu.VMEM((k, tn), _BF)],
        compiler_params=_cparams(("arbitrary", "arbitrary")),
        name="out_proj",
    )(a, w_out, h, gate)


def _t5_bucket(dist):
    max_exact = N_BUCKETS // 2
    d = jnp.maximum(dist, 1).astype(_F32)
    large = max_exact + (jnp.log(d / max_exact) / math.log(MAX_DISTANCE / max_exact)
                         * (N_BUCKETS - max_exact)).astype(jnp.int32)
    large = jnp.minimum(large, N_BUCKETS - 1)
    return jnp.where(dist < max_exact, dist, large)


def _dilated_log_multiplicity(n):
    delta = jnp.arange(n)
    mult = jnp.zeros((n,), _F32)
    for window, dil in DILATED_PATTERNS:
        mult = mult + ((delta % dil == 0) & (delta <= window)).astype(_F32)
    return jnp.where(mult > 0, jnp.log(jnp.maximum(mult, 1.0)), NEG_BIG)


def _skewed_bias(bias_vec, n_off):
    t = SM_BLOCK
    h = bias_vec.shape[0]
    c = n_off * t
    width = c + t
    rr = jnp.take(bias_vec, jnp.clip(jnp.arange(width) - t, 0, c - 1), axis=1)
    return pl.pallas_call(
        _skew_kernel,
        grid=(h,),
        in_specs=[pl.BlockSpec((1, 1, width), lambda i: (i, 0, 0))],
        out_specs=pl.BlockSpec((1, t, width), lambda i: (i, 0, 0)),
        out_shape=jax.ShapeDtypeStruct((h, t, width), _F32),
        compiler_params=_cparams(("arbitrary",)),
        name="bias_skew",
    )(rr.reshape(h, 1, width))


def _skew_kernel(rr_ref, o_ref):
    _, t, width = o_ref.shape
    rows = 8

    def body(g, carry):
        r0 = pl.multiple_of(g * rows, rows)
        x = jnp.broadcast_to(rr_ref[0], (rows, width))
        o_ref[0, pl.ds(r0, rows), :] = pltpu.roll(x, r0, 1, stride=1, stride_axis=0)
        return carry

    lax.fori_loop(0, t // rows, body, 0)


def _flash_attend(qts, k_refs, col0s, bias_ref, vt_ref, bufs, scale2, lo, qi, finish):
    t = SM_BLOCK
    raw_a, raw_b, p_a, p_b, acc_ref = bufs
    chains = range(len(qts))
    n = qts[0].shape[1]

    def scores_into(ki, raw_ref):
        start = pl.multiple_of(ki * t, t)
        for c in chains:
            raw_ref[c] = _dot(k_refs[c][pl.ds(start, t), :], qts[c])

    def softmax_from(raw_ref, ki, causal, stats, p_ref):
        bias = _bias_tile(bias_ref, qi - ki)
        new_stats, alphas, ps = [], [], []
        for c in chains:
            m, l = stats[c]
            s = raw_ref[c] * scale2 + bias[:, col0s[c]:col0s[c] + n]
            if causal:
                key = lax.broadcasted_iota(jnp.int32, s.shape, 0)
                qry = lax.broadcasted_iota(jnp.int32, s.shape, 1) + col0s[c]
                s = jnp.where(key <= qry, s, NEG_BIG)
            m_new = jnp.maximum(m, jnp.max(s, axis=0, keepdims=True))
            alphas.append(jnp.exp2(m - m_new))
            p = jnp.exp2(s - m_new)
            new_stats.append((m_new, l * alphas[c] + jnp.sum(p, axis=0, keepdims=True)))
            if p_ref is None:
                ps.append(p.astype(_BF))
            else:
                p_ref[c] = p.astype(_BF)
        return tuple(new_stats), tuple(alphas), ps

    def add_weighted_values(ki, p_src, alphas):
        vt_blk = vt_ref[:, pl.ds(pl.multiple_of(ki * t, t), t)]
        for c in chains:
            acc_ref[c] = acc_ref[c] * alphas[c] + _dot(vt_blk, p_src[c])

    def half_step(ki, raw_cur, raw_next, p_cur, p_prev, stats, alphas):
        add_weighted_values(jnp.maximum(ki - 1, lo), p_prev, alphas)
        scores_into(ki + 1, raw_next)
        stats, alphas, _ = softmax_from(raw_cur, ki, False, stats, p_cur)
        return stats, alphas

    def last_steps(raw_ref, p_prev, stats, alphas):
        add_weighted_values(jnp.maximum(qi - 1, lo), p_prev, alphas)
        stats, alphas, ps = softmax_from(raw_ref, qi, True, stats, None)
        add_weighted_values(qi, ps, alphas)
        finish(tuple((stats[c][1], acc_ref[c]) for c in chains))

    for c in chains:
        acc_ref[c] = jnp.zeros(acc_ref.shape[1:], _F32)
        p_b[c] = jnp.zeros(p_b.shape[1:], _BF)
    scores_into(lo, raw_a)
    stats = tuple((jnp.full((1, n), NEG_BIG, _F32), jnp.zeros((1, n), _F32)) for _ in chains)
    alphas = tuple(jnp.ones((1, n), _F32) for _ in chains)
    n_full = qi - lo

    def pair(it, carry):
        stats, alphas = carry
        k0 = lo + 2 * it
        stats, alphas = half_step(k0, raw_a, raw_b, p_a, p_b, stats, alphas)
        return half_step(k0 + 1, raw_b, raw_a, p_b, p_a, stats, alphas)

    stats, alphas = lax.fori_loop(0, n_full // 2, pair, (stats, alphas))

    @pl.when(n_full % 2 == 1)
    def _():
        st, al = half_step(qi - 1, raw_a, raw_b, p_a, p_b, stats, alphas)
        last_steps(raw_b, p_a, st, al)

    @pl.when(n_full % 2 == 0)
    def _():
        last_steps(raw_a, p_b, stats, alphas)


def _bias_tile(bias_ref, off):
    t = SM_BLOCK
    return bias_ref[0, :, pl.ds(pl.multiple_of((off + 1) * t, t), t)]


def _fill_v_transposed(v_ref, vt_ref):
    t = SM_BLOCK
    for c in range(v_ref.shape[0] // t):
        vt_ref[:, c * t:(c + 1) * t] = v_ref[c * t:(c + 1) * t, :].astype(_F32).T.astype(_BF)


def _dilated_kernel(q_ref, k_ref, v_ref, z_ref, bias_ref, o_ref, vt_ref, *bufs, n_off):
    t = SM_BLOCK
    qi = pl.program_id(2)
    qt = q_ref[...].astype(_F32).T.astype(_BF)
    scale2 = HEAD_DIM ** -0.5 * LOG2E

    @pl.when(qi == 0)
    def _():
        _fill_v_transposed(v_ref, vt_ref)

    half = t // 2
    lo = jnp.maximum(qi - (n_off - 1), 0)

    def finish(outs):
        o = jnp.concatenate([acc / l for l, acc in outs], axis=1).T
        o_ref[...] = (o * _silu(z_ref[...].astype(_F32))).astype(o_ref.dtype)

    _flash_attend((qt[:, :half], qt[:, half:]), (k_ref, k_ref), (0, half), bias_ref, vt_ref, bufs, scale2, lo, qi,
                  finish)


def _diff_kernel(q1_ref, q2_ref, k1_ref, k2_ref, v_ref, z_ref, bias_ref, lam_ref, g_ref, o_ref, vt_ref, *bufs,
                 lam_init):
    t = SM_BLOCK
    h = pl.program_id(1)
    qi = pl.program_id(2)
    scale2 = DIFF_QK_DIM ** -0.5 * LOG2E
    dim = lax.broadcasted_iota(jnp.int32, (HEAD_DIM, t), 0)
    mine = (dim // DIFF_QK_DIM) == (h % 2)
    q1t = jnp.where(mine, q1_ref[...].astype(_F32).T, 0.0).astype(_BF)
    q2t = jnp.where(mine, q2_ref[...].astype(_F32).T, 0.0).astype(_BF)

    @pl.when(qi == 0)
    def _():
        _fill_v_transposed(v_ref, vt_ref)

    def finish(outs):
        (l1, a1), (l2, a2) = outs
        lv = lam_ref[...]
        lam = (jnp.exp(jnp.sum(lv[0:1] * lv[1:2], axis=1, keepdims=True))
               - jnp.exp(jnp.sum(lv[2:3] * lv[3:4], axis=1, keepdims=True)) + lam_init)
        o = a1 / l1 - lam * (a2 / l2)
        o = (o * lax.rsqrt(jnp.mean(o * o, axis=0, keepdims=True) + EPS)).T
        o = o * (g_ref[...] * (1.0 - lam_init))
        o_ref[...] = (o * _silu(z_ref[...].astype(_F32))).astype(o_ref.dtype)

    _flash_attend((q1t, q2t), (k1_ref, k2_ref), (0, 0), bias_ref, vt_ref, bufs, scale2, 0, qi, finish)


def _stick_kernel(q_ref, k_ref, v_ref, z_ref, o_ref, acc_ref):
    t = STICK_BLOCK
    d = HEAD_DIM
    heads = range(q_ref.shape[1] // d)
    qi = pl.program_id(2)
    scale = d ** -0.5
    row = lax.broadcasted_iota(jnp.int32, (t, t), 0)
    col = lax.broadcasted_iota(jnp.int32, (t, t), 1)
    later = jnp.where(row > col, 1.0, 0.0).astype(_BF)
    strict = col < row
    qs = [q_ref[:, c * d:(c + 1) * d] for c in heads]

    def step(ki, tails, diagonal):
        start = pl.multiple_of(ki * t, t)
        zs = [_dot_nt(qs[c], k_ref[pl.ds(start, t), c * d:(c + 1) * d]) for c in heads]
        log_betas, parts = [], []
        for c in heads:
            z = zs[c] * scale
            softplus = jnp.log(1.0 + jnp.exp(-jnp.abs(z)))
            log_beta = jnp.minimum(z, 0.0) - softplus
            log_1m = log_beta - z
            if diagonal:
                log_1m = jnp.where(strict, log_1m, 0.0)
            log_betas.append(log_beta)
            parts.append(_split_hi_lo(log_1m) + (jnp.sum(log_1m, axis=1, keepdims=True),))
        sums = [_dot(hi, later) + _dot(lo, later) for hi, lo, _ in parts]
        weights = []
        for c in heads:
            a = jnp.exp(log_betas[c] + (tails[c] + sums[c]))
            if diagonal:
                a = jnp.where(strict, a, 0.0)
            weights.append(a.astype(_BF))
        for c in heads:
            acc_ref[c] = acc_ref[c] + _dot(weights[c], v_ref[pl.ds(start, t), c * d:(c + 1) * d])
        return tuple(tails[c] + parts[c][2] for c in heads)

    for c in heads:
        acc_ref[c] = jnp.zeros((t, d), _F32)
    tails = step(qi, tuple(jnp.zeros((t, 1), _F32) for _ in heads), True)

    def live(tails):
        worst = functools.reduce(jnp.maximum, tails)
        return (jnp.max(worst) > STICK_SKIP_LOG).astype(jnp.int32)

    def cond(state):
        i, _, alive = state
        return jnp.logical_and(i < qi, alive > 0)

    def body(state):
        i, tails, _ = state
        tails = step(qi - 1 - i, tails, False)
        return i + 1, tails, live(tails)

    lax.while_loop(cond, body, (jnp.int32(0), tails, live(tails)))
    for c in heads:
        gate = _silu(z_ref[:, c * d:(c + 1) * d].astype(_F32))
        o_ref[:, c * d:(c + 1) * d] = (acc_ref[c] * gate).astype(o_ref.dtype)


def _flash_buffers(chains, n):
    t = SM_BLOCK
    return [pltpu.VMEM((chains, t, n), _F32), pltpu.VMEM((chains, t, n), _F32),
            pltpu.VMEM((chains, t, n), _BF), pltpu.VMEM((chains, t, n), _BF),
            pltpu.VMEM((chains, HEAD_DIM, n), _F32)]


def _head_spec(t, col0, nq):
    return pl.BlockSpec((t, HEAD_DIM), lambda b, h, qi: (b * nq + qi, col0 + h))


def _seq_spec(seq, col0, per_block=1):
    return pl.BlockSpec((seq, HEAD_DIM), lambda b, h, qi: (b, col0 + h // per_block))


def _odd_mixer(proj, bsz, seq, d_inner):
    t = STICK_BLOCK
    w = STICK_HEADS * HEAD_DIM
    groups = d_inner // w
    nq = seq // t
    tile = lambda col0: pl.BlockSpec((t, w), lambda b, g, qi: (b * nq + qi, col0 + g))
    whole = lambda col0: pl.BlockSpec((seq, w), lambda b, g, qi: (b, col0 + g))
    return pl.pallas_call(
        _stick_kernel,
        grid=(bsz, groups, nq),
        in_specs=[tile(0), whole(groups), whole(2 * groups), tile(3 * groups)],
        out_specs=tile(0),
        out_shape=jax.ShapeDtypeStruct((bsz * seq, d_inner), _BF),
        scratch_shapes=[pltpu.VMEM((STICK_HEADS, t, HEAD_DIM), _F32)],
        compiler_params=_cparams(("arbitrary", "arbitrary", "arbitrary")),
        name="stick_breaking",
    )(proj, proj, proj, proj)


def _even_mixer(proj, bias_a, bias_b, lam_vec, subln_g, lam_init, bsz, seq, d_inner):
    d_a = d_inner // 2
    h_a = d_a // HEAD_DIM
    h_b = (d_inner - d_a) // HEAD_DIM
    t = SM_BLOCK
    nq = seq // t
    n_off_a = bias_a.shape[2] // t - 1
    n_off_b = bias_b.shape[2] // t - 1
    out_shape = jax.ShapeDtypeStruct((bsz * seq, d_a), _BF)
    params = _cparams(("arbitrary", "arbitrary", "arbitrary"))
    z0 = 3 * d_inner // HEAD_DIM

    o_a = pl.pallas_call(
        functools.partial(_dilated_kernel, n_off=n_off_a),
        grid=(bsz, h_a, nq),
        in_specs=[
            _head_spec(t, 0, nq),
            _seq_spec(seq, h_a),
            _seq_spec(seq, 2 * h_a),
            _head_spec(t, z0, nq),
            pl.BlockSpec((1, t, (n_off_a + 1) * t), lambda b, h, qi: (h, 0, 0)),
        ],
        out_specs=_head_spec(t, 0, nq),
        out_shape=out_shape,
        scratch_shapes=[pltpu.VMEM((HEAD_DIM, seq), _BF)] + _flash_buffers(2, t // 2),
        compiler_params=params,
        name="dilated_attention",
    )(proj, proj, proj, proj, bias_a)

    c0 = 3 * h_a
    half = h_b // 2
    o_b = pl.pallas_call(
        functools.partial(_diff_kernel, lam_init=lam_init),
        grid=(bsz, h_b, nq),
        in_specs=[
            pl.BlockSpec((t, HEAD_DIM), lambda b, h, qi: (b * nq + qi, c0 + h // 2)),
            pl.BlockSpec((t, HEAD_DIM), lambda b, h, qi: (b * nq + qi, c0 + half + h // 2)),
            _seq_spec(seq, c0 + 2 * half, 2),
            _seq_spec(seq, c0 + 3 * half, 2),
            _seq_spec(seq, c0 + 4 * half),
            _head_spec(t, z0 + h_a, nq),
            pl.BlockSpec((1, t, (n_off_b + 1) * t), lambda b, h, qi: (h, 0, 0)),
            pl.BlockSpec((4, DIFF_QK_DIM), lambda b, h, qi: (0, 0)),
            pl.BlockSpec((1, HEAD_DIM), lambda b, h, qi: (0, 0)),
        ],
        out_specs=_head_spec(t, 0, nq),
        out_shape=out_shape,
        scratch_shapes=[pltpu.VMEM((HEAD_DIM, seq), _BF)] + _flash_buffers(2, t),
        compiler_params=params,
        name="diff_attention",
    )(proj, proj, proj, proj, proj, proj, bias_b, lam_vec, subln_g.reshape(1, HEAD_DIM))
    return jnp.concatenate([o_a, o_b], axis=1)


def kernel(x, c, norm_g, w_mod, b_mod, w_in, w_out, rel_bias, diff_lambda, diff_subln_g, final_norm_g):
    bsz, seq, d = x.shape
    depth = w_in.shape[0]
    d_inner = w_out.shape[1]
    h_a = d_inner // 2 // HEAD_DIM
    n_blk = seq // SM_BLOCK

    mod = _modulation(c, w_mod, b_mod)

    table = rel_bias[_t5_bucket(jnp.arange(seq))].T
    n_off_a = min(n_blk, DILATED_PATTERNS[-1][0] // SM_BLOCK + 1)
    vec_a = table[:h_a, : n_off_a * SM_BLOCK] + _dilated_log_multiplicity(n_off_a * SM_BLOCK)[None]
    bias_a = _skewed_bias(vec_a * LOG2E, n_off_a)
    bias_b = _skewed_bias(table[h_a:] * LOG2E, n_blk)

    h = x.reshape(bsz * seq, d)
    for layer in range(depth):
        shift = mod[layer, :, 0:d].reshape(bsz, 1, d)
        scale = mod[layer, :, d:2 * d].reshape(bsz, 1, d)
        gate = mod[layer, :, 2 * d:].reshape(bsz, 1, d)
        u = _norm_mod(h, norm_g[layer], scale, shift, seq)
        proj = _in_proj(u, w_in, layer)
        if layer % 2 == 0:
            e = layer // 2
            lam_init = 0.8 - 0.6 * math.exp(-0.3 * layer)
            mixed = _even_mixer(proj, bias_a, bias_b, diff_lambda[e], diff_subln_g[e], lam_init,
                                bsz, seq, d_inner)
        else:
            mixed = _odd_mixer(proj, bsz, seq, d_inner)
        h = _out_proj(mixed, w_out, layer, h, gate, seq)
    return _final_norm(h, final_norm_g).reshape(bsz, seq, d)
```

```python
import functools
import math

import jax
import jax.numpy as jnp
from jax import lax
from jax.experimental import pallas as pl
from jax.experimental.pallas import tpu as pltpu

HEAD_DIM = 128
DIFF_QK_DIM = 64
EXPAND = 2
N_BUCKETS = 32
MAX_DISTANCE = 2048
DILATED_PATTERNS = ((128, 1), (512, 4), (2048, 16))
EPS = 1e-6
SM_BLOCK = 512
STICK_BLOCK = 256
STICK_HEADS = 4
STICK_SKIP_LOG = -104.0
NEG_BIG = -1e30
LOG2E = math.log2(math.e)
VMEM_LIMIT = 56 * 1024 * 1024

_BF = jnp.bfloat16
_F32 = jnp.float32


def _cparams(sem):
    return pltpu.CompilerParams(dimension_semantics=sem, vmem_limit_bytes=VMEM_LIMIT)


def _dot(a, b):
    return jnp.dot(a, b, preferred_element_type=_F32)


def _dot_nt(a, b):
    return lax.dot_general(a, b, (((1,), (1,)), ((), ())), preferred_element_type=_F32)


def _split_hi_lo(x):
    hi = x.astype(_BF)
    lo = (x - hi.astype(_F32)).astype(_BF)
    return hi, lo


def _silu(z):
    return z / (1.0 + jnp.exp(-z))


def _mod_kernel(c_ref, w_ref, b_ref, o_ref):
    a_hi, a_lo = _split_hi_lo(_silu(c_ref[...]))
    w_hi, w_lo = _split_hi_lo(w_ref[0])
    acc = _dot(a_hi, w_hi) + _dot(a_lo, w_hi) + _dot(a_hi, w_lo)
    o_ref[0] = acc + b_ref[0]


def _modulation(c, w_mod, b_mod):
    depth, d, n = w_mod.shape
    bsz = c.shape[0]
    rows = 8
    tn = 1024 if n % 1024 == 0 else n
    c_pad = jnp.zeros((rows, d), _F32).at[:bsz].set(c)
    out = pl.pallas_call(
        _mod_kernel,
        grid=(depth, n // tn),
        in_specs=[
            pl.BlockSpec((rows, d), lambda l, j: (0, 0)),
            pl.BlockSpec((1, d, tn), lambda l, j: (l, 0, j)),
            pl.BlockSpec((1, 1, tn), lambda l, j: (l, 0, j)),
        ],
        out_specs=pl.BlockSpec((1, rows, tn), lambda l, j: (l, 0, j)),
        out_shape=jax.ShapeDtypeStruct((depth, rows, n), _F32),
        compiler_params=_cparams(("arbitrary", "arbitrary")),
        name="modulation",
    )(c_pad, w_mod, b_mod.reshape(depth, 1, n))
    return out[:, :bsz]


def _norm_mod_kernel(h_ref, g_ref, scale_ref, shift_ref, o_ref):
    h = h_ref[...]
    inv = lax.rsqrt(jnp.mean(h * h, axis=-1, keepdims=True) + EPS)
    gain = g_ref[...] * (1.0 + scale_ref[0])
    o_ref[...] = (h * inv * gain + shift_ref[0]).astype(o_ref.dtype)


def _norm_mod(h, g, scale, shift, seq):
    m, d = h.shape
    tm = 512
    per_b = seq // tm
    return pl.pallas_call(
        _norm_mod_kernel,
        grid=(m // tm,),
        in_specs=[
            pl.BlockSpec((tm, d), lambda i: (i, 0)),
            pl.BlockSpec((1, d), lambda i: (0, 0)),
            pl.BlockSpec((1, 1, d), lambda i: (i // per_b, 0, 0)),
            pl.BlockSpec((1, 1, d), lambda i: (i // per_b, 0, 0)),
        ],
        out_specs=pl.BlockSpec((tm, d), lambda i: (i, 0)),
        out_shape=jax.ShapeDtypeStruct((m, d), _BF),
        compiler_params=_cparams(("arbitrary",)),
        name="norm_mod",
    )(h, g.reshape(1, d), scale, shift)


def _final_norm_kernel(h_ref, g_ref, o_ref):
    h = h_ref[...]
    inv = lax.rsqrt(jnp.mean(h * h, axis=-1, keepdims=True) + EPS)
    o_ref[...] = h * inv * g_ref[...]


def _final_norm(h, g):
    m, d = h.shape
    tm = 512
    return pl.pallas_call(
        _final_norm_kernel,
        grid=(m // tm,),
        in_specs=[pl.BlockSpec((tm, d), lambda i: (i, 0)), pl.BlockSpec((1, d), lambda i: (0, 0))],
        out_specs=pl.BlockSpec((tm, d), lambda i: (i, 0)),
        out_shape=jax.ShapeDtypeStruct((m, d), _F32),
        compiler_params=_cparams(("arbitrary",)),
        name="final_norm",
    )(h, g.reshape(1, d))


def _proj_kernel(u_ref, w_ref, o_ref, wbf_ref):
    @pl.when(pl.program_id(1) == 0)
    def _():
        wbf_ref[...] = w_ref[0].astype(_BF)

    o_ref[...] = _dot(u_ref[...], wbf_ref[...]).astype(o_ref.dtype)


def _in_proj(u, w_in, layer):
    m, d = u.shape
    p = w_in.shape[2]
    tm, tn = min(1024, m), min(1024, p)
    return pl.pallas_call(
        _proj_kernel,
        grid=(p // tn, m // tm),
        in_specs=[
            pl.BlockSpec((tm, d), lambda j, i: (i, 0)),
            pl.BlockSpec((1, d, tn), lambda j, i: (layer, 0, j)),
        ],
        out_specs=pl.BlockSpec((tm, tn), lambda j, i: (i, j)),
        out_shape=jax.ShapeDtypeStruct((m, p), _BF),
        scratch_shapes=[pltpu.VMEM((d, tn), _BF)],
        compiler_params=_cparams(("arbitrary", "arbitrary")),
        name="in_proj",
    )(u, w_in)


def _out_kernel(a_ref, w_ref, h_ref, gate_ref, o_ref, wbf_ref):
    @pl.when(pl.program_id(1) == 0)
    def _():
        wbf_ref[...] = w_ref[0].astype(_BF)

    y = _dot(a_ref[...], wbf_ref[...])
    o_ref[...] = h_ref[...] + gate_ref[0] * y


def _out_proj(a, w_out, layer, h, gate, seq):
    m, k = a.shape
    d = w_out.shape[2]
    tm, tn = 512, min(512, d)
    per_b = seq // tm
    return pl.pallas_call(
        _out_kernel,
        grid=(d // tn, m // tm),
        in_specs=[
            pl.BlockSpec((tm, k), lambda j, i: (i, 0)),
            pl.BlockSpec((1, k, tn), lambda j, i: (layer, 0, j)),
            pl.BlockSpec((tm, tn), lambda j, i: (i, j)),
            pl.BlockSpec((1, 1, tn), lambda j, i: (i // per_b, 0, j)),
        ],
        out_specs=pl.BlockSpec((tm, tn), lambda j, i: (i, j)),
        out_shape=jax.ShapeDtypeStruct((m, d), _F32),
        scratch_shapes=[pltpu.VMEM((k, tn), _BF)],
        compiler_params=_cparams(("arbitrary", "arbitrary")),
        name="out_proj",
    )(a, w_out, h, gate)


def _t5_bucket(dist):
    max_exact = N_BUCKETS // 2
    d = jnp.maximum(dist, 1).astype(_F32)
    large = max_exact + (jnp.log(d / max_exact) / math.log(MAX_DISTANCE / max_exact)
                         * (N_BUCKETS - max_exact)).astype(jnp.int32)
    large = jnp.minimum(large, N_BUCKETS - 1)
    return jnp.where(dist < max_exact, dist, large)


def _dilated_log_multiplicity(n):
    delta = jnp.arange(n)
    mult = jnp.zeros((n,), _F32)
    for window, dil in DILATED_PATTERNS:
        mult = mult + ((delta % dil == 0) & (delta <= window)).astype(_F32)
    return jnp.where(mult > 0, jnp.log(jnp.maximum(mult, 1.0)), NEG_BIG)


def _skewed_bias(bias_vec, n_off):
    t = SM_BLOCK
    h = bias_vec.shape[0]
    c = n_off * t
    width = c + t
    rr = jnp.take(bias_vec, jnp.clip(jnp.arange(width) - t, 0, c - 1), axis=1)
    return pl.pallas_call(
        _skew_kernel,
        grid=(h,),
        in_specs=[pl.BlockSpec((1, 1, width), lambda i: (i, 0, 0))],
        out_specs=pl.BlockSpec((1, t, width), lambda i: (i, 0, 0)),
        out_shape=jax.ShapeDtypeStruct((h, t, width), _F32),
        compiler_params=_cparams(("arbitrary",)),
        name="bias_skew",
    )(rr.reshape(h, 1, width))


def _skew_kernel(rr_ref, o_ref):
    _, t, width = o_ref.shape
    rows = 8

    def body(g, carry):
        r0 = pl.multiple_of(g * rows, rows)
        x = jnp.broadcast_to(rr_ref[0], (rows, width))
        o_ref[0, pl.ds(r0, rows), :] = pltpu.roll(x, r0, 1, stride=1, stride_axis=0)
        return carry

    lax.fori_loop(0, t // rows, body, 0)


def _flash_attend(qts, k_refs, col0s, bias_ref, vt_ref, bufs, scale2, lo, qi, finish):
    t = SM_BLOCK
    raw_a, raw_b, p_a, p_b, acc_ref = bufs
    chains = range(len(qts))
    n = qts[0].shape[1]

    def scores_into(ki, s_ref):
        start = pl.multiple_of(ki * t, t)
        bias = _bias_tile(bias_ref, qi - ki)
        tops = []
        for c in chains:
            s = _dot(k_refs[c][pl.ds(start, t), :], qts[c]) * scale2 + bias[:, col0s[c]:col0s[c] + n]
            s_ref[c] = s
            tops.append(jnp.max(s, axis=0, keepdims=True))
        return tuple(tops)

    def softmax_from(s_ref, tops, causal, stats, p_ref):
        new_stats, alphas, ps = [], [], []
        for c in chains:
            m, l = stats[c]
            s = s_ref[c]
            top = tops[c]
            if causal:
                key = lax.broadcasted_iota(jnp.int32, s.shape, 0)
                qry = lax.broadcasted_iota(jnp.int32, s.shape, 1) + col0s[c]
                s = jnp.where(key <= qry, s, NEG_BIG)
                top = jnp.max(s, axis=0, keepdims=True)
            m_new = jnp.maximum(m, top)
            alphas.append(jnp.exp2(m - m_new))
            p = jnp.exp2(s - m_new)
            new_stats.append((m_new, l * alphas[c] + jnp.sum(p, axis=0, keepdims=True)))
            if p_ref is None:
                ps.append(p.astype(_BF))
            else:
                p_ref[c] = p.astype(_BF)
        return tuple(new_stats), tuple(alphas), ps

    def add_weighted_values(ki, p_src, alphas):
        vt_blk = vt_ref[:, pl.ds(pl.multiple_of(ki * t, t), t)]
        for c in chains:
            acc_ref[c] = acc_ref[c] * alphas[c] + _dot(vt_blk, p_src[c])

    def half_step(ki, s_cur, s_next, p_cur, p_prev, tops, stats, alphas):
        add_weighted_values(jnp.maximum(ki - 1, lo), p_prev, alphas)
        tops_next = scores_into(ki + 1, s_next)
        stats, alphas, _ = softmax_from(s_cur, tops, False, stats, p_cur)
        return tops_next, stats, alphas

    def last_steps(s_ref, p_prev, tops, stats, alphas):
        add_weighted_values(jnp.maximum(qi - 1, lo), p_prev, alphas)
        stats, alphas, ps = softmax_from(s_ref, tops, True, stats, None)
        add_weighted_values(qi, ps, alphas)
        finish(tuple((stats[c][1], acc_ref[c]) for c in chains))

    for c in chains:
        acc_ref[c] = jnp.zeros(acc_ref.shape[1:], _F32)
        p_b[c] = jnp.zeros(p_b.shape[1:], _BF)
    tops = scores_into(lo, raw_a)
    stats = tuple((jnp.full((1, n), NEG_BIG, _F32), jnp.zeros((1, n), _F32)) for _ in chains)
    alphas = tuple(jnp.ones((1, n), _F32) for _ in chains)
    n_full = qi - lo

    def pair(it, carry):
        k0 = lo + 2 * it
        carry = half_step(k0, raw_a, raw_b, p_a, p_b, *carry)
        return half_step(k0 + 1, raw_b, raw_a, p_b, p_a, *carry)

    tops, stats, alphas = lax.fori_loop(0, n_full // 2, pair, (tops, stats, alphas))

    @pl.when(n_full % 2 == 1)
    def _():
        last_steps(raw_b, p_a, *half_step(qi - 1, raw_a, raw_b, p_a, p_b, tops, stats, alphas))

    @pl.when(n_full % 2 == 0)
    def _():
        last_steps(raw_a, p_b, tops, stats, alphas)


def _bias_tile(bias_ref, off):
    t = SM_BLOCK
    return bias_ref[0, :, pl.ds(pl.multiple_of((off + 1) * t, t), t)]


def _fill_v_transposed(v_ref, vt_ref):
    t = SM_BLOCK
    for c in range(v_ref.shape[0] // t):
        vt_ref[:, c * t:(c + 1) * t] = v_ref[c * t:(c + 1) * t, :].astype(_F32).T.astype(_BF)


def _dilated_kernel(q_ref, k_ref, v_ref, z_ref, bias_ref, o_ref, vt_ref, *bufs, n_off):
    t = SM_BLOCK
    qi = pl.program_id(2)
    qt = q_ref[...].astype(_F32).T.astype(_BF)
    scale2 = HEAD_DIM ** -0.5 * LOG2E

    @pl.when(qi == 0)
    def _():
        _fill_v_transposed(v_ref, vt_ref)

    half = t // 2
    lo = jnp.maximum(qi - (n_off - 1), 0)

    def finish(outs):
        o = jnp.concatenate([acc / l for l, acc in outs], axis=1).T
        o_ref[...] = (o * _silu(z_ref[...].astype(_F32))).astype(o_ref.dtype)

    _flash_attend((qt[:, :half], qt[:, half:]), (k_ref, k_ref), (0, half), bias_ref, vt_ref, bufs, scale2, lo, qi,
                  finish)


def _diff_kernel(q1_ref, q2_ref, k1_ref, k2_ref, v_ref, z_ref, bias_ref, lam_ref, g_ref, o_ref, vt_ref, *bufs,
                 lam_init):
    t = SM_BLOCK
    h = pl.program_id(1)
    qi = pl.program_id(2)
    scale2 = DIFF_QK_DIM ** -0.5 * LOG2E
    dim = lax.broadcasted_iota(jnp.int32, (HEAD_DIM, t), 0)
    mine = (dim // DIFF_QK_DIM) == (h % 2)
    q1t = jnp.where(mine, q1_ref[...].astype(_F32).T, 0.0).astype(_BF)
    q2t = jnp.where(mine, q2_ref[...].astype(_F32).T, 0.0).astype(_BF)

    @pl.when(qi == 0)
    def _():
        _fill_v_transposed(v_ref, vt_ref)

    def finish(outs):
        (l1, a1), (l2, a2) = outs
        lv = lam_ref[...]
        lam = (jnp.exp(jnp.sum(lv[0:1] * lv[1:2], axis=1, keepdims=True))
               - jnp.exp(jnp.sum(lv[2:3] * lv[3:4], axis=1, keepdims=True)) + lam_init)
        o = a1 / l1 - lam * (a2 / l2)
        o = (o * lax.rsqrt(jnp.mean(o * o, axis=0, keepdims=True) + EPS)).T
        o = o * (g_ref[...] * (1.0 - lam_init))
        o_ref[...] = (o * _silu(z_ref[...].astype(_F32))).astype(o_ref.dtype)

    _flash_attend((q1t, q2t), (k1_ref, k2_ref), (0, 0), bias_ref, vt_ref, bufs, scale2, 0, qi, finish)


def _stick_kernel(q_ref, k_ref, v_ref, z_ref, o_ref, acc_ref):
    t = STICK_BLOCK
    d = HEAD_DIM
    heads = range(q_ref.shape[1] // d)
    qi = pl.program_id(2)
    scale = d ** -0.5
    row = lax.broadcasted_iota(jnp.int32, (t, t), 0)
    col = lax.broadcasted_iota(jnp.int32, (t, t), 1)
    later = jnp.where(row > col, 1.0, 0.0).astype(_BF)
    strict = col < row
    qs = [q_ref[:, c * d:(c + 1) * d] for c in heads]

    def step(ki, tails, diagonal):
        start = pl.multiple_of(ki * t, t)
        zs = [_dot_nt(qs[c], k_ref[pl.ds(start, t), c * d:(c + 1) * d]) for c in heads]
        log_betas, parts = [], []
        for c in heads:
            z = zs[c] * scale
            softplus = jnp.log(1.0 + jnp.exp(-jnp.abs(z)))
            log_beta = jnp.minimum(z, 0.0) - softplus
            log_1m = log_beta - z
            if diagonal:
                log_1m = jnp.where(strict, log_1m, 0.0)
            log_betas.append(log_beta)
            parts.append(_split_hi_lo(log_1m) + (jnp.sum(log_1m, axis=1, keepdims=True),))
        sums = [_dot(hi, later) + _dot(lo, later) for hi, lo, _ in parts]
        weights = []
        for c in heads:
            a = jnp.exp(log_betas[c] + (tails[c] + sums[c]))
            if diagonal:
                a = jnp.where(strict, a, 0.0)
            weights.append(a.astype(_BF))
        for c in heads:
            acc_ref[c] = acc_ref[c] + _dot(weights[c], v_ref[pl.ds(start, t), c * d:(c + 1) * d])
        return tuple(tails[c] + parts[c][2] for c in heads)

    for c in heads:
        acc_ref[c] = jnp.zeros((t, d), _F32)
    tails = step(qi, tuple(jnp.zeros((t, 1), _F32) for _ in heads), True)

    def live(tails):
        worst = functools.reduce(jnp.maximum, tails)
        return (jnp.max(worst) > STICK_SKIP_LOG).astype(jnp.int32)

    def cond(state):
        i, _, alive = state
        return jnp.logical_and(i < qi, alive > 0)

    def body(state):
        i, tails, _ = state
        tails = step(qi - 1 - i, tails, False)
        return i + 1, tails, live(tails)

    lax.while_loop(cond, body, (jnp.int32(0), tails, live(tails)))
    for c in heads:
        gate = _silu(z_ref[:, c * d:(c + 1) * d].astype(_F32))
        o_ref[:, c * d:(c + 1) * d] = (acc_ref[c] * gate).astype(o_ref.dtype)


def _flash_buffers(chains, n):
    t = SM_BLOCK
    return [pltpu.VMEM((chains, t, n), _F32), pltpu.VMEM((chains, t, n), _F32),
            pltpu.VMEM((chains, t, n), _BF), pltpu.VMEM((chains, t, n), _BF),
            pltpu.VMEM((chains, HEAD_DIM, n), _F32)]


def _head_spec(t, col0, nq):
    return pl.BlockSpec((t, HEAD_DIM), lambda b, h, qi: (b * nq + qi, col0 + h))


def _seq_spec(seq, col0, per_block=1):
    return pl.BlockSpec((seq, HEAD_DIM), lambda b, h, qi: (b, col0 + h // per_block))


def _odd_mixer(proj, bsz, seq, d_inner):
    t = STICK_BLOCK
    w = STICK_HEADS * HEAD_DIM
    groups = d_inner // w
    nq = seq // t
    tile = lambda col0: pl.BlockSpec((t, w), lambda b, g, qi: (b * nq + qi, col0 + g))
    whole = lambda col0: pl.BlockSpec((seq, w), lambda b, g, qi: (b, col0 + g))
    return pl.pallas_call(
        _stick_kernel,
        grid=(bsz, groups, nq),
        in_specs=[tile(0), whole(groups), whole(2 * groups), tile(3 * groups)],
        out_specs=tile(0),
        out_shape=jax.ShapeDtypeStruct((bsz * seq, d_inner), _BF),
        scratch_shapes=[pltpu.VMEM((STICK_HEADS, t, HEAD_DIM), _F32)],
        compiler_params=_cparams(("arbitrary", "arbitrary", "arbitrary")),
        name="stick_breaking",
    )(proj, proj, proj, proj)


def _even_mixer(proj, bias_a, bias_b, lam_vec, subln_g, lam_init, bsz, seq, d_inner):
    d_a = d_inner // 2
    h_a = d_a // HEAD_DIM
    h_b = (d_inner - d_a) // HEAD_DIM
    t = SM_BLOCK
    nq = seq // t
    n_off_a = bias_a.shape[2] // t - 1
    n_off_b = bias_b.shape[2] // t - 1
    out_shape = jax.ShapeDtypeStruct((bsz * seq, d_a), _BF)
    params = _cparams(("arbitrary", "arbitrary", "arbitrary"))
    z0 = 3 * d_inner // HEAD_DIM

    o_a = pl.pallas_call(
        functools.partial(_dilated_kernel, n_off=n_off_a),
        grid=(bsz, h_a, nq),
        in_specs=[
            _head_spec(t, 0, nq),
            _seq_spec(seq, h_a),
            _seq_spec(seq, 2 * h_a),
            _head_spec(t, z0, nq),
            pl.BlockSpec((1, t, (n_off_a + 1) * t), lambda b, h, qi: (h, 0, 0)),
        ],
        out_specs=_head_spec(t, 0, nq),
        out_shape=out_shape,
        scratch_shapes=[pltpu.VMEM((HEAD_DIM, seq), _BF)] + _flash_buffers(2, t // 2),
        compiler_params=params,
        name="dilated_attention",
    )(proj, proj, proj, proj, bias_a)

    c0 = 3 * h_a
    half = h_b // 2
    o_b = pl.pallas_call(
        functools.partial(_diff_kernel, lam_init=lam_init),
        grid=(bsz, h_b, nq),
        in_specs=[
            pl.BlockSpec((t, HEAD_DIM), lambda b, h, qi: (b * nq + qi, c0 + h // 2)),
            pl.BlockSpec((t, HEAD_DIM), lambda b, h, qi: (b * nq + qi, c0 + half + h // 2)),
            _seq_spec(seq, c0 + 2 * half, 2),
            _seq_spec(seq, c0 + 3 * half, 2),
            _seq_spec(seq, c0 + 4 * half),
            _head_spec(t, z0 + h_a, nq),
            pl.BlockSpec((1, t, (n_off_b + 1) * t), lambda b, h, qi: (h, 0, 0)),
            pl.BlockSpec((4, DIFF_QK_DIM), lambda b, h, qi: (0, 0)),
            pl.BlockSpec((1, HEAD_DIM), lambda b, h, qi: (0, 0)),
        ],
        out_specs=_head_spec(t, 0, nq),
        out_shape=out_shape,
        scratch_shapes=[pltpu.VMEM((HEAD_DIM, seq), _BF)] + _flash_buffers(2, t),
        compiler_params=params,
        name="diff_attention",
    )(proj, proj, proj, proj, proj, proj, bias_b, lam_vec, subln_g.reshape(1, HEAD_DIM))
    return jnp.concatenate([o_a, o_b], axis=1)


def kernel(x, c, norm_g, w_mod, b_mod, w_in, w_out, rel_bias, diff_lambda, diff_subln_g, final_norm_g):
    bsz, seq, d = x.shape
    depth = w_in.shape[0]
    d_inner = w_out.shape[1]
    h_a = d_inner // 2 // HEAD_DIM
    n_blk = seq // SM_BLOCK

    mod = _modulation(c, w_mod, b_mod)

    table = rel_bias[_t5_bucket(jnp.arange(seq))].T
    n_off_a = min(n_blk, DILATED_PATTERNS[-1][0] // SM_BLOCK + 1)
    vec_a = table[:h_a, : n_off_a * SM_BLOCK] + _dilated_log_multiplicity(n_off_a * SM_BLOCK)[None]
    bias_a = _skewed_bias(vec_a * LOG2E, n_off_a)
    bias_b = _skewed_bias(table[h_a:] * LOG2E, n_blk)

    h = x.reshape(bsz * seq, d)
    for layer in range(depth):
        shift = mod[layer, :, 0:d].reshape(bsz, 1, d)
        scale = mod[layer, :, d:2 * d].reshape(bsz, 1, d)
        gate = mod[layer, :, 2 * d:].reshape(bsz, 1, d)
        u = _norm_mod(h, norm_g[layer], scale, shift, seq)
        proj = _in_proj(u, w_in, layer)
        if layer % 2 == 0:
            e = layer // 2
            lam_init = 0.8 - 0.6 * math.exp(-0.3 * layer)
            mixed = _even_mixer(proj, bias_a, bias_b, diff_lambda[e], diff_subln_g[e], lam_init,
                                bsz, seq, d_inner)
        else:
            mixed = _odd_mixer(proj, bsz, seq, d_inner)
        h = _out_proj(mixed, w_out, layer, h, gate, seq)
    return _final_norm(h, final_norm_g).reshape(bsz, seq, d)
```

```python
import functools
import math

import jax
import jax.numpy as jnp
from jax import lax
from jax.experimental import pallas as pl
from jax.experimental.pallas import tpu as pltpu

HEAD_DIM = 128
DIFF_QK_DIM = 64
EXPAND = 2
N_BUCKETS = 32
MAX_DISTANCE = 2048
DILATED_PATTERNS = ((128, 1), (512, 4), (2048, 16))
EPS = 1e-6
SM_BLOCK = 512
STICK_BLOCK = 256
STICK_HEADS = 4
STICK_SKIP_LOG2 = -150.0
NEG_BIG = -1e30
LOG2E = math.log2(math.e)
V_ROWS = HEAD_DIM + 16
VMEM_LIMIT = 56 * 1024 * 1024

_BF = jnp.bfloat16
_F32 = jnp.float32


def _cparams(sem):
    return pltpu.CompilerParams(dimension_semantics=sem, vmem_limit_bytes=VMEM_LIMIT)


def _dot(a, b):
    return jnp.dot(a, b, preferred_element_type=_F32)


def _dot_nt(a, b):
    return lax.dot_general(a, b, (((1,), (1,)), ((), ())), preferred_element_type=_F32)


def _split_hi_lo(x):
    hi = x.astype(_BF)
    lo = (x - hi.astype(_F32)).astype(_BF)
    return hi, lo


def _silu(z):
    return z / (1.0 + jnp.exp(-z))


def _mod_kernel(c_ref, w_ref, b_ref, o_ref):
    a_hi, a_lo = _split_hi_lo(_silu(c_ref[...]))
    w_hi, w_lo = _split_hi_lo(w_ref[0])
    acc = _dot(a_hi, w_hi) + _dot(a_lo, w_hi) + _dot(a_hi, w_lo)
    o_ref[0] = acc + b_ref[0]


def _modulation(c, w_mod, b_mod):
    depth, d, n = w_mod.shape
    bsz = c.shape[0]
    rows = 8
    tn = 1024 if n % 1024 == 0 else n
    c_pad = jnp.zeros((rows, d), _F32).at[:bsz].set(c)
    out = pl.pallas_call(
        _mod_kernel,
        grid=(depth, n // tn),
        in_specs=[
            pl.BlockSpec((rows, d), lambda l, j: (0, 0)),
            pl.BlockSpec((1, d, tn), lambda l, j: (l, 0, j)),
            pl.BlockSpec((1, 1, tn), lambda l, j: (l, 0, j)),
        ],
        out_specs=pl.BlockSpec((1, rows, tn), lambda l, j: (l, 0, j)),
        out_shape=jax.ShapeDtypeStruct((depth, rows, n), _F32),
        compiler_params=_cparams(("arbitrary", "arbitrary")),
        name="modulation",
    )(c_pad, w_mod, b_mod.reshape(depth, 1, n))
    return out[:, :bsz]


def _norm_mod_kernel(h_ref, g_ref, scale_ref, shift_ref, o_ref):
    h = h_ref[...]
    inv = lax.rsqrt(jnp.mean(h * h, axis=-1, keepdims=True) + EPS)
    gain = g_ref[...] * (1.0 + scale_ref[0])
    o_ref[...] = (h * inv * gain + shift_ref[0]).astype(o_ref.dtype)


def _norm_mod(h, g, scale, shift, seq):
    m, d = h.shape
    tm = 512
    per_b = seq // tm
    return pl.pallas_call(
        _norm_mod_kernel,
        grid=(m // tm,),
        in_specs=[
            pl.BlockSpec((tm, d), lambda i: (i, 0)),
            pl.BlockSpec((1, d), lambda i: (0, 0)),
            pl.BlockSpec((1, 1, d), lambda i: (i // per_b, 0, 0)),
            pl.BlockSpec((1, 1, d), lambda i: (i // per_b, 0, 0)),
        ],
        out_specs=pl.BlockSpec((tm, d), lambda i: (i, 0)),
        out_shape=jax.ShapeDtypeStruct((m, d), _BF),
        compiler_params=_cparams(("arbitrary",)),
        name="norm_mod",
    )(h, g.reshape(1, d), scale, shift)


def _final_norm_kernel(h_ref, g_ref, o_ref):
    h = h_ref[...]
    inv = lax.rsqrt(jnp.mean(h * h, axis=-1, keepdims=True) + EPS)
    o_ref[...] = h * inv * g_ref[...]


def _final_norm(h, g):
    m, d = h.shape
    tm = 512
    return pl.pallas_call(
        _final_norm_kernel,
        grid=(m // tm,),
        in_specs=[pl.BlockSpec((tm, d), lambda i: (i, 0)), pl.BlockSpec((1, d), lambda i: (0, 0))],
        out_specs=pl.BlockSpec((tm, d), lambda i: (i, 0)),
        out_shape=jax.ShapeDtypeStruct((m, d), _F32),
        compiler_params=_cparams(("arbitrary",)),
        name="final_norm",
    )(h, g.reshape(1, d))


def _proj_kernel(u_ref, w_ref, o_ref, wbf_ref):
    @pl.when(pl.program_id(1) == 0)
    def _():
        wbf_ref[...] = w_ref[0].astype(_BF)

    o_ref[...] = _dot(u_ref[...], wbf_ref[...]).astype(o_ref.dtype)


def _in_proj(u, w_in, layer):
    m, d = u.shape
    p = w_in.shape[2]
    tm, tn = min(1024, m), min(1024, p)
    return pl.pallas_call(
        _proj_kernel,
        grid=(p // tn, m // tm),
        in_specs=[
            pl.BlockSpec((tm, d), lambda j, i: (i, 0)),
            pl.BlockSpec((1, d, tn), lambda j, i: (layer, 0, j)),
        ],
        out_specs=pl.BlockSpec((tm, tn), lambda j, i: (i, j)),
        out_shape=jax.ShapeDtypeStruct((m, p), _BF),
        scratch_shapes=[pltpu.VMEM((d, tn), _BF)],
        compiler_params=_cparams(("arbitrary", "arbitrary")),
        name="in_proj",
    )(u, w_in)


def _out_kernel(*refs):
    *a_refs, w_ref, h_ref, gate_ref, o_ref, wbf_ref = refs

    @pl.when(pl.program_id(1) == 0)
    def _():
        wbf_ref[...] = w_ref[0].astype(_BF)

    y, k0 = None, 0
    for a_ref in a_refs:
        k1 = k0 + a_ref.shape[1]
        part = _dot(a_ref[...], wbf_ref[k0:k1, :])
        y = part if y is None else y + part
        k0 = k1
    o_ref[...] = h_ref[...] + gate_ref[0] * y


def _out_proj(parts, w_out, layer, h, gate, seq):
    m = parts[0].shape[0]
    k, d = w_out.shape[1], w_out.shape[2]
    tm, tn = 256, min(1024, d)
    per_b = seq // tm
    return pl.pallas_call(
        _out_kernel,
        grid=(d // tn, m // tm),
        in_specs=[pl.BlockSpec((tm, p.shape[1]), lambda j, i: (i, 0)) for p in parts] + [
            pl.BlockSpec((1, k, tn), lambda j, i: (layer, 0, j)),
            pl.BlockSpec((tm, tn), lambda j, i: (i, j)),
            pl.BlockSpec((1, 1, tn), lambda j, i: (i // per_b, 0, j)),
        ],
        out_specs=pl.BlockSpec((tm, tn), lambda j, i: (i, j)),
        out_shape=jax.ShapeDtypeStruct((m, d), _F32),
        scratch_shapes=[pltpu.VMEM((k, tn), _BF)],
        compiler_params=_cparams(("arbitrary", "arbitrary")),
        name="out_proj",
    )(*parts, w_out, h, gate)


def _t5_bucket(dist):
    max_exact = N_BUCKETS // 2
    d = jnp.maximum(dist, 1).astype(_F32)
    large = max_exact + (jnp.log(d / max_exact) / math.log(MAX_DISTANCE / max_exact)
                         * (N_BUCKETS - max_exact)).astype(jnp.int32)
    large = jnp.minimum(large, N_BUCKETS - 1)
    return jnp.where(dist < max_exact, dist, large)


def _dilated_log_multiplicity(n):
    delta = jnp.arange(n)
    mult = jnp.zeros((n,), _F32)
    for window, dil in DILATED_PATTERNS:
        mult = mult + ((delta % dil == 0) & (delta <= window)).astype(_F32)
    return jnp.where(mult > 0, jnp.log(jnp.maximum(mult, 1.0)), NEG_BIG)


def _skewed_bias(bias_vec, n_off):
    t = SM_BLOCK
    h = bias_vec.shape[0]
    c = n_off * t
    width = c + t
    rr = jnp.take(bias_vec, jnp.clip(jnp.arange(width) - t, 0, c - 1), axis=1)
    return pl.pallas_call(
        _skew_kernel,
        grid=(h,),
        in_specs=[pl.BlockSpec((1, 1, width), lambda i: (i, 0, 0))],
        out_specs=pl.BlockSpec((1, t, width), lambda i: (i, 0, 0)),
        out_shape=jax.ShapeDtypeStruct((h, t, width), _F32),
        compiler_params=_cparams(("arbitrary",)),
        name="bias_skew",
    )(rr.reshape(h, 1, width))


def _skew_kernel(rr_ref, o_ref):
    _, t, width = o_ref.shape
    rows = 8

    def body(g, carry):
        r0 = pl.multiple_of(g * rows, rows)
        x = jnp.broadcast_to(rr_ref[0], (rows, width))
        o_ref[0, pl.ds(r0, rows), :] = pltpu.roll(x, r0, 1, stride=1, stride_axis=0)
        return carry

    lax.fori_loop(0, t // rows, body, 0)


def _flash_attend(qts, k_refs, col0s, bias_ref, vt_ref, bufs, scale2, lo, qi, finish):
    t = SM_BLOCK
    raw_a, raw_b, p_a, p_b, acc_ref = bufs
    chains = range(len(qts))
    n = qts[0].shape[1]

    def scores_into(ki, s_ref):
        start = pl.multiple_of(ki * t, t)
        bias = _bias_tile(bias_ref, qi - ki)
        tops = []
        for c in chains:
            s = _dot(k_refs[c][pl.ds(start, t), :], qts[c]) * scale2 + bias[:, col0s[c]:col0s[c] + n]
            s_ref[c] = s
            tops.append(jnp.max(s, axis=0, keepdims=True))
        return tuple(tops)

    def softmax_from(s_ref, tops, causal, stats, p_ref):
        new_stats, alphas, ps = [], [], []
        for c in chains:
            m = stats[c]
            s = s_ref[c]
            top = tops[c]
            if causal:
                key = lax.broadcasted_iota(jnp.int32, s.shape, 0)
                qry = lax.broadcasted_iota(jnp.int32, s.shape, 1) + col0s[c]
                s = jnp.where(key <= qry, s, NEG_BIG)
                top = jnp.max(s, axis=0, keepdims=True)
            m_new = jnp.maximum(m, top)
            alphas.append(jnp.exp2(m - m_new))
            p = jnp.exp2(s - m_new)
            new_stats.append(m_new)
            if p_ref is None:
                ps.append(p.astype(_BF))
            else:
                p_ref[c] = p.astype(_BF)
        return tuple(new_stats), tuple(alphas), ps

    def add_weighted_values(ki, p_src, alphas):
        vt_blk = vt_ref[:, pl.ds(pl.multiple_of(ki * t, t), t)]
        for c in chains:
            acc_ref[c] = acc_ref[c] * alphas[c] + _dot(vt_blk, p_src[c])

    def half_step(ki, s_cur, s_next, p_cur, p_prev, tops, stats, alphas):
        add_weighted_values(jnp.maximum(ki - 1, lo), p_prev, alphas)
        tops_next = scores_into(ki + 1, s_next)
        stats, alphas, _ = softmax_from(s_cur, tops, False, stats, p_cur)
        return tops_next, stats, alphas

    def last_steps(s_ref, p_prev, tops, stats, alphas):
        add_weighted_values(jnp.maximum(qi - 1, lo), p_prev, alphas)
        stats, alphas, ps = softmax_from(s_ref, tops, True, stats, None)
        add_weighted_values(qi, ps, alphas)
        finish(tuple((acc_ref[c, HEAD_DIM:HEAD_DIM + 1, :], acc_ref[c, :HEAD_DIM, :]) for c in chains))

    for c in chains:
        acc_ref[c] = jnp.zeros(acc_ref.shape[1:], _F32)
        p_b[c] = jnp.zeros(p_b.shape[1:], _BF)
    tops = scores_into(lo, raw_a)
    stats = tuple(jnp.full((1, n), NEG_BIG, _F32) for _ in chains)
    alphas = tuple(jnp.ones((1, n), _F32) for _ in chains)
    n_full = qi - lo

    def pair(it, carry):
        k0 = lo + 2 * it
        carry = half_step(k0, raw_a, raw_b, p_a, p_b, *carry)
        return half_step(k0 + 1, raw_b, raw_a, p_b, p_a, *carry)

    tops, stats, alphas = lax.fori_loop(0, n_full // 2, pair, (tops, stats, alphas))

    @pl.when(n_full % 2 == 1)
    def _():
        last_steps(raw_b, p_a, *half_step(qi - 1, raw_a, raw_b, p_a, p_b, tops, stats, alphas))

    @pl.when(n_full % 2 == 0)
    def _():
        last_steps(raw_a, p_b, tops, stats, alphas)


def _bias_tile(bias_ref, off):
    t = SM_BLOCK
    return bias_ref[0, :, pl.ds(pl.multiple_of((off + 1) * t, t), t)]


def _fill_v_transposed(v_ref, vt_ref):
    t = SM_BLOCK
    pad = lax.broadcasted_iota(jnp.int32, (V_ROWS - HEAD_DIM, t), 0)
    for c in range(v_ref.shape[0] // t):
        vt_ref[:HEAD_DIM, c * t:(c + 1) * t] = v_ref[c * t:(c + 1) * t, :].astype(_F32).T.astype(_BF)
        vt_ref[HEAD_DIM:, c * t:(c + 1) * t] = jnp.where(pad == 0, 1.0, 0.0).astype(_BF)


def _dilated_kernel(q_ref, k_ref, v_ref, z_ref, bias_ref, o_ref, vt_ref, *bufs, n_off):
    t = SM_BLOCK
    qi = pl.program_id(2)
    qt = q_ref[...].astype(_F32).T.astype(_BF)
    scale2 = HEAD_DIM ** -0.5 * LOG2E

    @pl.when(qi == 0)
    def _():
        _fill_v_transposed(v_ref, vt_ref)

    half = t // 2
    lo = jnp.maximum(qi - (n_off - 1), 0)

    def finish(outs):
        o = jnp.concatenate([acc / l for l, acc in outs], axis=1).T
        o_ref[...] = (o * _silu(z_ref[...].astype(_F32))).astype(o_ref.dtype)

    _flash_attend((qt[:, :half], qt[:, half:]), (k_ref, k_ref), (0, half), bias_ref, vt_ref, bufs, scale2, lo, qi,
                  finish)


def _diff_kernel(q1_ref, q2_ref, k1_ref, k2_ref, v_ref, z_ref, bias_ref, lam_ref, g_ref, o_ref, vt_ref, *bufs,
                 lam_init):
    t = SM_BLOCK
    h = pl.program_id(1)
    qi = pl.program_id(2)
    scale2 = DIFF_QK_DIM ** -0.5 * LOG2E
    dim = lax.broadcasted_iota(jnp.int32, (HEAD_DIM, t), 0)
    mine = (dim // DIFF_QK_DIM) == (h % 2)
    q1t = jnp.where(mine, q1_ref[...].astype(_F32).T, 0.0).astype(_BF)
    q2t = jnp.where(mine, q2_ref[...].astype(_F32).T, 0.0).astype(_BF)

    @pl.when(qi == 0)
    def _():
        _fill_v_transposed(v_ref, vt_ref)

    def finish(outs):
        (l1, a1), (l2, a2) = outs
        lv = lam_ref[...]
        lam = (jnp.exp(jnp.sum(lv[0:1] * lv[1:2], axis=1, keepdims=True))
               - jnp.exp(jnp.sum(lv[2:3] * lv[3:4], axis=1, keepdims=True)) + lam_init)
        o = a1 / l1 - lam * (a2 / l2)
        o = (o * lax.rsqrt(jnp.mean(o * o, axis=0, keepdims=True) + EPS)).T
        o = o * (g_ref[...] * (1.0 - lam_init))
        o_ref[...] = (o * _silu(z_ref[...].astype(_F32))).astype(o_ref.dtype)

    _flash_attend((q1t, q2t), (k1_ref, k2_ref), (0, 0), bias_ref, vt_ref, bufs, scale2, 0, qi, finish)


def _stick_kernel(q_ref, k_ref, v_ref, z_ref, o_ref, acc_ref):
    t = STICK_BLOCK
    d = HEAD_DIM
    heads = range(q_ref.shape[1] // d)
    qi = pl.program_id(2)
    scale2 = d ** -0.5 * LOG2E
    row = lax.broadcasted_iota(jnp.int32, (t, t), 0)
    col = lax.broadcasted_iota(jnp.int32, (t, t), 1)
    later = jnp.where(row > col, 1.0, 0.0).astype(_BF)
    strict = col < row
    qs = [q_ref[:, c * d:(c + 1) * d] for c in heads]

    def step(ki, tails, diagonal):
        start = pl.multiple_of(ki * t, t)
        zs = [_dot_nt(qs[c], k_ref[pl.ds(start, t), c * d:(c + 1) * d]) for c in heads]
        log_betas, parts = [], []
        for c in heads:
            z = zs[c] * scale2
            softplus = jnp.log(1.0 + jnp.exp2(-jnp.abs(z))) * LOG2E
            log_beta = jnp.minimum(z, 0.0) - softplus
            log_1m = log_beta - z
            if diagonal:
                log_1m = jnp.where(strict, log_1m, 0.0)
            log_betas.append(log_beta)
            parts.append(_split_hi_lo(log_1m) + (log_1m[:, 0:1],))
        sums = [_dot(hi, later) + _dot(lo, later) for hi, lo, _ in parts]
        weights = []
        for c in heads:
            a = jnp.exp2(log_betas[c] + (tails[c] + sums[c]))
            if diagonal:
                a = jnp.where(strict, a, 0.0)
            weights.append(a.astype(_BF))
        for c in heads:
            acc_ref[c] = acc_ref[c] + _dot(weights[c], v_ref[pl.ds(start, t), c * d:(c + 1) * d])
        return tuple(tails[c] + (sums[c][:, 0:1] + parts[c][2]) for c in heads)

    for c in heads:
        acc_ref[c] = jnp.zeros((t, d), _F32)
    tails = step(qi, tuple(jnp.zeros((t, 1), _F32) for _ in heads), True)

    def live(tails):
        worst = functools.reduce(jnp.maximum, tails)
        return (jnp.max(worst) > STICK_SKIP_LOG2).astype(jnp.int32)

    def cond(state):
        i, _, alive = state
        return jnp.logical_and(i < qi, alive > 0)

    def body(state):
        i, tails, _ = state
        tails = step(qi - 1 - i, tails, False)
        return i + 1, tails, live(tails)

    lax.while_loop(cond, body, (jnp.int32(0), tails, live(tails)))
    for c in heads:
        gate = _silu(z_ref[:, c * d:(c + 1) * d].astype(_F32))
        o_ref[:, c * d:(c + 1) * d] = (acc_ref[c] * gate).astype(o_ref.dtype)


def _flash_buffers(chains, n):
    t = SM_BLOCK
    return [pltpu.VMEM((chains, t, n), _F32), pltpu.VMEM((chains, t, n), _F32),
            pltpu.VMEM((chains, t, n), _BF), pltpu.VMEM((chains, t, n), _BF),
            pltpu.VMEM((chains, V_ROWS, n), _F32)]


def _head_spec(t, col0, nq):
    return pl.BlockSpec((t, HEAD_DIM), lambda b, h, qi: (b * nq + qi, col0 + h))


def _seq_spec(seq, col0, per_block=1):
    return pl.BlockSpec((seq, HEAD_DIM), lambda b, h, qi: (b, col0 + h // per_block))


def _odd_mixer(proj, bsz, seq, d_inner):
    t = STICK_BLOCK
    w = STICK_HEADS * HEAD_DIM
    groups = d_inner // w
    nq = seq // t
    tile = lambda col0: pl.BlockSpec((t, w), lambda b, g, qi: (b * nq + qi, col0 + g))
    whole = lambda col0: pl.BlockSpec((seq, w), lambda b, g, qi: (b, col0 + g))
    return pl.pallas_call(
        _stick_kernel,
        grid=(bsz, groups, nq),
        in_specs=[tile(0), whole(groups), whole(2 * groups), tile(3 * groups)],
        out_specs=tile(0),
        out_shape=jax.ShapeDtypeStruct((bsz * seq, d_inner), _BF),
        scratch_shapes=[pltpu.VMEM((STICK_HEADS, t, HEAD_DIM), _F32)],
        compiler_params=_cparams(("arbitrary", "arbitrary", "arbitrary")),
        name="stick_breaking",
    )(proj, proj, proj, proj)


def _even_mixer(proj, bias_a, bias_b, lam_vec, subln_g, lam_init, bsz, seq, d_inner):
    d_a = d_inner // 2
    h_a = d_a // HEAD_DIM
    h_b = (d_inner - d_a) // HEAD_DIM
    t = SM_BLOCK
    nq = seq // t
    n_off_a = bias_a.shape[2] // t - 1
    n_off_b = bias_b.shape[2] // t - 1
    out_shape = jax.ShapeDtypeStruct((bsz * seq, d_a), _BF)
    params = _cparams(("arbitrary", "arbitrary", "arbitrary"))
    z0 = 3 * d_inner // HEAD_DIM

    o_a = pl.pallas_call(
        functools.partial(_dilated_kernel, n_off=n_off_a),
        grid=(bsz, h_a, nq),
        in_specs=[
            _head_spec(t, 0, nq),
            _seq_spec(seq, h_a),
            _seq_spec(seq, 2 * h_a),
            _head_spec(t, z0, nq),
            pl.BlockSpec((1, t, (n_off_a + 1) * t), lambda b, h, qi: (h, 0, 0)),
        ],
        out_specs=_head_spec(t, 0, nq),
        out_shape=out_shape,
        scratch_shapes=[pltpu.VMEM((V_ROWS, seq), _BF)] + _flash_buffers(2, t // 2),
        compiler_params=params,
        name="dilated_attention",
    )(proj, proj, proj, proj, bias_a)

    c0 = 3 * h_a
    half = h_b // 2
    o_b = pl.pallas_call(
        functools.partial(_diff_kernel, lam_init=lam_init),
        grid=(bsz, h_b, nq),
        in_specs=[
            pl.BlockSpec((t, HEAD_DIM), lambda b, h, qi: (b * nq + qi, c0 + h // 2)),
            pl.BlockSpec((t, HEAD_DIM), lambda b, h, qi: (b * nq + qi, c0 + half + h // 2)),
            _seq_spec(seq, c0 + 2 * half, 2),
            _seq_spec(seq, c0 + 3 * half, 2),
            _seq_spec(seq, c0 + 4 * half),
            _head_spec(t, z0 + h_a, nq),
            pl.BlockSpec((1, t, (n_off_b + 1) * t), lambda b, h, qi: (h, 0, 0)),
            pl.BlockSpec((4, DIFF_QK_DIM), lambda b, h, qi: (0, 0)),
            pl.BlockSpec((1, HEAD_DIM), lambda b, h, qi: (0, 0)),
        ],
        out_specs=_head_spec(t, 0, nq),
        out_shape=out_shape,
        scratch_shapes=[pltpu.VMEM((V_ROWS, seq), _BF)] + _flash_buffers(2, t),
        compiler_params=params,
        name="diff_attention",
    )(proj, proj, proj, proj, proj, proj, bias_b, lam_vec, subln_g.reshape(1, HEAD_DIM))
    return o_a, o_b


def kernel(x, c, norm_g, w_mod, b_mod, w_in, w_out, rel_bias, diff_lambda, diff_subln_g, final_norm_g):
    bsz, seq, d = x.shape
    depth = w_in.shape[0]
    d_inner = w_out.shape[1]
    h_a = d_inner // 2 // HEAD_DIM
    n_blk = seq // SM_BLOCK

    mod = _modulation(c, w_mod, b_mod)

    table = rel_bias[_t5_bucket(jnp.arange(seq))].T
    n_off_a = min(n_blk, DILATED_PATTERNS[-1][0] // SM_BLOCK + 1)
    vec_a = table[:h_a, : n_off_a * SM_BLOCK] + _dilated_log_multiplicity(n_off_a * SM_BLOCK)[None]
    bias_a = _skewed_bias(vec_a * LOG2E, n_off_a)
    bias_b = _skewed_bias(table[h_a:] * LOG2E, n_blk)

    h = x.reshape(bsz * seq, d)
    for layer in range(depth):
        shift = mod[layer, :, 0:d].reshape(bsz, 1, d)
        scale = mod[layer, :, d:2 * d].reshape(bsz, 1, d)
        gate = mod[layer, :, 2 * d:].reshape(bsz, 1, d)
        u = _norm_mod(h, norm_g[layer], scale, shift, seq)
        proj = _in_proj(u, w_in, layer)
        if layer % 2 == 0:
            e = layer // 2
            lam_init = 0.8 - 0.6 * math.exp(-0.3 * layer)
            mixed = _even_mixer(proj, bias_a, bias_b, diff_lambda[e], diff_subln_g[e], lam_init,
                                bsz, seq, d_inner)
        else:
            mixed = (_odd_mixer(proj, bsz, seq, d_inner),)
        h = _out_proj(mixed, w_out, layer, h, gate, seq)
    return _final_norm(h, final_norm_g).reshape(bsz, seq, d)
```

```python
import functools
import math

import jax
import jax.numpy as jnp
from jax import lax
from jax.experimental import pallas as pl
from jax.experimental.pallas import tpu as pltpu

HEAD_DIM = 128
DIFF_QK_DIM = 64
N_BUCKETS = 32
MAX_DISTANCE = 2048
DILATED_PATTERNS = ((128, 1), (512, 4), (2048, 16))
EPS = 1e-6
SM_BLOCK = 512
STICK_BLOCK = 256
STICK_HEADS = 4
STICK_SKIP_LOG2 = -150.0
NEG_BIG = -1e30
LOG2E = math.log2(math.e)
SUBLANES = 8
V_ROWS = HEAD_DIM + 2 * SUBLANES
VMEM_LIMIT = 56 * 1024 * 1024
ROW_TILE = 512
MOD_TILE_N = 1024
PROJ_TILE = (1024, 1024)
OUT_TILE = (256, 1024)

_BF = jnp.bfloat16
_F32 = jnp.float32


def _cparams(sem):
    return pltpu.CompilerParams(dimension_semantics=sem, vmem_limit_bytes=VMEM_LIMIT)


def _dot(a, b):
    return jnp.dot(a, b, preferred_element_type=_F32)


def _dot_nt(a, b):
    return lax.dot_general(a, b, (((1,), (1,)), ((), ())), preferred_element_type=_F32)


def _split_hi_lo(x):
    hi = x.astype(_BF)
    lo = (x - hi.astype(_F32)).astype(_BF)
    return hi, lo


def _silu(z):
    return z / (1.0 + jnp.exp(-z))


def _mod_kernel(c_ref, w_ref, b_ref, o_ref):
    a_hi, a_lo = _split_hi_lo(_silu(c_ref[...]))
    w_hi, w_lo = _split_hi_lo(w_ref[0])
    acc = _dot(a_hi, w_hi) + _dot(a_lo, w_hi) + _dot(a_hi, w_lo)
    o_ref[0] = acc + b_ref[0]


def _modulation(c, w_mod, b_mod):
    depth, d, n = w_mod.shape
    bsz = c.shape[0]
    rows = SUBLANES
    tn = MOD_TILE_N if n % MOD_TILE_N == 0 else n
    c_pad = jnp.zeros((rows, d), _F32).at[:bsz].set(c)
    out = pl.pallas_call(
        _mod_kernel,
        grid=(depth, n // tn),
        in_specs=[
            pl.BlockSpec((rows, d), lambda l, j: (0, 0)),
            pl.BlockSpec((1, d, tn), lambda l, j: (l, 0, j)),
            pl.BlockSpec((1, 1, tn), lambda l, j: (l, 0, j)),
        ],
        out_specs=pl.BlockSpec((1, rows, tn), lambda l, j: (l, 0, j)),
        out_shape=jax.ShapeDtypeStruct((depth, rows, n), _F32),
        compiler_params=_cparams(("arbitrary", "arbitrary")),
        name="modulation",
    )(c_pad, w_mod, b_mod.reshape(depth, 1, n))
    return out[:, :bsz]


def _norm_mod_kernel(h_ref, g_ref, scale_ref, shift_ref, o_ref):
    h = h_ref[...]
    inv = lax.rsqrt(jnp.mean(h * h, axis=-1, keepdims=True) + EPS)
    gain = g_ref[...] * (1.0 + scale_ref[0])
    o_ref[...] = (h * inv * gain + shift_ref[0]).astype(o_ref.dtype)


def _norm_mod(h, g, scale, shift, seq):
    m, d = h.shape
    tm = ROW_TILE
    per_b = seq // tm
    return pl.pallas_call(
        _norm_mod_kernel,
        grid=(m // tm,),
        in_specs=[
            pl.BlockSpec((tm, d), lambda i: (i, 0)),
            pl.BlockSpec((1, d), lambda i: (0, 0)),
            pl.BlockSpec((1, 1, d), lambda i: (i // per_b, 0, 0)),
            pl.BlockSpec((1, 1, d), lambda i: (i // per_b, 0, 0)),
        ],
        out_specs=pl.BlockSpec((tm, d), lambda i: (i, 0)),
        out_shape=jax.ShapeDtypeStruct((m, d), _BF),
        compiler_params=_cparams(("arbitrary",)),
        name="norm_mod",
    )(h, g.reshape(1, d), scale, shift)


def _final_norm_kernel(h_ref, g_ref, o_ref):
    h = h_ref[...]
    inv = lax.rsqrt(jnp.mean(h * h, axis=-1, keepdims=True) + EPS)
    o_ref[...] = h * inv * g_ref[...]


def _final_norm(h, g):
    m, d = h.shape
    tm = ROW_TILE
    return pl.pallas_call(
        _final_norm_kernel,
        grid=(m // tm,),
        in_specs=[pl.BlockSpec((tm, d), lambda i: (i, 0)), pl.BlockSpec((1, d), lambda i: (0, 0))],
        out_specs=pl.BlockSpec((tm, d), lambda i: (i, 0)),
        out_shape=jax.ShapeDtypeStruct((m, d), _F32),
        compiler_params=_cparams(("arbitrary",)),
        name="final_norm",
    )(h, g.reshape(1, d))


def _proj_kernel(u_ref, w_ref, o_ref, wbf_ref):
    @pl.when(pl.program_id(1) == 0)
    def _():
        wbf_ref[...] = w_ref[0].astype(_BF)

    o_ref[...] = _dot(u_ref[...], wbf_ref[...]).astype(o_ref.dtype)


def _in_proj(u, w_in, layer):
    m, d = u.shape
    p = w_in.shape[2]
    tm, tn = min(PROJ_TILE[0], m), min(PROJ_TILE[1], p)
    return pl.pallas_call(
        _proj_kernel,
        grid=(p // tn, m // tm),
        in_specs=[
            pl.BlockSpec((tm, d), lambda j, i: (i, 0)),
            pl.BlockSpec((1, d, tn), lambda j, i: (layer, 0, j)),
        ],
        out_specs=pl.BlockSpec((tm, tn), lambda j, i: (i, j)),
        out_shape=jax.ShapeDtypeStruct((m, p), _BF),
        scratch_shapes=[pltpu.VMEM((d, tn), _BF)],
        compiler_params=_cparams(("arbitrary", "arbitrary")),
        name="in_proj",
    )(u, w_in)


def _out_kernel(*refs):
    *a_refs, w_ref, h_ref, gate_ref, o_ref, wbf_ref = refs

    @pl.when(pl.program_id(1) == 0)
    def _():
        wbf_ref[...] = w_ref[0].astype(_BF)

    y, k0 = None, 0
    for a_ref in a_refs:
        k1 = k0 + a_ref.shape[1]
        part = _dot(a_ref[...], wbf_ref[k0:k1, :])
        y = part if y is None else y + part
        k0 = k1
    o_ref[...] = h_ref[...] + gate_ref[0] * y


def _out_proj(parts, w_out, layer, h, gate, seq):
    m = parts[0].shape[0]
    k, d = w_out.shape[1], w_out.shape[2]
    tm, tn = OUT_TILE[0], min(OUT_TILE[1], d)
    per_b = seq // tm
    return pl.pallas_call(
        _out_kernel,
        grid=(d // tn, m // tm),
        in_specs=[pl.BlockSpec((tm, p.shape[1]), lambda j, i: (i, 0)) for p in parts] + [
            pl.BlockSpec((1, k, tn), lambda j, i: (layer, 0, j)),
            pl.BlockSpec((tm, tn), lambda j, i: (i, j)),
            pl.BlockSpec((1, 1, tn), lambda j, i: (i // per_b, 0, j)),
        ],
        out_specs=pl.BlockSpec((tm, tn), lambda j, i: (i, j)),
        out_shape=jax.ShapeDtypeStruct((m, d), _F32),
        scratch_shapes=[pltpu.VMEM((k, tn), _BF)],
        compiler_params=_cparams(("arbitrary", "arbitrary")),
        name="out_proj",
    )(*parts, w_out, h, gate)


def _t5_bucket(dist):
    max_exact = N_BUCKETS // 2
    d = jnp.maximum(dist, 1).astype(_F32)
    large = max_exact + (jnp.log(d / max_exact) / math.log(MAX_DISTANCE / max_exact)
                         * (N_BUCKETS - max_exact)).astype(jnp.int32)
    large = jnp.minimum(large, N_BUCKETS - 1)
    return jnp.where(dist < max_exact, dist, large)


def _dilated_log_multiplicity(n):
    delta = jnp.arange(n)
    mult = jnp.zeros((n,), _F32)
    for window, dil in DILATED_PATTERNS:
        mult = mult + ((delta % dil == 0) & (delta <= window)).astype(_F32)
    return jnp.where(mult > 0, jnp.log(jnp.maximum(mult, 1.0)), NEG_BIG)


def _skewed_bias(bias_vec, n_off):
    t = SM_BLOCK
    h = bias_vec.shape[0]
    c = n_off * t
    width = c + t
    rr = jnp.take(bias_vec, jnp.clip(jnp.arange(width) - t, 0, c - 1), axis=1)
    return pl.pallas_call(
        _skew_kernel,
        grid=(h,),
        in_specs=[pl.BlockSpec((1, 1, width), lambda i: (i, 0, 0))],
        out_specs=pl.BlockSpec((1, t, width), lambda i: (i, 0, 0)),
        out_shape=jax.ShapeDtypeStruct((h, t, width), _F32),
        compiler_params=_cparams(("arbitrary",)),
        name="bias_skew",
    )(rr.reshape(h, 1, width))


def _skew_kernel(rr_ref, o_ref):
    _, t, width = o_ref.shape
    rows = SUBLANES
    base = pltpu.roll(jnp.broadcast_to(rr_ref[0], (rows, width)), 0, 1, stride=1, stride_axis=0)
    for g in range(t // rows):
        o_ref[0, g * rows:(g + 1) * rows, :] = pltpu.roll(base, g * rows, 1)


def _flash_attend(qts, k_refs, col0s, bias_ref, vt_ref, bufs, scale2, lo, qi, finish):
    t = SM_BLOCK
    raw_a, raw_b, p_a, p_b, acc_ref = bufs
    chains = range(len(qts))
    n = qts[0].shape[1]

    def scores_into(ki, s_ref):
        start = pl.multiple_of(ki * t, t)
        bias = _bias_tile(bias_ref, qi - ki)
        tops = []
        for c in chains:
            s = _dot(k_refs[c][pl.ds(start, t), :], qts[c]) * scale2 + bias[:, col0s[c]:col0s[c] + n]
            s_ref[c] = s
            tops.append(jnp.max(s, axis=0, keepdims=True))
        return tuple(tops)

    def softmax_from(s_ref, tops, causal, stats, p_ref):
        new_stats, alphas, ps = [], [], []
        for c in chains:
            m = stats[c]
            s = s_ref[c]
            top = tops[c]
            if causal:
                key = lax.broadcasted_iota(jnp.int32, s.shape, 0)
                qry = lax.broadcasted_iota(jnp.int32, s.shape, 1) + col0s[c]
                s = jnp.where(key <= qry, s, NEG_BIG)
                top = jnp.max(s, axis=0, keepdims=True)
            m_new = jnp.maximum(m, top)
            alphas.append(jnp.exp2(m - m_new))
            p = jnp.exp2(s - m_new)
            new_stats.append(m_new)
            if p_ref is None:
                ps.append(p.astype(_BF))
            else:
                p_ref[c] = p.astype(_BF)
        return tuple(new_stats), tuple(alphas), ps

    def add_weighted_values(ki, p_src, alphas):
        vt_blk = vt_ref[:, pl.ds(pl.multiple_of(ki * t, t), t)]
        for c in chains:
            acc_ref[c] = acc_ref[c] * alphas[c] + _dot(vt_blk, p_src[c])

    def half_step(ki, s_cur, s_next, p_cur, p_prev, tops, stats, alphas):
        add_weighted_values(jnp.maximum(ki - 1, lo), p_prev, alphas)
        tops_next = scores_into(ki + 1, s_next)
        stats, alphas, _ = softmax_from(s_cur, tops, False, stats, p_cur)
        return tops_next, stats, alphas

    def last_steps(s_ref, p_prev, tops, stats, alphas):
        add_weighted_values(jnp.maximum(qi - 1, lo), p_prev, alphas)
        stats, alphas, ps = softmax_from(s_ref, tops, True, stats, None)
        add_weighted_values(qi, ps, alphas)
        finish(tuple((acc_ref[c, HEAD_DIM:HEAD_DIM + 1, :], acc_ref[c, :HEAD_DIM, :]) for c in chains))

    for c in chains:
        acc_ref[c] = jnp.zeros(acc_ref.shape[1:], _F32)
        p_b[c] = jnp.zeros(p_b.shape[1:], _BF)
    tops = scores_into(lo, raw_a)
    stats = tuple(jnp.full((1, n), NEG_BIG, _F32) for _ in chains)
    alphas = tuple(jnp.ones((1, n), _F32) for _ in chains)
    n_full = qi - lo

    def pair(it, carry):
        k0 = lo + 2 * it
        carry = half_step(k0, raw_a, raw_b, p_a, p_b, *carry)
        return half_step(k0 + 1, raw_b, raw_a, p_b, p_a, *carry)

    tops, stats, alphas = lax.fori_loop(0, n_full // 2, pair, (tops, stats, alphas))

    @pl.when(n_full % 2 == 1)
    def _():
        last_steps(raw_b, p_a, *half_step(qi - 1, raw_a, raw_b, p_a, p_b, tops, stats, alphas))

    @pl.when(n_full % 2 == 0)
    def _():
        last_steps(raw_a, p_b, tops, stats, alphas)


def _bias_tile(bias_ref, off):
    t = SM_BLOCK
    return bias_ref[0, :, pl.ds(pl.multiple_of((off + 1) * t, t), t)]


def _fill_v_transposed(v_ref, vt_ref):
    t = SM_BLOCK
    pad = lax.broadcasted_iota(jnp.int32, (V_ROWS - HEAD_DIM, t), 0)
    for c in range(v_ref.shape[0] // t):
        vt_ref[:HEAD_DIM, c * t:(c + 1) * t] = v_ref[c * t:(c + 1) * t, :].astype(_F32).T.astype(_BF)
        vt_ref[HEAD_DIM:, c * t:(c + 1) * t] = jnp.where(pad == 0, 1.0, 0.0).astype(_BF)


def _dilated_kernel(q_ref, k_ref, v_ref, z_ref, bias_ref, o_ref, vt_ref, *bufs, n_off):
    t = SM_BLOCK
    qi = pl.program_id(2)
    qt = q_ref[...].astype(_F32).T.astype(_BF)
    scale2 = HEAD_DIM ** -0.5 * LOG2E

    @pl.when(qi == 0)
    def _():
        _fill_v_transposed(v_ref, vt_ref)

    half = t // 2
    lo = jnp.maximum(qi - (n_off - 1), 0)

    def finish(outs):
        o = jnp.concatenate([acc / l for l, acc in outs], axis=1).T
        o_ref[...] = (o * _silu(z_ref[...].astype(_F32))).astype(o_ref.dtype)

    _flash_attend((qt[:, :half], qt[:, half:]), (k_ref, k_ref), (0, half), bias_ref, vt_ref, bufs, scale2, lo, qi,
                  finish)


def _diff_kernel(q1_ref, q2_ref, k1_ref, k2_ref, v_ref, z_ref, bias_ref, lam_ref, g_ref, o_ref, vt_ref, *bufs,
                 lam_init):
    t = SM_BLOCK
    h = pl.program_id(1)
    qi = pl.program_id(2)
    scale2 = DIFF_QK_DIM ** -0.5 * LOG2E
    dim = lax.broadcasted_iota(jnp.int32, (HEAD_DIM, t), 0)
    mine = (dim // DIFF_QK_DIM) == (h % 2)
    q1t = jnp.where(mine, q1_ref[...].astype(_F32).T, 0.0).astype(_BF)
    q2t = jnp.where(mine, q2_ref[...].astype(_F32).T, 0.0).astype(_BF)

    @pl.when(qi == 0)
    def _():
        _fill_v_transposed(v_ref, vt_ref)

    def finish(outs):
        (l1, a1), (l2, a2) = outs
        lv = lam_ref[...]
        lam = (jnp.exp(jnp.sum(lv[0:1] * lv[1:2], axis=1, keepdims=True))
               - jnp.exp(jnp.sum(lv[2:3] * lv[3:4], axis=1, keepdims=True)) + lam_init)
        o = a1 / l1 - lam * (a2 / l2)
        o = (o * lax.rsqrt(jnp.mean(o * o, axis=0, keepdims=True) + EPS)).T
        o = o * (g_ref[...] * (1.0 - lam_init))
        o_ref[...] = (o * _silu(z_ref[...].astype(_F32))).astype(o_ref.dtype)

    _flash_attend((q1t, q2t), (k1_ref, k2_ref), (0, 0), bias_ref, vt_ref, bufs, scale2, 0, qi, finish)


def _stick_kernel(q_ref, k_ref, v_ref, z_ref, o_ref, acc_ref):
    t = STICK_BLOCK
    d = HEAD_DIM
    heads = range(q_ref.shape[1] // d)
    qi = pl.program_id(2)
    scale2 = d ** -0.5 * LOG2E
    row = lax.broadcasted_iota(jnp.int32, (t, t), 0)
    col = lax.broadcasted_iota(jnp.int32, (t, t), 1)
    later = jnp.where(row > col, 1.0, 0.0).astype(_BF)
    strict = col < row
    qs = [q_ref[:, c * d:(c + 1) * d] for c in heads]

    def step(ki, tails, diagonal):
        start = pl.multiple_of(ki * t, t)
        zs = [_dot_nt(qs[c], k_ref[pl.ds(start, t), c * d:(c + 1) * d]) for c in heads]
        log_betas, parts = [], []
        for c in heads:
            z = zs[c] * scale2
            softplus = jnp.log(1.0 + jnp.exp2(-jnp.abs(z))) * LOG2E
            log_beta = jnp.minimum(z, 0.0) - softplus
            log_1m = log_beta - z
            if diagonal:
                log_1m = jnp.where(strict, log_1m, 0.0)
            log_betas.append(log_beta)
            parts.append(_split_hi_lo(log_1m) + (log_1m[:, 0:1],))
        sums = [_dot(hi, later) + _dot(lo, later) for hi, lo, _ in parts]
        weights = []
        for c in heads:
            a = jnp.exp2(log_betas[c] + (tails[c] + sums[c]))
            if diagonal:
                a = jnp.where(strict, a, 0.0)
            weights.append(a.astype(_BF))
        for c in heads:
            acc_ref[c] = acc_ref[c] + _dot(weights[c], v_ref[pl.ds(start, t), c * d:(c + 1) * d])
        return tuple(tails[c] + (sums[c][:, 0:1] + parts[c][2]) for c in heads)

    for c in heads:
        acc_ref[c] = jnp.zeros((t, d), _F32)
    tails = step(qi, tuple(jnp.zeros((t, 1), _F32) for _ in heads), True)

    def live(tails):
        worst = functools.reduce(jnp.maximum, tails)
        return (jnp.max(worst) > STICK_SKIP_LOG2).astype(jnp.int32)

    def cond(state):
        i, _, alive = state
        return jnp.logical_and(i < qi, alive > 0)

    def body(state):
        i, tails, _ = state
        tails = step(qi - 1 - i, tails, False)
        return i + 1, tails, live(tails)

    lax.while_loop(cond, body, (jnp.int32(0), tails, live(tails)))
    for c in heads:
        gate = _silu(z_ref[:, c * d:(c + 1) * d].astype(_F32))
        o_ref[:, c * d:(c + 1) * d] = (acc_ref[c] * gate).astype(o_ref.dtype)


def _flash_buffers(chains, n):
    t = SM_BLOCK
    return [pltpu.VMEM((chains, t, n), _F32), pltpu.VMEM((chains, t, n), _F32),
            pltpu.VMEM((chains, t, n), _BF), pltpu.VMEM((chains, t, n), _BF),
            pltpu.VMEM((chains, V_ROWS, n), _F32)]


def _head_spec(t, col0, nq):
    return pl.BlockSpec((t, HEAD_DIM), lambda b, h, qi: (b * nq + qi, col0 + h))


def _seq_spec(seq, col0, per_block=1):
    return pl.BlockSpec((seq, HEAD_DIM), lambda b, h, qi: (b, col0 + h // per_block))


def _odd_mixer(proj, bsz, seq, d_inner):
    t = STICK_BLOCK
    w = STICK_HEADS * HEAD_DIM
    groups = d_inner // w
    nq = seq // t
    tile = lambda col0: pl.BlockSpec((t, w), lambda b, g, qi: (b * nq + qi, col0 + g))
    whole = lambda col0: pl.BlockSpec((seq, w), lambda b, g, qi: (b, col0 + g))
    return pl.pallas_call(
        _stick_kernel,
        grid=(bsz, groups, nq),
        in_specs=[tile(0), whole(groups), whole(2 * groups), tile(3 * groups)],
        out_specs=tile(0),
        out_shape=jax.ShapeDtypeStruct((bsz * seq, d_inner), _BF),
        scratch_shapes=[pltpu.VMEM((STICK_HEADS, t, HEAD_DIM), _F32)],
        compiler_params=_cparams(("arbitrary", "arbitrary", "arbitrary")),
        name="stick_breaking",
    )(proj, proj, proj, proj)


def _even_mixer(proj, bias_a, bias_b, lam_vec, subln_g, lam_init, bsz, seq, d_inner):
    d_a = d_inner // 2
    h_a = d_a // HEAD_DIM
    h_b = (d_inner - d_a) // HEAD_DIM
    t = SM_BLOCK
    nq = seq // t
    n_off_a = bias_a.shape[2] // t - 1
    n_off_b = bias_b.shape[2] // t - 1
    out_shape = jax.ShapeDtypeStruct((bsz * seq, d_a), _BF)
    params = _cparams(("arbitrary", "arbitrary", "arbitrary"))
    z0 = 3 * d_inner // HEAD_DIM

    o_a = pl.pallas_call(
        functools.partial(_dilated_kernel, n_off=n_off_a),
        grid=(bsz, h_a, nq),
        in_specs=[
            _head_spec(t, 0, nq),
            _seq_spec(seq, h_a),
            _seq_spec(seq, 2 * h_a),
            _head_spec(t, z0, nq),
            pl.BlockSpec((1, t, (n_off_a + 1) * t), lambda b, h, qi: (h, 0, 0)),
        ],
        out_specs=_head_spec(t, 0, nq),
        out_shape=out_shape,
        scratch_shapes=[pltpu.VMEM((V_ROWS, seq), _BF)] + _flash_buffers(2, t // 2),
        compiler_params=params,
        name="dilated_attention",
    )(proj, proj, proj, proj, bias_a)

    c0 = 3 * h_a
    half = h_b // 2
    o_b = pl.pallas_call(
        functools.partial(_diff_kernel, lam_init=lam_init),
        grid=(bsz, h_b, nq),
        in_specs=[
            pl.BlockSpec((t, HEAD_DIM), lambda b, h, qi: (b * nq + qi, c0 + h // 2)),
            pl.BlockSpec((t, HEAD_DIM), lambda b, h, qi: (b * nq + qi, c0 + half + h // 2)),
            _seq_spec(seq, c0 + 2 * half, 2),
            _seq_spec(seq, c0 + 3 * half, 2),
            _seq_spec(seq, c0 + 4 * half),
            _head_spec(t, z0 + h_a, nq),
            pl.BlockSpec((1, t, (n_off_b + 1) * t), lambda b, h, qi: (h, 0, 0)),
            pl.BlockSpec((4, DIFF_QK_DIM), lambda b, h, qi: (0, 0)),
            pl.BlockSpec((1, HEAD_DIM), lambda b, h, qi: (0, 0)),
        ],
        out_specs=_head_spec(t, 0, nq),
        out_shape=out_shape,
        scratch_shapes=[pltpu.VMEM((V_ROWS, seq), _BF)] + _flash_buffers(2, t),
        compiler_params=params,
        name="diff_attention",
    )(proj, proj, proj, proj, proj, proj, bias_b, lam_vec, subln_g.reshape(1, HEAD_DIM))
    return o_a, o_b


def kernel(x, c, norm_g, w_mod, b_mod, w_in, w_out, rel_bias, diff_lambda, diff_subln_g, final_norm_g):
    bsz, seq, d = x.shape
    depth = w_in.shape[0]
    d_inner = w_out.shape[1]
    h_a = d_inner // 2 // HEAD_DIM
    n_blk = seq // SM_BLOCK

    mod = _modulation(c, w_mod, b_mod)

    table = rel_bias[_t5_bucket(jnp.arange(seq))].T
    n_off_a = min(n_blk, DILATED_PATTERNS[-1][0] // SM_BLOCK + 1)
    vec_a = table[:h_a, : n_off_a * SM_BLOCK] + _dilated_log_multiplicity(n_off_a * SM_BLOCK)[None]
    bias_a = _skewed_bias(vec_a * LOG2E, n_off_a)
    bias_b = _skewed_bias(table[h_a:] * LOG2E, n_blk)

    h = x.reshape(bsz * seq, d)
    for layer in range(depth):
        shift = mod[layer, :, 0:d].reshape(bsz, 1, d)
        scale = mod[layer, :, d:2 * d].reshape(bsz, 1, d)
        gate = mod[layer, :, 2 * d:].reshape(bsz, 1, d)
        u = _norm_mod(h, norm_g[layer], scale, shift, seq)
        proj = _in_proj(u, w_in, layer)
        if layer % 2 == 0:
            e = layer // 2
            lam_init = 0.8 - 0.6 * math.exp(-0.3 * layer)
            mixed = _even_mixer(proj, bias_a, bias_b, diff_lambda[e], diff_subln_g[e], lam_init,
                                bsz, seq, d_inner)
        else:
            mixed = (_odd_mixer(proj, bsz, seq, d_inner),)
        h = _out_proj(mixed, w_out, layer, h, gate, seq)
    return _final_norm(h, final_norm_g).reshape(bsz, seq, d)
```

```python
import functools
import math

import jax
import jax.numpy as jnp
from jax import lax
from jax.experimental import pallas as pl
from jax.experimental.pallas import tpu as pltpu

HEAD_DIM = 128
DIFF_QK_DIM = 64
N_BUCKETS = 32
MAX_DISTANCE = 2048
DILATED_PATTERNS = ((128, 1), (512, 4), (2048, 16))
EPS = 1e-6
SM_BLOCK = 512
STICK_BLOCK = 256
STICK_HEADS = 8
STICK_SKIP_LOG2 = -150.0
NEG_BIG = -1e30
LOG2E = math.log2(math.e)
SUBLANES = 8
V_ROWS = HEAD_DIM + 2 * SUBLANES
VMEM_LIMIT = 56 * 1024 * 1024
ROW_TILE = 512
MOD_TILE_N = 1024
PROJ_TILE = (1024, 1024)
OUT_TILE = (256, 1024)

_BF = jnp.bfloat16
_F32 = jnp.float32


def _cparams(sem):
    return pltpu.CompilerParams(dimension_semantics=sem, vmem_limit_bytes=VMEM_LIMIT)


def _dot(a, b):
    return jnp.dot(a, b, preferred_element_type=_F32)


def _dot_nt(a, b):
    return lax.dot_general(a, b, (((1,), (1,)), ((), ())), preferred_element_type=_F32)


def _split_hi_lo(x):
    hi = x.astype(_BF)
    lo = (x - hi.astype(_F32)).astype(_BF)
    return hi, lo


def _silu(z):
    return z / (1.0 + jnp.exp(-z))


def _mod_kernel(c_ref, w_ref, b_ref, o_ref):
    a_hi, a_lo = _split_hi_lo(_silu(c_ref[...]))
    w_hi, w_lo = _split_hi_lo(w_ref[0])
    acc = _dot(a_hi, w_hi) + _dot(a_lo, w_hi) + _dot(a_hi, w_lo)
    o_ref[0] = acc + b_ref[0]


def _modulation(c, w_mod, b_mod):
    depth, d, n = w_mod.shape
    bsz = c.shape[0]
    rows = SUBLANES
    tn = MOD_TILE_N if n % MOD_TILE_N == 0 else n
    c_pad = jnp.zeros((rows, d), _F32).at[:bsz].set(c)
    out = pl.pallas_call(
        _mod_kernel,
        grid=(depth, n // tn),
        in_specs=[
            pl.BlockSpec((rows, d), lambda l, j: (0, 0)),
            pl.BlockSpec((1, d, tn), lambda l, j: (l, 0, j)),
            pl.BlockSpec((1, 1, tn), lambda l, j: (l, 0, j)),
        ],
        out_specs=pl.BlockSpec((1, rows, tn), lambda l, j: (l, 0, j)),
        out_shape=jax.ShapeDtypeStruct((depth, rows, n), _F32),
        compiler_params=_cparams(("arbitrary", "arbitrary")),
        name="modulation",
    )(c_pad, w_mod, b_mod.reshape(depth, 1, n))
    return out[:, :bsz]


def _norm_mod_kernel(h_ref, g_ref, scale_ref, shift_ref, o_ref):
    h = h_ref[...]
    inv = lax.rsqrt(jnp.mean(h * h, axis=-1, keepdims=True) + EPS)
    gain = g_ref[...] * (1.0 + scale_ref[0])
    o_ref[...] = (h * inv * gain + shift_ref[0]).astype(o_ref.dtype)


def _norm_mod(h, g, scale, shift, seq):
    m, d = h.shape
    tm = ROW_TILE
    per_b = seq // tm
    return pl.pallas_call(
        _norm_mod_kernel,
        grid=(m // tm,),
        in_specs=[
            pl.BlockSpec((tm, d), lambda i: (i, 0)),
            pl.BlockSpec((1, d), lambda i: (0, 0)),
            pl.BlockSpec((1, 1, d), lambda i: (i // per_b, 0, 0)),
            pl.BlockSpec((1, 1, d), lambda i: (i // per_b, 0, 0)),
        ],
        out_specs=pl.BlockSpec((tm, d), lambda i: (i, 0)),
        out_shape=jax.ShapeDtypeStruct((m, d), _BF),
        compiler_params=_cparams(("arbitrary",)),
        name="norm_mod",
    )(h, g.reshape(1, d), scale, shift)


def _final_norm_kernel(h_ref, g_ref, o_ref):
    h = h_ref[...]
    inv = lax.rsqrt(jnp.mean(h * h, axis=-1, keepdims=True) + EPS)
    o_ref[...] = h * inv * g_ref[...]


def _final_norm(h, g):
    m, d = h.shape
    tm = ROW_TILE
    return pl.pallas_call(
        _final_norm_kernel,
        grid=(m // tm,),
        in_specs=[pl.BlockSpec((tm, d), lambda i: (i, 0)), pl.BlockSpec((1, d), lambda i: (0, 0))],
        out_specs=pl.BlockSpec((tm, d), lambda i: (i, 0)),
        out_shape=jax.ShapeDtypeStruct((m, d), _F32),
        compiler_params=_cparams(("arbitrary",)),
        name="final_norm",
    )(h, g.reshape(1, d))


def _proj_kernel(u_ref, w_ref, o_ref, wbf_ref):
    @pl.when(pl.program_id(1) == 0)
    def _():
        wbf_ref[...] = w_ref[0].astype(_BF)

    o_ref[...] = _dot(u_ref[...], wbf_ref[...]).astype(o_ref.dtype)


def _in_proj(u, w_in, layer):
    m, d = u.shape
    p = w_in.shape[2]
    tm, tn = min(PROJ_TILE[0], m), min(PROJ_TILE[1], p)
    return pl.pallas_call(
        _proj_kernel,
        grid=(p // tn, m // tm),
        in_specs=[
            pl.BlockSpec((tm, d), lambda j, i: (i, 0)),
            pl.BlockSpec((1, d, tn), lambda j, i: (layer, 0, j)),
        ],
        out_specs=pl.BlockSpec((tm, tn), lambda j, i: (i, j)),
        out_shape=jax.ShapeDtypeStruct((m, p), _BF),
        scratch_shapes=[pltpu.VMEM((d, tn), _BF)],
        compiler_params=_cparams(("arbitrary", "arbitrary")),
        name="in_proj",
    )(u, w_in)


def _out_kernel(*refs):
    *a_refs, w_ref, h_ref, gate_ref, o_ref, wbf_ref = refs

    @pl.when(pl.program_id(1) == 0)
    def _():
        wbf_ref[...] = w_ref[0].astype(_BF)

    y, k0 = None, 0
    for a_ref in a_refs:
        k1 = k0 + a_ref.shape[1]
        part = _dot(a_ref[...], wbf_ref[k0:k1, :])
        y = part if y is None else y + part
        k0 = k1
    o_ref[...] = h_ref[...] + gate_ref[0] * y


def _out_proj(parts, w_out, layer, h, gate, seq):
    m = parts[0].shape[0]
    k, d = w_out.shape[1], w_out.shape[2]
    tm, tn = OUT_TILE[0], min(OUT_TILE[1], d)
    per_b = seq // tm
    return pl.pallas_call(
        _out_kernel,
        grid=(d // tn, m // tm),
        in_specs=[pl.BlockSpec((tm, p.shape[1]), lambda j, i: (i, 0)) for p in parts] + [
            pl.BlockSpec((1, k, tn), lambda j, i: (layer, 0, j)),
            pl.BlockSpec((tm, tn), lambda j, i: (i, j)),
            pl.BlockSpec((1, 1, tn), lambda j, i: (i // per_b, 0, j)),
        ],
        out_specs=pl.BlockSpec((tm, tn), lambda j, i: (i, j)),
        out_shape=jax.ShapeDtypeStruct((m, d), _F32),
        scratch_shapes=[pltpu.VMEM((k, tn), _BF)],
        compiler_params=_cparams(("arbitrary", "arbitrary")),
        name="out_proj",
    )(*parts, w_out, h, gate)


def _t5_bucket(dist):
    max_exact = N_BUCKETS // 2
    d = jnp.maximum(dist, 1).astype(_F32)
    large = max_exact + (jnp.log(d / max_exact) / math.log(MAX_DISTANCE / max_exact)
                         * (N_BUCKETS - max_exact)).astype(jnp.int32)
    large = jnp.minimum(large, N_BUCKETS - 1)
    return jnp.where(dist < max_exact, dist, large)


def _dilated_log_multiplicity(n):
    delta = jnp.arange(n)
    mult = jnp.zeros((n,), _F32)
    for window, dil in DILATED_PATTERNS:
        mult = mult + ((delta % dil == 0) & (delta <= window)).astype(_F32)
    return jnp.where(mult > 0, jnp.log(jnp.maximum(mult, 1.0)), NEG_BIG)


def _skewed_bias(bias_vec, n_off):
    t = SM_BLOCK
    h = bias_vec.shape[0]
    c = n_off * t
    width = c + t
    rr = jnp.take(bias_vec, jnp.clip(jnp.arange(width) - t, 0, c - 1), axis=1)
    return pl.pallas_call(
        _skew_kernel,
        grid=(h,),
        in_specs=[pl.BlockSpec((1, 1, width), lambda i: (i, 0, 0))],
        out_specs=pl.BlockSpec((1, t, width), lambda i: (i, 0, 0)),
        out_shape=jax.ShapeDtypeStruct((h, t, width), _F32),
        compiler_params=_cparams(("arbitrary",)),
        name="bias_skew",
    )(rr.reshape(h, 1, width))


def _skew_kernel(rr_ref, o_ref):
    _, t, width = o_ref.shape
    rows = SUBLANES
    base = pltpu.roll(jnp.broadcast_to(rr_ref[0], (rows, width)), 0, 1, stride=1, stride_axis=0)
    for g in range(t // rows):
        o_ref[0, g * rows:(g + 1) * rows, :] = pltpu.roll(base, g * rows, 1)


def _flash_attend(qts, k_refs, col0s, bias_ref, vt_ref, bufs, scale2, lo, qi, finish):
    t = SM_BLOCK
    raw_a, raw_b, p_a, p_b, acc_ref = bufs
    chains = range(len(qts))
    n = qts[0].shape[1]

    def scores_into(ki, s_ref):
        start = pl.multiple_of(ki * t, t)
        bias = _bias_tile(bias_ref, qi - ki)
        tops = []
        for c in chains:
            s = _dot(k_refs[c][pl.ds(start, t), :], qts[c]) * scale2 + bias[:, col0s[c]:col0s[c] + n]
            s_ref[c] = s
            tops.append(jnp.max(s, axis=0, keepdims=True))
        return tuple(tops)

    def softmax_from(s_ref, tops, causal, stats, p_ref):
        new_stats, alphas, ps = [], [], []
        for c in chains:
            m = stats[c]
            s = s_ref[c]
            top = tops[c]
            if causal:
                key = lax.broadcasted_iota(jnp.int32, s.shape, 0)
                qry = lax.broadcasted_iota(jnp.int32, s.shape, 1) + col0s[c]
                s = jnp.where(key <= qry, s, NEG_BIG)
                top = jnp.max(s, axis=0, keepdims=True)
            m_new = jnp.maximum(m, top)
            alphas.append(jnp.exp2(m - m_new))
            p = jnp.exp2(s - m_new)
            new_stats.append(m_new)
            if p_ref is None:
                ps.append(p.astype(_BF))
            else:
                p_ref[c] = p.astype(_BF)
        return tuple(new_stats), tuple(alphas), ps

    def add_weighted_values(ki, p_src, alphas):
        vt_blk = vt_ref[:, pl.ds(pl.multiple_of(ki * t, t), t)]
        for c in chains:
            acc_ref[c] = acc_ref[c] * alphas[c] + _dot(vt_blk, p_src[c])

    def half_step(ki, s_cur, s_next, p_cur, p_prev, tops, stats, alphas):
        add_weighted_values(jnp.maximum(ki - 1, lo), p_prev, alphas)
        tops_next = scores_into(ki + 1, s_next)
        stats, alphas, _ = softmax_from(s_cur, tops, False, stats, p_cur)
        return tops_next, stats, alphas

    def last_steps(s_ref, p_prev, tops, stats, alphas):
        add_weighted_values(jnp.maximum(qi - 1, lo), p_prev, alphas)
        stats, alphas, ps = softmax_from(s_ref, tops, True, stats, None)
        add_weighted_values(qi, ps, alphas)
        finish(tuple((acc_ref[c, HEAD_DIM:HEAD_DIM + 1, :], acc_ref[c, :HEAD_DIM, :]) for c in chains))

    for c in chains:
        acc_ref[c] = jnp.zeros(acc_ref.shape[1:], _F32)
        p_b[c] = jnp.zeros(p_b.shape[1:], _BF)
    tops = scores_into(lo, raw_a)
    stats = tuple(jnp.full((1, n), NEG_BIG, _F32) for _ in chains)
    alphas = tuple(jnp.ones((1, n), _F32) for _ in chains)
    n_full = qi - lo

    def pair(it, carry):
        k0 = lo + 2 * it
        carry = half_step(k0, raw_a, raw_b, p_a, p_b, *carry)
        return half_step(k0 + 1, raw_b, raw_a, p_b, p_a, *carry)

    tops, stats, alphas = lax.fori_loop(0, n_full // 2, pair, (tops, stats, alphas))

    @pl.when(n_full % 2 == 1)
    def _():
        last_steps(raw_b, p_a, *half_step(qi - 1, raw_a, raw_b, p_a, p_b, tops, stats, alphas))

    @pl.when(n_full % 2 == 0)
    def _():
        last_steps(raw_a, p_b, tops, stats, alphas)


def _bias_tile(bias_ref, off):
    t = SM_BLOCK
    return bias_ref[0, :, pl.ds(pl.multiple_of((off + 1) * t, t), t)]


def _fill_v_transposed(v_ref, vt_ref):
    t = SM_BLOCK
    pad = lax.broadcasted_iota(jnp.int32, (V_ROWS - HEAD_DIM, t), 0)
    for c in range(v_ref.shape[0] // t):
        vt_ref[:HEAD_DIM, c * t:(c + 1) * t] = v_ref[c * t:(c + 1) * t, :].astype(_F32).T.astype(_BF)
        vt_ref[HEAD_DIM:, c * t:(c + 1) * t] = jnp.where(pad == 0, 1.0, 0.0).astype(_BF)


def _dilated_kernel(q_ref, k_ref, v_ref, z_ref, bias_ref, o_ref, vt_ref, *bufs, n_off):
    t = SM_BLOCK
    qi = pl.program_id(2)
    qt = q_ref[...].astype(_F32).T.astype(_BF)
    scale2 = HEAD_DIM ** -0.5 * LOG2E

    @pl.when(qi == 0)
    def _():
        _fill_v_transposed(v_ref, vt_ref)

    half = t // 2
    lo = jnp.maximum(qi - (n_off - 1), 0)

    def finish(outs):
        o = jnp.concatenate([acc / l for l, acc in outs], axis=1).T
        o_ref[...] = (o * _silu(z_ref[...].astype(_F32))).astype(o_ref.dtype)

    _flash_attend((qt[:, :half], qt[:, half:]), (k_ref, k_ref), (0, half), bias_ref, vt_ref, bufs, scale2, lo, qi,
                  finish)


def _diff_kernel(q1_ref, q2_ref, k1_ref, k2_ref, v_ref, z_ref, bias_ref, lam_ref, g_ref, o_ref, vt_ref, *bufs,
                 lam_init):
    t = SM_BLOCK
    h = pl.program_id(1)
    qi = pl.program_id(2)
    scale2 = DIFF_QK_DIM ** -0.5 * LOG2E
    dim = lax.broadcasted_iota(jnp.int32, (HEAD_DIM, t), 0)
    mine = (dim // DIFF_QK_DIM) == (h % 2)
    q1t = jnp.where(mine, q1_ref[...].astype(_F32).T, 0.0).astype(_BF)
    q2t = jnp.where(mine, q2_ref[...].astype(_F32).T, 0.0).astype(_BF)

    @pl.when(qi == 0)
    def _():
        _fill_v_transposed(v_ref, vt_ref)

    def finish(outs):
        (l1, a1), (l2, a2) = outs
        lv = lam_ref[...]
        lam = (jnp.exp(jnp.sum(lv[0:1] * lv[1:2], axis=1, keepdims=True))
               - jnp.exp(jnp.sum(lv[2:3] * lv[3:4], axis=1, keepdims=True)) + lam_init)
        o = a1 / l1 - lam * (a2 / l2)
        o = (o * lax.rsqrt(jnp.mean(o * o, axis=0, keepdims=True) + EPS)).T
        o = o * (g_ref[...] * (1.0 - lam_init))
        o_ref[...] = (o * _silu(z_ref[...].astype(_F32))).astype(o_ref.dtype)

    _flash_attend((q1t, q2t), (k1_ref, k2_ref), (0, 0), bias_ref, vt_ref, bufs, scale2, 0, qi, finish)


def _stick_kernel(q_ref, k_ref, v_ref, z_ref, o_ref, acc_ref):
    t = STICK_BLOCK
    d = HEAD_DIM
    heads = range(q_ref.shape[1] // d)
    qi = pl.program_id(2)
    scale2 = d ** -0.5 * LOG2E
    row = lax.broadcasted_iota(jnp.int32, (t, t), 0)
    col = lax.broadcasted_iota(jnp.int32, (t, t), 1)
    later = jnp.where(row > col, 1.0, 0.0).astype(_BF)
    strict = col < row
    qs = [q_ref[:, c * d:(c + 1) * d] for c in heads]

    def step(ki, tails, diagonal):
        start = pl.multiple_of(ki * t, t)
        zs = [_dot_nt(qs[c], k_ref[pl.ds(start, t), c * d:(c + 1) * d]) for c in heads]
        log_betas, parts = [], []
        for c in heads:
            z = zs[c] * scale2
            softplus = jnp.log(1.0 + jnp.exp2(-jnp.abs(z))) * LOG2E
            log_beta = jnp.minimum(z, 0.0) - softplus
            log_1m = log_beta - z
            if diagonal:
                log_1m = jnp.where(strict, log_1m, 0.0)
            log_betas.append(log_beta)
            parts.append(_split_hi_lo(log_1m) + (log_1m[:, 0:1],))
        sums = [_dot(hi, later) + _dot(lo, later) for hi, lo, _ in parts]
        weights = []
        for c in heads:
            a = jnp.exp2(log_betas[c] + (tails[c] + sums[c]))
            if diagonal:
                a = jnp.where(strict, a, 0.0)
            weights.append(a.astype(_BF))
        for c in heads:
            acc_ref[c] = acc_ref[c] + _dot(weights[c], v_ref[pl.ds(start, t), c * d:(c + 1) * d])
        return tuple(tails[c] + (sums[c][:, 0:1] + parts[c][2]) for c in heads)

    for c in heads:
        acc_ref[c] = jnp.zeros((t, d), _F32)
    tails = step(qi, tuple(jnp.zeros((t, 1), _F32) for _ in heads), True)

    def live(tails):
        worst = functools.reduce(jnp.maximum, tails)
        return (jnp.max(worst) > STICK_SKIP_LOG2).astype(jnp.int32)

    def cond(state):
        i, _, alive = state
        return jnp.logical_and(i < qi, alive > 0)

    def body(state):
        i, tails, _ = state
        tails = step(qi - 1 - i, tails, False)
        return i + 1, tails, live(tails)

    lax.while_loop(cond, body, (jnp.int32(0), tails, live(tails)))
    for c in heads:
        gate = _silu(z_ref[:, c * d:(c + 1) * d].astype(_F32))
        o_ref[:, c * d:(c + 1) * d] = (acc_ref[c] * gate).astype(o_ref.dtype)


def _flash_buffers(chains, n):
    t = SM_BLOCK
    return [pltpu.VMEM((chains, t, n), _F32), pltpu.VMEM((chains, t, n), _F32),
            pltpu.VMEM((chains, t, n), _BF), pltpu.VMEM((chains, t, n), _BF),
            pltpu.VMEM((chains, V_ROWS, n), _F32)]


def _head_spec(t, col0, nq):
    return pl.BlockSpec((t, HEAD_DIM), lambda b, h, qi: (b * nq + qi, col0 + h))


def _seq_spec(seq, col0, per_block=1):
    return pl.BlockSpec((seq, HEAD_DIM), lambda b, h, qi: (b, col0 + h // per_block))


def _odd_mixer(proj, bsz, seq, d_inner):
    t = STICK_BLOCK
    w = STICK_HEADS * HEAD_DIM
    groups = d_inner // w
    nq = seq // t
    tile = lambda col0: pl.BlockSpec((t, w), lambda b, g, qi: (b * nq + qi, col0 + g))
    whole = lambda col0: pl.BlockSpec((seq, w), lambda b, g, qi: (b, col0 + g))
    return pl.pallas_call(
        _stick_kernel,
        grid=(bsz, groups, nq),
        in_specs=[tile(0), whole(groups), whole(2 * groups), tile(3 * groups)],
        out_specs=tile(0),
        out_shape=jax.ShapeDtypeStruct((bsz * seq, d_inner), _BF),
        scratch_shapes=[pltpu.VMEM((STICK_HEADS, t, HEAD_DIM), _F32)],
        compiler_params=_cparams(("arbitrary", "arbitrary", "arbitrary")),
        name="stick_breaking",
    )(proj, proj, proj, proj)


def _even_mixer(proj, bias_a, bias_b, lam_vec, subln_g, lam_init, bsz, seq, d_inner):
    d_a = d_inner // 2
    h_a = d_a // HEAD_DIM
    h_b = (d_inner - d_a) // HEAD_DIM
    t = SM_BLOCK
    nq = seq // t
    n_off_a = bias_a.shape[2] // t - 1
    n_off_b = bias_b.shape[2] // t - 1
    out_shape = jax.ShapeDtypeStruct((bsz * seq, d_a), _BF)
    params = _cparams(("arbitrary", "arbitrary", "arbitrary"))
    z0 = 3 * d_inner // HEAD_DIM

    o_a = pl.pallas_call(
        functools.partial(_dilated_kernel, n_off=n_off_a),
        grid=(bsz, h_a, nq),
        in_specs=[
            _head_spec(t, 0, nq),
            _seq_spec(seq, h_a),
            _seq_spec(seq, 2 * h_a),
            _head_spec(t, z0, nq),
            pl.BlockSpec((1, t, (n_off_a + 1) * t), lambda b, h, qi: (h, 0, 0)),
        ],
        out_specs=_head_spec(t, 0, nq),
        out_shape=out_shape,
        scratch_shapes=[pltpu.VMEM((V_ROWS, seq), _BF)] + _flash_buffers(2, t // 2),
        compiler_params=params,
        name="dilated_attention",
    )(proj, proj, proj, proj, bias_a)

    c0 = 3 * h_a
    half = h_b // 2
    o_b = pl.pallas_call(
        functools.partial(_diff_kernel, lam_init=lam_init),
        grid=(bsz, h_b, nq),
        in_specs=[
            pl.BlockSpec((t, HEAD_DIM), lambda b, h, qi: (b * nq + qi, c0 + h // 2)),
            pl.BlockSpec((t, HEAD_DIM), lambda b, h, qi: (b * nq + qi, c0 + half + h // 2)),
            _seq_spec(seq, c0 + 2 * half, 2),
            _seq_spec(seq, c0 + 3 * half, 2),
            _seq_spec(seq, c0 + 4 * half),
            _head_spec(t, z0 + h_a, nq),
            pl.BlockSpec((1, t, (n_off_b + 1) * t), lambda b, h, qi: (h, 0, 0)),
            pl.BlockSpec((4, DIFF_QK_DIM), lambda b, h, qi: (0, 0)),
            pl.BlockSpec((1, HEAD_DIM), lambda b, h, qi: (0, 0)),
        ],
        out_specs=_head_spec(t, 0, nq),
        out_shape=out_shape,
        scratch_shapes=[pltpu.VMEM((V_ROWS, seq), _BF)] + _flash_buffers(2, t),
        compiler_params=params,
        name="diff_attention",
    )(proj, proj, proj, proj, proj, proj, bias_b, lam_vec, subln_g.reshape(1, HEAD_DIM))
    return o_a, o_b


def kernel(x, c, norm_g, w_mod, b_mod, w_in, w_out, rel_bias, diff_lambda, diff_subln_g, final_norm_g):
    bsz, seq, d = x.shape
    depth = w_in.shape[0]
    d_inner = w_out.shape[1]
    h_a = d_inner // 2 // HEAD_DIM
    n_blk = seq // SM_BLOCK

    mod = _modulation(c, w_mod, b_mod)

    table = rel_bias[_t5_bucket(jnp.arange(seq))].T
    n_off_a = min(n_blk, DILATED_PATTERNS[-1][0] // SM_BLOCK + 1)
    vec_a = table[:h_a, : n_off_a * SM_BLOCK] + _dilated_log_multiplicity(n_off_a * SM_BLOCK)[None]
    bias_a = _skewed_bias(vec_a * LOG2E, n_off_a)
    bias_b = _skewed_bias(table[h_a:] * LOG2E, n_blk)

    h = x.reshape(bsz * seq, d)
    for layer in range(depth):
        shift = mod[layer, :, 0:d].reshape(bsz, 1, d)
        scale = mod[layer, :, d:2 * d].reshape(bsz, 1, d)
        gate = mod[layer, :, 2 * d:].reshape(bsz, 1, d)
        u = _norm_mod(h, norm_g[layer], scale, shift, seq)
        proj = _in_proj(u, w_in, layer)
        if layer % 2 == 0:
            e = layer // 2
            lam_init = 0.8 - 0.6 * math.exp(-0.3 * layer)
            mixed = _even_mixer(proj, bias_a, bias_b, diff_lambda[e], diff_subln_g[e], lam_init,
                                bsz, seq, d_inner)
        else:
            mixed = (_odd_mixer(proj, bsz, seq, d_inner),)
        h = _out_proj(mixed, w_out, layer, h, gate, seq)
    return _final_norm(h, final_norm_g).reshape(bsz, seq, d)
```

```python
import functools
import math

import jax
import jax.numpy as jnp
from jax import lax
from jax.experimental import pallas as pl
from jax.experimental.pallas import tpu as pltpu

HEAD_DIM = 128
DIFF_QK_DIM = 64
N_BUCKETS = 32
MAX_DISTANCE = 2048
DILATED_PATTERNS = ((128, 1), (512, 4), (2048, 16))
EPS = 1e-6
SM_BLOCK = 512
STICK_BLOCK = 256
STICK_HEADS = 8
STICK_SKIP_LOG2 = -150.0
NEG_BIG = -1e30
LOG2E = math.log2(math.e)
SUBLANES = 8
V_ROWS = HEAD_DIM + 2 * SUBLANES
VMEM_LIMIT = 56 * 1024 * 1024
ROW_TILE = 512
MOD_TILE_N = 1024
PROJ_TILE = (1024, 1024)
OUT_TILE = (256, 1024)

_BF = jnp.bfloat16
_F32 = jnp.float32


def _cparams(sem):
    return pltpu.CompilerParams(dimension_semantics=sem, vmem_limit_bytes=VMEM_LIMIT)


def _dot(a, b):
    return jnp.dot(a, b, preferred_element_type=_F32)


def _dot_nt(a, b):
    return lax.dot_general(a, b, (((1,), (1,)), ((), ())), preferred_element_type=_F32)


def _split_hi_lo(x):
    hi = x.astype(_BF)
    lo = (x - hi.astype(_F32)).astype(_BF)
    return hi, lo


def _silu(z):
    return z / (1.0 + jnp.exp(-z))


def _mod_kernel(c_ref, w_ref, b_ref, o_ref):
    a_hi, a_lo = _split_hi_lo(_silu(c_ref[...]))
    w_hi, w_lo = _split_hi_lo(w_ref[0])
    acc = _dot(a_hi, w_hi) + _dot(a_lo, w_hi) + _dot(a_hi, w_lo)
    o_ref[0] = acc + b_ref[0]


def _modulation(c, w_mod, b_mod):
    depth, d, n = w_mod.shape
    bsz = c.shape[0]
    rows = SUBLANES
    tn = MOD_TILE_N if n % MOD_TILE_N == 0 else n
    c_pad = jnp.zeros((rows, d), _F32).at[:bsz].set(c)
    out = pl.pallas_call(
        _mod_kernel,
        grid=(depth, n // tn),
        in_specs=[
            pl.BlockSpec((rows, d), lambda l, j: (0, 0)),
            pl.BlockSpec((1, d, tn), lambda l, j: (l, 0, j)),
            pl.BlockSpec((1, 1, tn), lambda l, j: (l, 0, j)),
        ],
        out_specs=pl.BlockSpec((1, rows, tn), lambda l, j: (l, 0, j)),
        out_shape=jax.ShapeDtypeStruct((depth, rows, n), _F32),
        compiler_params=_cparams(("arbitrary", "arbitrary")),
        name="modulation",
    )(c_pad, w_mod, b_mod.reshape(depth, 1, n))
    return out[:, :bsz]


def _norm_mod_kernel(h_ref, g_ref, scale_ref, shift_ref, o_ref):
    h = h_ref[...]
    inv = lax.rsqrt(jnp.mean(h * h, axis=-1, keepdims=True) + EPS)
    gain = g_ref[...] * (1.0 + scale_ref[0])
    o_ref[...] = (h * inv * gain + shift_ref[0]).astype(o_ref.dtype)


def _norm_mod(h, g, scale, shift, seq):
    m, d = h.shape
    tm = ROW_TILE
    per_b = seq // tm
    return pl.pallas_call(
        _norm_mod_kernel,
        grid=(m // tm,),
        in_specs=[
            pl.BlockSpec((tm, d), lambda i: (i, 0)),
            pl.BlockSpec((1, d), lambda i: (0, 0)),
            pl.BlockSpec((1, 1, d), lambda i: (i // per_b, 0, 0)),
            pl.BlockSpec((1, 1, d), lambda i: (i // per_b, 0, 0)),
        ],
        out_specs=pl.BlockSpec((tm, d), lambda i: (i, 0)),
        out_shape=jax.ShapeDtypeStruct((m, d), _BF),
        compiler_params=_cparams(("arbitrary",)),
        name="norm_mod",
    )(h, g.reshape(1, d), scale, shift)


def _final_norm_kernel(h_ref, g_ref, o_ref):
    h = h_ref[...]
    inv = lax.rsqrt(jnp.mean(h * h, axis=-1, keepdims=True) + EPS)
    o_ref[...] = h * inv * g_ref[...]


def _final_norm(h, g):
    m, d = h.shape
    tm = ROW_TILE
    return pl.pallas_call(
        _final_norm_kernel,
        grid=(m // tm,),
        in_specs=[pl.BlockSpec((tm, d), lambda i: (i, 0)), pl.BlockSpec((1, d), lambda i: (0, 0))],
        out_specs=pl.BlockSpec((tm, d), lambda i: (i, 0)),
        out_shape=jax.ShapeDtypeStruct((m, d), _F32),
        compiler_params=_cparams(("arbitrary",)),
        name="final_norm",
    )(h, g.reshape(1, d))


def _proj_kernel(u_ref, w_ref, o_ref, wbf_ref):
    @pl.when(pl.program_id(1) == 0)
    def _():
        wbf_ref[...] = w_ref[0].astype(_BF)

    o_ref[...] = _dot(u_ref[...], wbf_ref[...]).astype(o_ref.dtype)


def _in_proj(u, w_in, layer):
    m, d = u.shape
    p = w_in.shape[2]
    tm, tn = min(PROJ_TILE[0], m), min(PROJ_TILE[1], p)
    return pl.pallas_call(
        _proj_kernel,
        grid=(p // tn, m // tm),
        in_specs=[
            pl.BlockSpec((tm, d), lambda j, i: (i, 0)),
            pl.BlockSpec((1, d, tn), lambda j, i: (layer, 0, j)),
        ],
        out_specs=pl.BlockSpec((tm, tn), lambda j, i: (i, j)),
        out_shape=jax.ShapeDtypeStruct((m, p), _BF),
        scratch_shapes=[pltpu.VMEM((d, tn), _BF)],
        compiler_params=_cparams(("arbitrary", "arbitrary")),
        name="in_proj",
    )(u, w_in)


def _out_kernel(*refs):
    *a_refs, w_ref, h_ref, gate_ref, o_ref, wbf_ref = refs

    @pl.when(pl.program_id(1) == 0)
    def _():
        wbf_ref[...] = w_ref[0].astype(_BF)

    y, k0 = None, 0
    for a_ref in a_refs:
        k1 = k0 + a_ref.shape[1]
        part = _dot(a_ref[...], wbf_ref[k0:k1, :])
        y = part if y is None else y + part
        k0 = k1
    o_ref[...] = h_ref[...] + gate_ref[0] * y


def _out_proj(parts, w_out, layer, h, gate, seq):
    m = parts[0].shape[0]
    k, d = w_out.shape[1], w_out.shape[2]
    tm, tn = OUT_TILE[0], min(OUT_TILE[1], d)
    per_b = seq // tm
    return pl.pallas_call(
        _out_kernel,
        grid=(d // tn, m // tm),
        in_specs=[pl.BlockSpec((tm, p.shape[1]), lambda j, i: (i, 0)) for p in parts] + [
            pl.BlockSpec((1, k, tn), lambda j, i: (layer, 0, j)),
            pl.BlockSpec((tm, tn), lambda j, i: (i, j)),
            pl.BlockSpec((1, 1, tn), lambda j, i: (i // per_b, 0, j)),
        ],
        out_specs=pl.BlockSpec((tm, tn), lambda j, i: (i, j)),
        out_shape=jax.ShapeDtypeStruct((m, d), _F32),
        scratch_shapes=[pltpu.VMEM((k, tn), _BF)],
        compiler_params=_cparams(("arbitrary", "arbitrary")),
        name="out_proj",
    )(*parts, w_out, h, gate)


def _t5_bucket(dist):
    max_exact = N_BUCKETS // 2
    d = jnp.maximum(dist, 1).astype(_F32)
    large = max_exact + (jnp.log(d / max_exact) / math.log(MAX_DISTANCE / max_exact)
                         * (N_BUCKETS - max_exact)).astype(jnp.int32)
    large = jnp.minimum(large, N_BUCKETS - 1)
    return jnp.where(dist < max_exact, dist, large)


def _dilated_log_multiplicity(n):
    delta = jnp.arange(n)
    mult = jnp.zeros((n,), _F32)
    for window, dil in DILATED_PATTERNS:
        mult = mult + ((delta % dil == 0) & (delta <= window)).astype(_F32)
    return jnp.where(mult > 0, jnp.log(jnp.maximum(mult, 1.0)), NEG_BIG)


def _skewed_bias(bias_vec, n_off):
    t = SM_BLOCK
    h = bias_vec.shape[0]
    c = n_off * t
    width = c + t
    rr = jnp.take(bias_vec, jnp.clip(jnp.arange(width) - t, 0, c - 1), axis=1)
    return pl.pallas_call(
        _skew_kernel,
        grid=(h,),
        in_specs=[pl.BlockSpec((1, 1, width), lambda i: (i, 0, 0))],
        out_specs=pl.BlockSpec((1, t, width), lambda i: (i, 0, 0)),
        out_shape=jax.ShapeDtypeStruct((h, t, width), _F32),
        compiler_params=_cparams(("arbitrary",)),
        name="bias_skew",
    )(rr.reshape(h, 1, width))


def _skew_kernel(rr_ref, o_ref):
    _, t, width = o_ref.shape
    rows = SUBLANES
    base = pltpu.roll(jnp.broadcast_to(rr_ref[0], (rows, width)), 0, 1, stride=1, stride_axis=0)
    for g in range(t // rows):
        o_ref[0, g * rows:(g + 1) * rows, :] = pltpu.roll(base, g * rows, 1)


def _flash_attend(qts, k_refs, col0s, bias_ref, vt_ref, bufs, scale2, lo, qi, finish):
    t = SM_BLOCK
    raw_a, raw_b, p_a, p_b, acc_ref = bufs
    chains = range(len(qts))
    n = qts[0].shape[1]

    def scores_into(ki, s_ref):
        start = pl.multiple_of(ki * t, t)
        bias = _bias_tile(bias_ref, qi - ki)
        tops = []
        for c in chains:
            s = _dot(k_refs[c][pl.ds(start, t), :], qts[c]) * scale2 + bias[:, col0s[c]:col0s[c] + n]
            s_ref[c] = s
            tops.append(jnp.max(s, axis=0, keepdims=True))
        return tuple(tops)

    def softmax_from(s_ref, tops, causal, stats, p_ref):
        new_stats, alphas, ps = [], [], []
        for c in chains:
            m = stats[c]
            s = s_ref[c]
            top = tops[c]
            if causal:
                key = lax.broadcasted_iota(jnp.int32, s.shape, 0)
                qry = lax.broadcasted_iota(jnp.int32, s.shape, 1) + col0s[c]
                s = jnp.where(key <= qry, s, NEG_BIG)
                top = jnp.max(s, axis=0, keepdims=True)
            m_new = jnp.maximum(m, top)
            alphas.append(jnp.exp2(m - m_new))
            p = jnp.exp2(s - m_new)
            new_stats.append(m_new)
            if p_ref is None:
                ps.append(p.astype(_BF))
            else:
                p_ref[c] = p.astype(_BF)
        return tuple(new_stats), tuple(alphas), ps

    def add_weighted_values(ki, p_src, alphas):
        vt_blk = vt_ref[:, pl.ds(pl.multiple_of(ki * t, t), t)]
        for c in chains:
            acc_ref[c] = acc_ref[c] * alphas[c] + _dot(vt_blk, p_src[c])

    def half_step(ki, s_cur, s_next, p_cur, p_prev, tops, stats, alphas):
        add_weighted_values(jnp.maximum(ki - 1, lo), p_prev, alphas)
        tops_next = scores_into(ki + 1, s_next)
        stats, alphas, _ = softmax_from(s_cur, tops, False, stats, p_cur)
        return tops_next, stats, alphas

    def last_steps(s_ref, p_prev, tops, stats, alphas):
        add_weighted_values(jnp.maximum(qi - 1, lo), p_prev, alphas)
        stats, alphas, ps = softmax_from(s_ref, tops, True, stats, None)
        add_weighted_values(qi, ps, alphas)
        finish(tuple((acc_ref[c, HEAD_DIM:HEAD_DIM + 1, :], acc_ref[c, :HEAD_DIM, :]) for c in chains))

    for c in chains:
        acc_ref[c] = jnp.zeros(acc_ref.shape[1:], _F32)
        p_b[c] = jnp.zeros(p_b.shape[1:], _BF)
    tops = scores_into(lo, raw_a)
    stats = tuple(jnp.full((1, n), NEG_BIG, _F32) for _ in chains)
    alphas = tuple(jnp.ones((1, n), _F32) for _ in chains)
    n_full = qi - lo

    def pair(it, carry):
        k0 = lo + 2 * it
        carry = half_step(k0, raw_a, raw_b, p_a, p_b, *carry)
        return half_step(k0 + 1, raw_b, raw_a, p_b, p_a, *carry)

    tops, stats, alphas = lax.fori_loop(0, n_full // 2, pair, (tops, stats, alphas))

    @pl.when(n_full % 2 == 1)
    def _():
        last_steps(raw_b, p_a, *half_step(qi - 1, raw_a, raw_b, p_a, p_b, tops, stats, alphas))

    @pl.when(n_full % 2 == 0)
    def _():
        last_steps(raw_a, p_b, tops, stats, alphas)


def _bias_tile(bias_ref, off):
    t = SM_BLOCK
    return bias_ref[0, :, pl.ds(pl.multiple_of((off + 1) * t, t), t)]


def _fill_v_transposed(v_ref, vt_ref):
    t = SM_BLOCK
    pad = lax.broadcasted_iota(jnp.int32, (V_ROWS - HEAD_DIM, t), 0)
    for c in range(v_ref.shape[0] // t):
        vt_ref[:HEAD_DIM, c * t:(c + 1) * t] = v_ref[c * t:(c + 1) * t, :].astype(_F32).T.astype(_BF)
        vt_ref[HEAD_DIM:, c * t:(c + 1) * t] = jnp.where(pad == 0, 1.0, 0.0).astype(_BF)


def _dilated_kernel(q_ref, k_ref, v_ref, z_ref, bias_ref, o_ref, vt_ref, *bufs, n_off):
    t = SM_BLOCK
    half = t // 2
    scale2 = HEAD_DIM ** -0.5 * LOG2E
    _fill_v_transposed(v_ref, vt_ref)

    def query_block(qi, carry):
        rows = pl.ds(pl.multiple_of(qi * t, t), t)
        qt = q_ref[rows, :].astype(_F32).T.astype(_BF)

        def finish(outs):
            o = jnp.concatenate([acc / l for l, acc in outs], axis=1).T
            o_ref[rows, :] = (o * _silu(z_ref[rows, :].astype(_F32))).astype(o_ref.dtype)

        _flash_attend((qt[:, :half], qt[:, half:]), (k_ref, k_ref), (0, half), bias_ref, vt_ref, bufs, scale2,
                      jnp.maximum(qi - (n_off - 1), 0), qi, finish)
        return carry

    lax.fori_loop(0, q_ref.shape[0] // t, query_block, 0)


def _diff_kernel(q1_ref, q2_ref, k1_ref, k2_ref, v_ref, z_ref, bias_ref, lam_ref, g_ref, o_ref, vt_ref, *bufs,
                 lam_init):
    t = SM_BLOCK
    h = pl.program_id(1)
    scale2 = DIFF_QK_DIM ** -0.5 * LOG2E
    dim = lax.broadcasted_iota(jnp.int32, (HEAD_DIM, t), 0)
    mine = (dim // DIFF_QK_DIM) == (h % 2)
    _fill_v_transposed(v_ref, vt_ref)
    lv = lam_ref[...]
    lam = (jnp.exp(jnp.sum(lv[0:1] * lv[1:2], axis=1, keepdims=True))
           - jnp.exp(jnp.sum(lv[2:3] * lv[3:4], axis=1, keepdims=True)) + lam_init)

    def query_block(qi, carry):
        rows = pl.ds(pl.multiple_of(qi * t, t), t)
        q1t = jnp.where(mine, q1_ref[rows, :].astype(_F32).T, 0.0).astype(_BF)
        q2t = jnp.where(mine, q2_ref[rows, :].astype(_F32).T, 0.0).astype(_BF)

        def finish(outs):
            (l1, a1), (l2, a2) = outs
            o = a1 / l1 - lam * (a2 / l2)
            o = (o * lax.rsqrt(jnp.mean(o * o, axis=0, keepdims=True) + EPS)).T
            o = o * (g_ref[...] * (1.0 - lam_init))
            o_ref[rows, :] = (o * _silu(z_ref[rows, :].astype(_F32))).astype(o_ref.dtype)

        _flash_attend((q1t, q2t), (k1_ref, k2_ref), (0, 0), bias_ref, vt_ref, bufs, scale2, 0, qi, finish)
        return carry

    lax.fori_loop(0, q1_ref.shape[0] // t, query_block, 0)


def _stick_kernel(q_ref, k_ref, v_ref, z_ref, o_ref, acc_ref):
    t = STICK_BLOCK
    d = HEAD_DIM
    heads = range(q_ref.shape[1] // d)
    qi = pl.program_id(2)
    scale2 = d ** -0.5 * LOG2E
    row = lax.broadcasted_iota(jnp.int32, (t, t), 0)
    col = lax.broadcasted_iota(jnp.int32, (t, t), 1)
    later = jnp.where(row > col, 1.0, 0.0).astype(_BF)
    strict = col < row
    qs = [q_ref[:, c * d:(c + 1) * d] for c in heads]

    def step(ki, tails, diagonal):
        start = pl.multiple_of(ki * t, t)
        zs = [_dot_nt(qs[c], k_ref[pl.ds(start, t), c * d:(c + 1) * d]) for c in heads]
        log_betas, parts = [], []
        for c in heads:
            z = zs[c] * scale2
            softplus = jnp.log(1.0 + jnp.exp2(-jnp.abs(z))) * LOG2E
            log_beta = jnp.minimum(z, 0.0) - softplus
            log_1m = log_beta - z
            if diagonal:
                log_1m = jnp.where(strict, log_1m, 0.0)
            log_betas.append(log_beta)
            parts.append(_split_hi_lo(log_1m) + (log_1m[:, 0:1],))
        sums = [_dot(hi, later) + _dot(lo, later) for hi, lo, _ in parts]
        weights = []
        for c in heads:
            a = jnp.exp2(log_betas[c] + (tails[c] + sums[c]))
            if diagonal:
                a = jnp.where(strict, a, 0.0)
            weights.append(a.astype(_BF))
        for c in heads:
            acc_ref[c] = acc_ref[c] + _dot(weights[c], v_ref[pl.ds(start, t), c * d:(c + 1) * d])
        return tuple(tails[c] + (sums[c][:, 0:1] + parts[c][2]) for c in heads)

    for c in heads:
        acc_ref[c] = jnp.zeros((t, d), _F32)
    tails = step(qi, tuple(jnp.zeros((t, 1), _F32) for _ in heads), True)

    def live(tails):
        worst = functools.reduce(jnp.maximum, tails)
        return (jnp.max(worst) > STICK_SKIP_LOG2).astype(jnp.int32)

    def cond(state):
        i, _, alive = state
        return jnp.logical_and(i < qi, alive > 0)

    def body(state):
        i, tails, _ = state
        tails = step(qi - 1 - i, tails, False)
        return i + 1, tails, live(tails)

    lax.while_loop(cond, body, (jnp.int32(0), tails, live(tails)))
    for c in heads:
        gate = _silu(z_ref[:, c * d:(c + 1) * d].astype(_F32))
        o_ref[:, c * d:(c + 1) * d] = (acc_ref[c] * gate).astype(o_ref.dtype)


def _flash_buffers(chains, n):
    t = SM_BLOCK
    return [pltpu.VMEM((chains, t, n), _F32), pltpu.VMEM((chains, t, n), _F32),
            pltpu.VMEM((chains, t, n), _BF), pltpu.VMEM((chains, t, n), _BF),
            pltpu.VMEM((chains, V_ROWS, n), _F32)]


def _odd_mixer(proj, bsz, seq, d_inner):
    t = STICK_BLOCK
    w = STICK_HEADS * HEAD_DIM
    groups = d_inner // w
    nq = seq // t
    tile = lambda col0: pl.BlockSpec((t, w), lambda b, g, qi: (b * nq + qi, col0 + g))
    whole = lambda col0: pl.BlockSpec((seq, w), lambda b, g, qi: (b, col0 + g))
    return pl.pallas_call(
        _stick_kernel,
        grid=(bsz, groups, nq),
        in_specs=[tile(0), whole(groups), whole(2 * groups), tile(3 * groups)],
        out_specs=tile(0),
        out_shape=jax.ShapeDtypeStruct((bsz * seq, d_inner), _BF),
        scratch_shapes=[pltpu.VMEM((STICK_HEADS, t, HEAD_DIM), _F32)],
        compiler_params=_cparams(("arbitrary", "arbitrary", "arbitrary")),
        name="stick_breaking",
    )(proj, proj, proj, proj)


def _even_mixer(proj, bias_a, bias_b, lam_vec, subln_g, lam_init, bsz, seq, d_inner):
    d_a = d_inner // 2
    h_a = d_a // HEAD_DIM
    h_b = (d_inner - d_a) // HEAD_DIM
    t = SM_BLOCK
    n_off_a = bias_a.shape[2] // t - 1
    n_off_b = bias_b.shape[2] // t - 1
    out_shape = jax.ShapeDtypeStruct((bsz * seq, d_a), _BF)
    params = _cparams(("arbitrary", "arbitrary"))
    z0 = 3 * d_inner // HEAD_DIM
    col = lambda col0, per_block=1: pl.BlockSpec((seq, HEAD_DIM), lambda b, h: (b, col0 + h // per_block))
    bias_spec = lambda n_off: pl.BlockSpec((1, t, (n_off + 1) * t), lambda b, h: (h, 0, 0))

    o_a = pl.pallas_call(
        functools.partial(_dilated_kernel, n_off=n_off_a),
        grid=(bsz, h_a),
        in_specs=[col(0), col(h_a), col(2 * h_a), col(z0), bias_spec(n_off_a)],
        out_specs=col(0),
        out_shape=out_shape,
        scratch_shapes=[pltpu.VMEM((V_ROWS, seq), _BF)] + _flash_buffers(2, t // 2),
        compiler_params=params,
        name="dilated_attention",
    )(proj, proj, proj, proj, bias_a)

    c0 = 3 * h_a
    half = h_b // 2
    o_b = pl.pallas_call(
        functools.partial(_diff_kernel, lam_init=lam_init),
        grid=(bsz, h_b),
        in_specs=[col(c0, 2), col(c0 + half, 2), col(c0 + 2 * half, 2), col(c0 + 3 * half, 2),
                  col(c0 + 4 * half), col(z0 + h_a), bias_spec(n_off_b),
                  pl.BlockSpec((4, DIFF_QK_DIM), lambda b, h: (0, 0)),
                  pl.BlockSpec((1, HEAD_DIM), lambda b, h: (0, 0))],
        out_specs=col(0),
        out_shape=out_shape,
        scratch_shapes=[pltpu.VMEM((V_ROWS, seq), _BF)] + _flash_buffers(2, t),
        compiler_params=params,
        name="diff_attention",
    )(proj, proj, proj, proj, proj, proj, bias_b, lam_vec, subln_g.reshape(1, HEAD_DIM))
    return o_a, o_b


def kernel(x, c, norm_g, w_mod, b_mod, w_in, w_out, rel_bias, diff_lambda, diff_subln_g, final_norm_g):
    bsz, seq, d = x.shape
    depth = w_in.shape[0]
    d_inner = w_out.shape[1]
    h_a = d_inner // 2 // HEAD_DIM
    n_blk = seq // SM_BLOCK

    mod = _modulation(c, w_mod, b_mod)

    table = rel_bias[_t5_bucket(jnp.arange(seq))].T
    n_off_a = min(n_blk, DILATED_PATTERNS[-1][0] // SM_BLOCK + 1)
    vec_a = table[:h_a, : n_off_a * SM_BLOCK] + _dilated_log_multiplicity(n_off_a * SM_BLOCK)[None]
    bias_a = _skewed_bias(vec_a * LOG2E, n_off_a)
    bias_b = _skewed_bias(table[h_a:] * LOG2E, n_blk)

    h = x.reshape(bsz * seq, d)
    for layer in range(depth):
        shift = mod[layer, :, 0:d].reshape(bsz, 1, d)
        scale = mod[layer, :, d:2 * d].reshape(bsz, 1, d)
        gate = mod[layer, :, 2 * d:].reshape(bsz, 1, d)
        u = _norm_mod(h, norm_g[layer], scale, shift, seq)
        proj = _in_proj(u, w_in, layer)
        if layer % 2 == 0:
            e = layer // 2
            lam_init = 0.8 - 0.6 * math.exp(-0.3 * layer)
            mixed = _even_mixer(proj, bias_a, bias_b, diff_lambda[e], diff_subln_g[e], lam_init,
                                bsz, seq, d_inner)
        else:
            mixed = (_odd_mixer(proj, bsz, seq, d_inner),)
        h = _out_proj(mixed, w_out, layer, h, gate, seq)
    return _final_norm(h, final_norm_g).reshape(bsz, seq, d)
```

```python
import functools
import math

import jax
import jax.numpy as jnp
from jax import lax
from jax.experimental import pallas as pl
from jax.experimental.pallas import tpu as pltpu

HEAD_DIM = 128
DIFF_QK_DIM = 64
N_BUCKETS = 32
MAX_DISTANCE = 2048
DILATED_PATTERNS = ((128, 1), (512, 4), (2048, 16))
EPS = 1e-6
SM_BLOCK = 512
STICK_BLOCK = 256
STICK_HEADS = 8
STICK_ROWS = 1024
STICK_SKIP_LOG2 = -150.0
NEG_BIG = -1e30
LOG2E = math.log2(math.e)
SUBLANES = 8
V_ROWS = HEAD_DIM + 2 * SUBLANES
VMEM_LIMIT = 56 * 1024 * 1024
ROW_TILE = 512
MOD_TILE_N = 1024
PROJ_TILE = (1024, 1024)
OUT_TILE = (256, 1024)

_BF = jnp.bfloat16
_F32 = jnp.float32


def _cparams(sem):
    return pltpu.CompilerParams(dimension_semantics=sem, vmem_limit_bytes=VMEM_LIMIT)


def _dot(a, b):
    return jnp.dot(a, b, preferred_element_type=_F32)


def _dot_nt(a, b):
    return lax.dot_general(a, b, (((1,), (1,)), ((), ())), preferred_element_type=_F32)


def _split_hi_lo(x):
    hi = x.astype(_BF)
    lo = (x - hi.astype(_F32)).astype(_BF)
    return hi, lo


def _silu(z):
    return z / (1.0 + jnp.exp(-z))


def _mod_kernel(c_ref, w_ref, b_ref, o_ref):
    a_hi, a_lo = _split_hi_lo(_silu(c_ref[...]))
    w_hi, w_lo = _split_hi_lo(w_ref[0])
    acc = _dot(a_hi, w_hi) + _dot(a_lo, w_hi) + _dot(a_hi, w_lo)
    o_ref[0] = acc + b_ref[0]


def _modulation(c, w_mod, b_mod):
    depth, d, n = w_mod.shape
    bsz = c.shape[0]
    rows = SUBLANES
    tn = MOD_TILE_N if n % MOD_TILE_N == 0 else n
    c_pad = jnp.zeros((rows, d), _F32).at[:bsz].set(c)
    out = pl.pallas_call(
        _mod_kernel,
        grid=(depth, n // tn),
        in_specs=[
            pl.BlockSpec((rows, d), lambda l, j: (0, 0)),
            pl.BlockSpec((1, d, tn), lambda l, j: (l, 0, j)),
            pl.BlockSpec((1, 1, tn), lambda l, j: (l, 0, j)),
        ],
        out_specs=pl.BlockSpec((1, rows, tn), lambda l, j: (l, 0, j)),
        out_shape=jax.ShapeDtypeStruct((depth, rows, n), _F32),
        compiler_params=_cparams(("arbitrary", "arbitrary")),
        name="modulation",
    )(c_pad, w_mod, b_mod.reshape(depth, 1, n))
    return out[:, :bsz]


def _norm_mod_kernel(h_ref, g_ref, scale_ref, shift_ref, o_ref):
    h = h_ref[...]
    inv = lax.rsqrt(jnp.mean(h * h, axis=-1, keepdims=True) + EPS)
    gain = g_ref[...] * (1.0 + scale_ref[0])
    o_ref[...] = (h * inv * gain + shift_ref[0]).astype(o_ref.dtype)


def _norm_mod(h, g, scale, shift, seq):
    m, d = h.shape
    tm = ROW_TILE
    per_b = seq // tm
    return pl.pallas_call(
        _norm_mod_kernel,
        grid=(m // tm,),
        in_specs=[
            pl.BlockSpec((tm, d), lambda i: (i, 0)),
            pl.BlockSpec((1, d), lambda i: (0, 0)),
            pl.BlockSpec((1, 1, d), lambda i: (i // per_b, 0, 0)),
            pl.BlockSpec((1, 1, d), lambda i: (i // per_b, 0, 0)),
        ],
        out_specs=pl.BlockSpec((tm, d), lambda i: (i, 0)),
        out_shape=jax.ShapeDtypeStruct((m, d), _BF),
        compiler_params=_cparams(("arbitrary",)),
        name="norm_mod",
    )(h, g.reshape(1, d), scale, shift)


def _final_norm_kernel(h_ref, g_ref, o_ref):
    h = h_ref[...]
    inv = lax.rsqrt(jnp.mean(h * h, axis=-1, keepdims=True) + EPS)
    o_ref[...] = h * inv * g_ref[...]


def _final_norm(h, g):
    m, d = h.shape
    tm = ROW_TILE
    return pl.pallas_call(
        _final_norm_kernel,
        grid=(m // tm,),
        in_specs=[pl.BlockSpec((tm, d), lambda i: (i, 0)), pl.BlockSpec((1, d), lambda i: (0, 0))],
        out_specs=pl.BlockSpec((tm, d), lambda i: (i, 0)),
        out_shape=jax.ShapeDtypeStruct((m, d), _F32),
        compiler_params=_cparams(("arbitrary",)),
        name="final_norm",
    )(h, g.reshape(1, d))


def _proj_kernel(u_ref, w_ref, o_ref, wbf_ref):
    @pl.when(pl.program_id(1) == 0)
    def _():
        wbf_ref[...] = w_ref[0].astype(_BF)

    o_ref[...] = _dot(u_ref[...], wbf_ref[...]).astype(o_ref.dtype)


def _in_proj(u, w_in, layer):
    m, d = u.shape
    p = w_in.shape[2]
    tm, tn = min(PROJ_TILE[0], m), min(PROJ_TILE[1], p)
    return pl.pallas_call(
        _proj_kernel,
        grid=(p // tn, m // tm),
        in_specs=[
            pl.BlockSpec((tm, d), lambda j, i: (i, 0)),
            pl.BlockSpec((1, d, tn), lambda j, i: (layer, 0, j)),
        ],
        out_specs=pl.BlockSpec((tm, tn), lambda j, i: (i, j)),
        out_shape=jax.ShapeDtypeStruct((m, p), _BF),
        scratch_shapes=[pltpu.VMEM((d, tn), _BF)],
        compiler_params=_cparams(("arbitrary", "arbitrary")),
        name="in_proj",
    )(u, w_in)


def _out_kernel(*refs):
    *a_refs, w_ref, h_ref, gate_ref, o_ref, wbf_ref = refs

    @pl.when(pl.program_id(1) == 0)
    def _():
        wbf_ref[...] = w_ref[0].astype(_BF)

    y, k0 = None, 0
    for a_ref in a_refs:
        k1 = k0 + a_ref.shape[1]
        part = _dot(a_ref[...], wbf_ref[k0:k1, :])
        y = part if y is None else y + part
        k0 = k1
    o_ref[...] = h_ref[...] + gate_ref[0] * y


def _out_proj(parts, w_out, layer, h, gate, seq):
    m = parts[0].shape[0]
    k, d = w_out.shape[1], w_out.shape[2]
    tm, tn = OUT_TILE[0], min(OUT_TILE[1], d)
    per_b = seq // tm
    return pl.pallas_call(
        _out_kernel,
        grid=(d // tn, m // tm),
        in_specs=[pl.BlockSpec((tm, p.shape[1]), lambda j, i: (i, 0)) for p in parts] + [
            pl.BlockSpec((1, k, tn), lambda j, i: (layer, 0, j)),
            pl.BlockSpec((tm, tn), lambda j, i: (i, j)),
            pl.BlockSpec((1, 1, tn), lambda j, i: (i // per_b, 0, j)),
        ],
        out_specs=pl.BlockSpec((tm, tn), lambda j, i: (i, j)),
        out_shape=jax.ShapeDtypeStruct((m, d), _F32),
        scratch_shapes=[pltpu.VMEM((k, tn), _BF)],
        compiler_params=_cparams(("arbitrary", "arbitrary")),
        name="out_proj",
    )(*parts, w_out, h, gate)


def _t5_bucket(dist):
    max_exact = N_BUCKETS // 2
    d = jnp.maximum(dist, 1).astype(_F32)
    large = max_exact + (jnp.log(d / max_exact) / math.log(MAX_DISTANCE / max_exact)
                         * (N_BUCKETS - max_exact)).astype(jnp.int32)
    large = jnp.minimum(large, N_BUCKETS - 1)
    return jnp.where(dist < max_exact, dist, large)


def _dilated_log_multiplicity(n):
    delta = jnp.arange(n)
    mult = jnp.zeros((n,), _F32)
    for window, dil in DILATED_PATTERNS:
        mult = mult + ((delta % dil == 0) & (delta <= window)).astype(_F32)
    return jnp.where(mult > 0, jnp.log(jnp.maximum(mult, 1.0)), NEG_BIG)


def _skewed_bias(bias_vec, n_off):
    t = SM_BLOCK
    h = bias_vec.shape[0]
    c = n_off * t
    width = c + t
    rr = jnp.take(bias_vec, jnp.clip(jnp.arange(width) - t, 0, c - 1), axis=1)
    return pl.pallas_call(
        _skew_kernel,
        grid=(h,),
        in_specs=[pl.BlockSpec((1, 1, width), lambda i: (i, 0, 0))],
        out_specs=pl.BlockSpec((1, t, width), lambda i: (i, 0, 0)),
        out_shape=jax.ShapeDtypeStruct((h, t, width), _F32),
        compiler_params=_cparams(("arbitrary",)),
        name="bias_skew",
    )(rr.reshape(h, 1, width))


def _skew_kernel(rr_ref, o_ref):
    _, t, width = o_ref.shape
    rows = SUBLANES
    base = pltpu.roll(jnp.broadcast_to(rr_ref[0], (rows, width)), 0, 1, stride=1, stride_axis=0)
    for g in range(t // rows):
        o_ref[0, g * rows:(g + 1) * rows, :] = pltpu.roll(base, g * rows, 1)


def _flash_attend(qts, k_refs, col0s, bias_ref, vt_ref, bufs, scale2, lo, qi, finish):
    t = SM_BLOCK
    raw_a, raw_b, p_a, p_b, acc_ref = bufs
    chains = range(len(qts))
    n = qts[0].shape[1]

    def scores_into(ki, s_ref):
        start = pl.multiple_of(ki * t, t)
        bias = _bias_tile(bias_ref, qi - ki)
        tops = []
        for c in chains:
            s = _dot(k_refs[c][pl.ds(start, t), :], qts[c]) * scale2 + bias[:, col0s[c]:col0s[c] + n]
            s_ref[c] = s
            tops.append(jnp.max(s, axis=0, keepdims=True))
        return tuple(tops)

    def softmax_from(s_ref, tops, causal, stats, p_ref):
        new_stats, alphas, ps = [], [], []
        for c in chains:
            m = stats[c]
            s = s_ref[c]
            top = tops[c]
            if causal:
                key = lax.broadcasted_iota(jnp.int32, s.shape, 0)
                qry = lax.broadcasted_iota(jnp.int32, s.shape, 1) + col0s[c]
                s = jnp.where(key <= qry, s, NEG_BIG)
                top = jnp.max(s, axis=0, keepdims=True)
            m_new = jnp.maximum(m, top)
            alphas.append(jnp.exp2(m - m_new))
            p = jnp.exp2(s - m_new)
            new_stats.append(m_new)
            if p_ref is None:
                ps.append(p.astype(_BF))
            else:
                p_ref[c] = p.astype(_BF)
        return tuple(new_stats), tuple(alphas), ps

    def add_weighted_values(ki, p_src, alphas):
        vt_blk = vt_ref[:, pl.ds(pl.multiple_of(ki * t, t), t)]
        for c in chains:
            acc_ref[c] = acc_ref[c] * alphas[c] + _dot(vt_blk, p_src[c])

    def half_step(ki, s_cur, s_next, p_cur, p_prev, tops, stats, alphas):
        add_weighted_values(jnp.maximum(ki - 1, lo), p_prev, alphas)
        tops_next = scores_into(ki + 1, s_next)
        stats, alphas, _ = softmax_from(s_cur, tops, False, stats, p_cur)
        return tops_next, stats, alphas

    def last_steps(s_ref, p_prev, tops, stats, alphas):
        add_weighted_values(jnp.maximum(qi - 1, lo), p_prev, alphas)
        stats, alphas, ps = softmax_from(s_ref, tops, True, stats, None)
        add_weighted_values(qi, ps, alphas)
        finish(tuple((acc_ref[c, HEAD_DIM:HEAD_DIM + 1, :], acc_ref[c, :HEAD_DIM, :]) for c in chains))

    for c in chains:
        acc_ref[c] = jnp.zeros(acc_ref.shape[1:], _F32)
        p_b[c] = jnp.zeros(p_b.shape[1:], _BF)
    tops = scores_into(lo, raw_a)
    stats = tuple(jnp.full((1, n), NEG_BIG, _F32) for _ in chains)
    alphas = tuple(jnp.ones((1, n), _F32) for _ in chains)
    n_full = qi - lo

    def pair(it, carry):
        k0 = lo + 2 * it
        carry = half_step(k0, raw_a, raw_b, p_a, p_b, *carry)
        return half_step(k0 + 1, raw_b, raw_a, p_b, p_a, *carry)

    tops, stats, alphas = lax.fori_loop(0, n_full // 2, pair, (tops, stats, alphas))

    @pl.when(n_full % 2 == 1)
    def _():
        last_steps(raw_b, p_a, *half_step(qi - 1, raw_a, raw_b, p_a, p_b, tops, stats, alphas))

    @pl.when(n_full % 2 == 0)
    def _():
        last_steps(raw_a, p_b, tops, stats, alphas)


def _bias_tile(bias_ref, off):
    t = SM_BLOCK
    return bias_ref[0, :, pl.ds(pl.multiple_of((off + 1) * t, t), t)]


def _fill_v_transposed(v_ref, vt_ref):
    t = SM_BLOCK
    pad = lax.broadcasted_iota(jnp.int32, (V_ROWS - HEAD_DIM, t), 0)
    for c in range(v_ref.shape[0] // t):
        vt_ref[:HEAD_DIM, c * t:(c + 1) * t] = v_ref[c * t:(c + 1) * t, :].astype(_F32).T.astype(_BF)
        vt_ref[HEAD_DIM:, c * t:(c + 1) * t] = jnp.where(pad == 0, 1.0, 0.0).astype(_BF)


def _dilated_kernel(q_ref, k_ref, v_ref, z_ref, bias_ref, o_ref, vt_ref, *bufs, n_off):
    t = SM_BLOCK
    half = t // 2
    scale2 = HEAD_DIM ** -0.5 * LOG2E
    _fill_v_transposed(v_ref, vt_ref)

    def query_block(qi, carry):
        rows = pl.ds(pl.multiple_of(qi * t, t), t)
        qt = q_ref[rows, :].astype(_F32).T.astype(_BF)

        def finish(outs):
            o = jnp.concatenate([acc / l for l, acc in outs], axis=1).T
            o_ref[rows, :] = (o * _silu(z_ref[rows, :].astype(_F32))).astype(o_ref.dtype)

        _flash_attend((qt[:, :half], qt[:, half:]), (k_ref, k_ref), (0, half), bias_ref, vt_ref, bufs, scale2,
                      jnp.maximum(qi - (n_off - 1), 0), qi, finish)
        return carry

    lax.fori_loop(0, q_ref.shape[0] // t, query_block, 0)


def _diff_kernel(q1_ref, q2_ref, k1_ref, k2_ref, v_ref, z_ref, bias_ref, lam_ref, g_ref, o_ref, vt_ref, *bufs,
                 lam_init):
    t = SM_BLOCK
    h = pl.program_id(1)
    scale2 = DIFF_QK_DIM ** -0.5 * LOG2E
    dim = lax.broadcasted_iota(jnp.int32, (HEAD_DIM, t), 0)
    mine = (dim // DIFF_QK_DIM) == (h % 2)
    _fill_v_transposed(v_ref, vt_ref)
    lv = lam_ref[...]
    lam = (jnp.exp(jnp.sum(lv[0:1] * lv[1:2], axis=1, keepdims=True))
           - jnp.exp(jnp.sum(lv[2:3] * lv[3:4], axis=1, keepdims=True)) + lam_init)

    def query_block(qi, carry):
        rows = pl.ds(pl.multiple_of(qi * t, t), t)
        q1t = jnp.where(mine, q1_ref[rows, :].astype(_F32).T, 0.0).astype(_BF)
        q2t = jnp.where(mine, q2_ref[rows, :].astype(_F32).T, 0.0).astype(_BF)

        def finish(outs):
            (l1, a1), (l2, a2) = outs
            o = a1 / l1 - lam * (a2 / l2)
            o = (o * lax.rsqrt(jnp.mean(o * o, axis=0, keepdims=True) + EPS)).T
            o = o * (g_ref[...] * (1.0 - lam_init))
            o_ref[rows, :] = (o * _silu(z_ref[rows, :].astype(_F32))).astype(o_ref.dtype)

        _flash_attend((q1t, q2t), (k1_ref, k2_ref), (0, 0), bias_ref, vt_ref, bufs, scale2, 0, qi, finish)
        return carry

    lax.fori_loop(0, q1_ref.shape[0] // t, query_block, 0)


def _stick_kernel(q_ref, k_ref, v_ref, z_ref, o_ref, acc_ref):
    t = STICK_BLOCK
    d = HEAD_DIM
    heads = range(q_ref.shape[1] // d)
    blocks_per_step = q_ref.shape[0] // t
    scale2 = d ** -0.5 * LOG2E
    row = lax.broadcasted_iota(jnp.int32, (t, t), 0)
    col = lax.broadcasted_iota(jnp.int32, (t, t), 1)
    later = jnp.where(row > col, 1.0, 0.0).astype(_BF)
    strict = col < row

    def query_block(j, carry):
        qi = pl.program_id(2) * blocks_per_step + j
        rows = pl.ds(pl.multiple_of(j * t, t), t)
        qs = [q_ref[rows, c * d:(c + 1) * d] for c in heads]

        def step(ki, tails, diagonal):
            start = pl.multiple_of(ki * t, t)
            zs = [_dot_nt(qs[c], k_ref[pl.ds(start, t), c * d:(c + 1) * d]) for c in heads]
            log_betas, parts = [], []
            for c in heads:
                z = zs[c] * scale2
                softplus = jnp.log(1.0 + jnp.exp2(-jnp.abs(z))) * LOG2E
                log_beta = jnp.minimum(z, 0.0) - softplus
                log_1m = log_beta - z
                if diagonal:
                    log_1m = jnp.where(strict, log_1m, 0.0)
                log_betas.append(log_beta)
                parts.append(_split_hi_lo(log_1m) + (log_1m[:, 0:1],))
            sums = [_dot(hi, later) + _dot(lo, later) for hi, lo, _ in parts]
            weights = []
            for c in heads:
                a = jnp.exp2(log_betas[c] + (tails[c] + sums[c]))
                if diagonal:
                    a = jnp.where(strict, a, 0.0)
                weights.append(a.astype(_BF))
            for c in heads:
                acc_ref[c] = acc_ref[c] + _dot(weights[c], v_ref[pl.ds(start, t), c * d:(c + 1) * d])
            return tuple(tails[c] + (sums[c][:, 0:1] + parts[c][2]) for c in heads)

        for c in heads:
            acc_ref[c] = jnp.zeros((t, d), _F32)
        tails = step(qi, tuple(jnp.zeros((t, 1), _F32) for _ in heads), True)

        def live(tails):
            worst = functools.reduce(jnp.maximum, tails)
            return (jnp.max(worst) > STICK_SKIP_LOG2).astype(jnp.int32)

        def cond(state):
            i, _, alive = state
            return jnp.logical_and(i < qi, alive > 0)

        def body(state):
            i, tails, _ = state
            tails = step(qi - 1 - i, tails, False)
            return i + 1, tails, live(tails)

        lax.while_loop(cond, body, (jnp.int32(0), tails, live(tails)))
        for c in heads:
            gate = _silu(z_ref[rows, c * d:(c + 1) * d].astype(_F32))
            o_ref[rows, c * d:(c + 1) * d] = (acc_ref[c] * gate).astype(o_ref.dtype)
        return carry

    lax.fori_loop(0, blocks_per_step, query_block, 0)


def _flash_buffers(chains, n):
    t = SM_BLOCK
    return [pltpu.VMEM((chains, t, n), _F32), pltpu.VMEM((chains, t, n), _F32),
            pltpu.VMEM((chains, t, n), _BF), pltpu.VMEM((chains, t, n), _BF),
            pltpu.VMEM((chains, V_ROWS, n), _F32)]


def _odd_mixer(proj, bsz, seq, d_inner):
    t = STICK_BLOCK
    w = STICK_HEADS * HEAD_DIM
    groups = d_inner // w
    rows = min(STICK_ROWS, seq)
    nr = seq // rows
    tile = lambda col0: pl.BlockSpec((rows, w), lambda b, g, r: (b * nr + r, col0 + g))
    whole = lambda col0: pl.BlockSpec((seq, w), lambda b, g, r: (b, col0 + g))
    return pl.pallas_call(
        _stick_kernel,
        grid=(bsz, groups, nr),
        in_specs=[tile(0), whole(groups), whole(2 * groups), tile(3 * groups)],
        out_specs=tile(0),
        out_shape=jax.ShapeDtypeStruct((bsz * seq, d_inner), _BF),
        scratch_shapes=[pltpu.VMEM((STICK_HEADS, t, HEAD_DIM), _F32)],
        compiler_params=_cparams(("arbitrary", "arbitrary", "arbitrary")),
        name="stick_breaking",
    )(proj, proj, proj, proj)


def _even_mixer(proj, bias_a, bias_b, lam_vec, subln_g, lam_init, bsz, seq, d_inner):
    d_a = d_inner // 2
    h_a = d_a // HEAD_DIM
    h_b = (d_inner - d_a) // HEAD_DIM
    t = SM_BLOCK
    n_off_a = bias_a.shape[2] // t - 1
    n_off_b = bias_b.shape[2] // t - 1
    out_shape = jax.ShapeDtypeStruct((bsz * seq, d_a), _BF)
    params = _cparams(("arbitrary", "arbitrary"))
    z0 = 3 * d_inner // HEAD_DIM
    col = lambda col0, per_block=1: pl.BlockSpec((seq, HEAD_DIM), lambda b, h: (b, col0 + h // per_block))
    bias_spec = lambda n_off: pl.BlockSpec((1, t, (n_off + 1) * t), lambda b, h: (h, 0, 0))

    o_a = pl.pallas_call(
        functools.partial(_dilated_kernel, n_off=n_off_a),
        grid=(bsz, h_a),
        in_specs=[col(0), col(h_a), col(2 * h_a), col(z0), bias_spec(n_off_a)],
        out_specs=col(0),
        out_shape=out_shape,
        scratch_shapes=[pltpu.VMEM((V_ROWS, seq), _BF)] + _flash_buffers(2, t // 2),
        compiler_params=params,
        name="dilated_attention",
    )(proj, proj, proj, proj, bias_a)

    c0 = 3 * h_a
    half = h_b // 2
    o_b = pl.pallas_call(
        functools.partial(_diff_kernel, lam_init=lam_init),
        grid=(bsz, h_b),
        in_specs=[col(c0, 2), col(c0 + half, 2), col(c0 + 2 * half, 2), col(c0 + 3 * half, 2),
                  col(c0 + 4 * half), col(z0 + h_a), bias_spec(n_off_b),
                  pl.BlockSpec((4, DIFF_QK_DIM), lambda b, h: (0, 0)),
                  pl.BlockSpec((1, HEAD_DIM), lambda b, h: (0, 0))],
        out_specs=col(0),
        out_shape=out_shape,
        scratch_shapes=[pltpu.VMEM((V_ROWS, seq), _BF)] + _flash_buffers(2, t),
        compiler_params=params,
        name="diff_attention",
    )(proj, proj, proj, proj, proj, proj, bias_b, lam_vec, subln_g.reshape(1, HEAD_DIM))
    return o_a, o_b


def kernel(x, c, norm_g, w_mod, b_mod, w_in, w_out, rel_bias, diff_lambda, diff_subln_g, final_norm_g):
    bsz, seq, d = x.shape
    depth = w_in.shape[0]
    d_inner = w_out.shape[1]
    h_a = d_inner // 2 // HEAD_DIM
    n_blk = seq // SM_BLOCK

    mod = _modulation(c, w_mod, b_mod)

    table = rel_bias[_t5_bucket(jnp.arange(seq))].T
    n_off_a = min(n_blk, DILATED_PATTERNS[-1][0] // SM_BLOCK + 1)
    vec_a = table[:h_a, : n_off_a * SM_BLOCK] + _dilated_log_multiplicity(n_off_a * SM_BLOCK)[None]
    bias_a = _skewed_bias(vec_a * LOG2E, n_off_a)
    bias_b = _skewed_bias(table[h_a:] * LOG2E, n_blk)

    h = x.reshape(bsz * seq, d)
    for layer in range(depth):
        shift = mod[layer, :, 0:d].reshape(bsz, 1, d)
        scale = mod[layer, :, d:2 * d].reshape(bsz, 1, d)
        gate = mod[layer, :, 2 * d:].reshape(bsz, 1, d)
        u = _norm_mod(h, norm_g[layer], scale, shift, seq)
        proj = _in_proj(u, w_in, layer)
        if layer % 2 == 0:
            e = layer // 2
            lam_init = 0.8 - 0.6 * math.exp(-0.3 * layer)
            mixed = _even_mixer(proj, bias_a, bias_b, diff_lambda[e], diff_subln_g[e], lam_init,
                                bsz, seq, d_inner)
        else:
            mixed = (_odd_mixer(proj, bsz, seq, d_inner),)
        h = _out_proj(mixed, w_out, layer, h, gate, seq)
    return _final_norm(h, final_norm_g).reshape(bsz, seq, d)
```

```python
import functools
import math

import jax
import jax.numpy as jnp
from jax import lax
from jax.experimental import pallas as pl
from jax.experimental.pallas import tpu as pltpu

HEAD_DIM = 128
DIFF_QK_DIM = 64
N_BUCKETS = 32
MAX_DISTANCE = 2048
DILATED_PATTERNS = ((128, 1), (512, 4), (2048, 16))
EPS = 1e-6
SM_BLOCK = 512
STICK_BLOCK = 256
STICK_HEADS = 8
STICK_ROWS = 1024
STICK_SKIP_LOG2 = -150.0
NEG_BIG = -1e30
LOG2E = math.log2(math.e)
SUBLANES = 8
V_ROWS = HEAD_DIM + 2 * SUBLANES
VMEM_LIMIT = 56 * 1024 * 1024
ROW_TILE = 512
MOD_TILE_N = 1024
PROJ_TILE = (1024, 1024)
OUT_TILE = (256, 1024)

_BF = jnp.bfloat16
_F32 = jnp.float32


def _cparams(sem):
    return pltpu.CompilerParams(dimension_semantics=sem, vmem_limit_bytes=VMEM_LIMIT)


def _dot(a, b):
    return jnp.dot(a, b, preferred_element_type=_F32)


def _dot_nt(a, b):
    return lax.dot_general(a, b, (((1,), (1,)), ((), ())), preferred_element_type=_F32)


def _split_hi_lo(x):
    hi = x.astype(_BF)
    lo = (x - hi.astype(_F32)).astype(_BF)
    return hi, lo


def _silu(z):
    return z / (1.0 + jnp.exp(-z))


def _mod_kernel(c_ref, w_ref, b_ref, o_ref):
    a_hi, a_lo = _split_hi_lo(_silu(c_ref[...]))
    w_hi, w_lo = _split_hi_lo(w_ref[0])
    acc = _dot(a_hi, w_hi) + _dot(a_lo, w_hi) + _dot(a_hi, w_lo)
    o_ref[0] = acc + b_ref[0]


def _modulation(c, w_mod, b_mod):
    depth, d, n = w_mod.shape
    bsz = c.shape[0]
    rows = SUBLANES
    tn = MOD_TILE_N if n % MOD_TILE_N == 0 else n
    c_pad = jnp.zeros((rows, d), _F32).at[:bsz].set(c)
    out = pl.pallas_call(
        _mod_kernel,
        grid=(depth, n // tn),
        in_specs=[
            pl.BlockSpec((rows, d), lambda l, j: (0, 0)),
            pl.BlockSpec((1, d, tn), lambda l, j: (l, 0, j)),
            pl.BlockSpec((1, 1, tn), lambda l, j: (l, 0, j)),
        ],
        out_specs=pl.BlockSpec((1, rows, tn), lambda l, j: (l, 0, j)),
        out_shape=jax.ShapeDtypeStruct((depth, rows, n), _F32),
        compiler_params=_cparams(("arbitrary", "arbitrary")),
        name="modulation",
    )(c_pad, w_mod, b_mod.reshape(depth, 1, n))
    return out[:, :bsz]


def _norm_mod_kernel(h_ref, g_ref, scale_ref, shift_ref, o_ref):
    h = h_ref[...]
    inv = lax.rsqrt(jnp.mean(h * h, axis=-1, keepdims=True) + EPS)
    gain = g_ref[...] * (1.0 + scale_ref[0])
    o_ref[...] = (h * inv * gain + shift_ref[0]).astype(o_ref.dtype)


def _norm_mod(h, g, scale, shift, seq):
    m, d = h.shape
    tm = ROW_TILE
    per_b = seq // tm
    return pl.pallas_call(
        _norm_mod_kernel,
        grid=(m // tm,),
        in_specs=[
            pl.BlockSpec((tm, d), lambda i: (i, 0)),
            pl.BlockSpec((1, d), lambda i: (0, 0)),
            pl.BlockSpec((1, 1, d), lambda i: (i // per_b, 0, 0)),
            pl.BlockSpec((1, 1, d), lambda i: (i // per_b, 0, 0)),
        ],
        out_specs=pl.BlockSpec((tm, d), lambda i: (i, 0)),
        out_shape=jax.ShapeDtypeStruct((m, d), _BF),
        compiler_params=_cparams(("arbitrary",)),
        name="norm_mod",
    )(h, g.reshape(1, d), scale, shift)


def _final_norm_kernel(h_ref, g_ref, o_ref):
    h = h_ref[...]
    inv = lax.rsqrt(jnp.mean(h * h, axis=-1, keepdims=True) + EPS)
    o_ref[...] = h * inv * g_ref[...]


def _final_norm(h, g):
    m, d = h.shape
    tm = ROW_TILE
    return pl.pallas_call(
        _final_norm_kernel,
        grid=(m // tm,),
        in_specs=[pl.BlockSpec((tm, d), lambda i: (i, 0)), pl.BlockSpec((1, d), lambda i: (0, 0))],
        out_specs=pl.BlockSpec((tm, d), lambda i: (i, 0)),
        out_shape=jax.ShapeDtypeStruct((m, d), _F32),
        compiler_params=_cparams(("arbitrary",)),
        name="final_norm",
    )(h, g.reshape(1, d))


def _proj_kernel(u_ref, w_ref, o_ref, wbf_ref):
    @pl.when(pl.program_id(1) == 0)
    def _():
        wbf_ref[...] = w_ref[0].astype(_BF)

    o_ref[...] = _dot(u_ref[...], wbf_ref[...]).astype(o_ref.dtype)


def _in_proj(u, w_in, layer):
    m, d = u.shape
    p = w_in.shape[2]
    tm, tn = min(PROJ_TILE[0], m), min(PROJ_TILE[1], p)
    return pl.pallas_call(
        _proj_kernel,
        grid=(p // tn, m // tm),
        in_specs=[
            pl.BlockSpec((tm, d), lambda j, i: (i, 0)),
            pl.BlockSpec((1, d, tn), lambda j, i: (layer, 0, j)),
        ],
        out_specs=pl.BlockSpec((tm, tn), lambda j, i: (i, j)),
        out_shape=jax.ShapeDtypeStruct((m, p), _BF),
        scratch_shapes=[pltpu.VMEM((d, tn), _BF)],
        compiler_params=_cparams(("arbitrary", "arbitrary")),
        name="in_proj",
    )(u, w_in)


def _out_kernel(*refs):
    *a_refs, w_ref, h_ref, gate_ref, o_ref, wbf_ref = refs

    @pl.when(pl.program_id(1) == 0)
    def _():
        wbf_ref[...] = w_ref[0].astype(_BF)

    y, k0 = None, 0
    for a_ref in a_refs:
        k1 = k0 + a_ref.shape[1]
        part = _dot(a_ref[...], wbf_ref[k0:k1, :])
        y = part if y is None else y + part
        k0 = k1
    o_ref[...] = h_ref[...] + gate_ref[0] * y


def _out_proj(parts, w_out, layer, h, gate, seq):
    m = parts[0].shape[0]
    k, d = w_out.shape[1], w_out.shape[2]
    tm, tn = OUT_TILE[0], min(OUT_TILE[1], d)
    per_b = seq // tm
    return pl.pallas_call(
        _out_kernel,
        grid=(d // tn, m // tm),
        in_specs=[pl.BlockSpec((tm, p.shape[1]), lambda j, i: (i, 0)) for p in parts] + [
            pl.BlockSpec((1, k, tn), lambda j, i: (layer, 0, j)),
            pl.BlockSpec((tm, tn), lambda j, i: (i, j)),
            pl.BlockSpec((1, 1, tn), lambda j, i: (i // per_b, 0, j)),
        ],
        out_specs=pl.BlockSpec((tm, tn), lambda j, i: (i, j)),
        out_shape=jax.ShapeDtypeStruct((m, d), _F32),
        scratch_shapes=[pltpu.VMEM((k, tn), _BF)],
        compiler_params=_cparams(("arbitrary", "arbitrary")),
        name="out_proj",
    )(*parts, w_out, h, gate)


def _t5_bucket(dist):
    max_exact = N_BUCKETS // 2
    d = jnp.maximum(dist, 1).astype(_F32)
    large = max_exact + (jnp.log(d / max_exact) / math.log(MAX_DISTANCE / max_exact)
                         * (N_BUCKETS - max_exact)).astype(jnp.int32)
    large = jnp.minimum(large, N_BUCKETS - 1)
    return jnp.where(dist < max_exact, dist, large)


def _dilated_log_multiplicity(n):
    delta = jnp.arange(n)
    mult = jnp.zeros((n,), _F32)
    for window, dil in DILATED_PATTERNS:
        mult = mult + ((delta % dil == 0) & (delta <= window)).astype(_F32)
    return jnp.where(mult > 0, jnp.log(jnp.maximum(mult, 1.0)), NEG_BIG)


def _skewed_bias(bias_vec, n_off):
    t = SM_BLOCK
    h = bias_vec.shape[0]
    c = n_off * t
    width = c + t
    rr = jnp.take(bias_vec, jnp.clip(jnp.arange(width) - t, 0, c - 1), axis=1)
    return pl.pallas_call(
        _skew_kernel,
        grid=(h,),
        in_specs=[pl.BlockSpec((1, 1, width), lambda i: (i, 0, 0))],
        out_specs=pl.BlockSpec((1, t, width), lambda i: (i, 0, 0)),
        out_shape=jax.ShapeDtypeStruct((h, t, width), _F32),
        compiler_params=_cparams(("arbitrary",)),
        name="bias_skew",
    )(rr.reshape(h, 1, width))


def _skew_kernel(rr_ref, o_ref):
    _, t, width = o_ref.shape
    rows = SUBLANES
    base = pltpu.roll(jnp.broadcast_to(rr_ref[0], (rows, width)), 0, 1, stride=1, stride_axis=0)
    for g in range(t // rows):
        o_ref[0, g * rows:(g + 1) * rows, :] = pltpu.roll(base, g * rows, 1)


def _flash_attend(qts, k_refs, col0s, bias_ref, vt_ref, bufs, scale2, lo, qi, finish, scores_first=False):
    t = SM_BLOCK
    raw_a, raw_b, p_a, p_b, acc_ref = bufs
    chains = range(len(qts))
    n = qts[0].shape[1]

    def scores_into(ki, s_ref):
        start = pl.multiple_of(ki * t, t)
        bias = _bias_tile(bias_ref, qi - ki)
        tops = []
        for c in chains:
            s = _dot(k_refs[c][pl.ds(start, t), :], qts[c]) * scale2 + bias[:, col0s[c]:col0s[c] + n]
            s_ref[c] = s
            tops.append(jnp.max(s, axis=0, keepdims=True))
        return tuple(tops)

    def softmax_from(s_ref, tops, causal, stats, p_ref):
        new_stats, alphas, ps = [], [], []
        for c in chains:
            m = stats[c]
            s = s_ref[c]
            top = tops[c]
            if causal:
                key = lax.broadcasted_iota(jnp.int32, s.shape, 0)
                qry = lax.broadcasted_iota(jnp.int32, s.shape, 1) + col0s[c]
                s = jnp.where(key <= qry, s, NEG_BIG)
                top = jnp.max(s, axis=0, keepdims=True)
            m_new = jnp.maximum(m, top)
            alphas.append(jnp.exp2(m - m_new))
            p = jnp.exp2(s - m_new)
            new_stats.append(m_new)
            if p_ref is None:
                ps.append(p.astype(_BF))
            else:
                p_ref[c] = p.astype(_BF)
        return tuple(new_stats), tuple(alphas), ps

    def add_weighted_values(ki, p_src, alphas):
        vt_blk = vt_ref[:, pl.ds(pl.multiple_of(ki * t, t), t)]
        for c in chains:
            acc_ref[c] = acc_ref[c] * alphas[c] + _dot(vt_blk, p_src[c])

    def half_step(ki, s_cur, s_next, p_cur, p_prev, tops, stats, alphas):
        if scores_first:
            tops_next = scores_into(ki + 1, s_next)
        add_weighted_values(jnp.maximum(ki - 1, lo), p_prev, alphas)
        if not scores_first:
            tops_next = scores_into(ki + 1, s_next)
        stats, alphas, _ = softmax_from(s_cur, tops, False, stats, p_cur)
        return tops_next, stats, alphas

    def last_steps(s_ref, p_prev, tops, stats, alphas):
        add_weighted_values(jnp.maximum(qi - 1, lo), p_prev, alphas)
        stats, alphas, ps = softmax_from(s_ref, tops, True, stats, None)
        add_weighted_values(qi, ps, alphas)
        finish(tuple((acc_ref[c, HEAD_DIM:HEAD_DIM + 1, :], acc_ref[c, :HEAD_DIM, :]) for c in chains))

    for c in chains:
        acc_ref[c] = jnp.zeros(acc_ref.shape[1:], _F32)
        p_b[c] = jnp.zeros(p_b.shape[1:], _BF)
    tops = scores_into(lo, raw_a)
    stats = tuple(jnp.full((1, n), NEG_BIG, _F32) for _ in chains)
    alphas = tuple(jnp.ones((1, n), _F32) for _ in chains)
    n_full = qi - lo

    def pair(it, carry):
        k0 = lo + 2 * it
        carry = half_step(k0, raw_a, raw_b, p_a, p_b, *carry)
        return half_step(k0 + 1, raw_b, raw_a, p_b, p_a, *carry)

    tops, stats, alphas = lax.fori_loop(0, n_full // 2, pair, (tops, stats, alphas))

    @pl.when(n_full % 2 == 1)
    def _():
        last_steps(raw_b, p_a, *half_step(qi - 1, raw_a, raw_b, p_a, p_b, tops, stats, alphas))

    @pl.when(n_full % 2 == 0)
    def _():
        last_steps(raw_a, p_b, tops, stats, alphas)


def _bias_tile(bias_ref, off):
    t = SM_BLOCK
    return bias_ref[0, :, pl.ds(pl.multiple_of((off + 1) * t, t), t)]


def _fill_v_transposed(v_ref, vt_ref):
    t = SM_BLOCK
    pad = lax.broadcasted_iota(jnp.int32, (V_ROWS - HEAD_DIM, t), 0)
    for c in range(v_ref.shape[0] // t):
        vt_ref[:HEAD_DIM, c * t:(c + 1) * t] = v_ref[c * t:(c + 1) * t, :].astype(_F32).T.astype(_BF)
        vt_ref[HEAD_DIM:, c * t:(c + 1) * t] = jnp.where(pad == 0, 1.0, 0.0).astype(_BF)


def _dilated_kernel(q_ref, k_ref, v_ref, z_ref, bias_ref, o_ref, vt_ref, *bufs, n_off):
    t = SM_BLOCK
    half = t // 2
    scale2 = HEAD_DIM ** -0.5 * LOG2E
    _fill_v_transposed(v_ref, vt_ref)

    def query_block(qi, carry):
        rows = pl.ds(pl.multiple_of(qi * t, t), t)
        qt = q_ref[rows, :].astype(_F32).T.astype(_BF)

        def finish(outs):
            o = jnp.concatenate([acc / l for l, acc in outs], axis=1).T
            o_ref[rows, :] = (o * _silu(z_ref[rows, :].astype(_F32))).astype(o_ref.dtype)

        _flash_attend((qt[:, :half], qt[:, half:]), (k_ref, k_ref), (0, half), bias_ref, vt_ref, bufs, scale2,
                      jnp.maximum(qi - (n_off - 1), 0), qi, finish)
        return carry

    lax.fori_loop(0, q_ref.shape[0] // t, query_block, 0)


def _diff_kernel(q1_ref, q2_ref, k1_ref, k2_ref, v_ref, z_ref, bias_ref, lam_ref, g_ref, o_ref, vt_ref, *bufs,
                 lam_init):
    t = SM_BLOCK
    h = pl.program_id(1)
    scale2 = DIFF_QK_DIM ** -0.5 * LOG2E
    dim = lax.broadcasted_iota(jnp.int32, (HEAD_DIM, t), 0)
    mine = (dim // DIFF_QK_DIM) == (h % 2)
    _fill_v_transposed(v_ref, vt_ref)
    lv = lam_ref[...]
    lam = (jnp.exp(jnp.sum(lv[0:1] * lv[1:2], axis=1, keepdims=True))
           - jnp.exp(jnp.sum(lv[2:3] * lv[3:4], axis=1, keepdims=True)) + lam_init)

    def query_block(qi, carry):
        rows = pl.ds(pl.multiple_of(qi * t, t), t)
        q1t = jnp.where(mine, q1_ref[rows, :].astype(_F32).T, 0.0).astype(_BF)
        q2t = jnp.where(mine, q2_ref[rows, :].astype(_F32).T, 0.0).astype(_BF)

        def finish(outs):
            (l1, a1), (l2, a2) = outs
            o = a1 / l1 - lam * (a2 / l2)
            o = (o * lax.rsqrt(jnp.mean(o * o, axis=0, keepdims=True) + EPS)).T
            o = o * (g_ref[...] * (1.0 - lam_init))
            o_ref[rows, :] = (o * _silu(z_ref[rows, :].astype(_F32))).astype(o_ref.dtype)

        _flash_attend((q1t, q2t), (k1_ref, k2_ref), (0, 0), bias_ref, vt_ref, bufs, scale2, 0, qi, finish,
                      scores_first=True)
        return carry

    lax.fori_loop(0, q1_ref.shape[0] // t, query_block, 0)


def _stick_kernel(q_ref, k_ref, v_ref, z_ref, o_ref, acc_ref):
    t = STICK_BLOCK
    d = HEAD_DIM
    heads = range(q_ref.shape[1] // d)
    blocks_per_step = q_ref.shape[0] // t
    scale2 = d ** -0.5 * LOG2E
    row = lax.broadcasted_iota(jnp.int32, (t, t), 0)
    col = lax.broadcasted_iota(jnp.int32, (t, t), 1)
    later = jnp.where(row > col, 1.0, 0.0).astype(_BF)
    strict = col < row

    def query_block(j, carry):
        qi = pl.program_id(2) * blocks_per_step + j
        rows = pl.ds(pl.multiple_of(j * t, t), t)
        qs = [q_ref[rows, c * d:(c + 1) * d] for c in heads]

        def step(ki, tails, diagonal):
            start = pl.multiple_of(ki * t, t)
            zs = [_dot_nt(qs[c], k_ref[pl.ds(start, t), c * d:(c + 1) * d]) for c in heads]
            log_betas, parts = [], []
            for c in heads:
                z = zs[c] * scale2
                softplus = jnp.log(1.0 + jnp.exp2(-jnp.abs(z))) * LOG2E
                log_beta = jnp.minimum(z, 0.0) - softplus
                log_1m = log_beta - z
                if diagonal:
                    log_1m = jnp.where(strict, log_1m, 0.0)
                log_betas.append(log_beta)
                parts.append(_split_hi_lo(log_1m) + (log_1m[:, 0:1],))
            sums = [_dot(hi, later) + _dot(lo, later) for hi, lo, _ in parts]
            weights = []
            for c in heads:
                a = jnp.exp2(log_betas[c] + (tails[c] + sums[c]))
                if diagonal:
                    a = jnp.where(strict, a, 0.0)
                weights.append(a.astype(_BF))
            for c in heads:
                acc_ref[c] = acc_ref[c] + _dot(weights[c], v_ref[pl.ds(start, t), c * d:(c + 1) * d])
            return tuple(tails[c] + (sums[c][:, 0:1] + parts[c][2]) for c in heads)

        for c in heads:
            acc_ref[c] = jnp.zeros((t, d), _F32)
        tails = step(qi, tuple(jnp.zeros((t, 1), _F32) for _ in heads), True)

        def live(tails):
            worst = functools.reduce(jnp.maximum, tails)
            return (jnp.max(worst) > STICK_SKIP_LOG2).astype(jnp.int32)

        def cond(state):
            i, _, alive = state
            return jnp.logical_and(i < qi, alive > 0)

        def body(state):
            i, tails, _ = state
            tails = step(qi - 1 - i, tails, False)
            return i + 1, tails, live(tails)

        lax.while_loop(cond, body, (jnp.int32(0), tails, live(tails)))
        for c in heads:
            gate = _silu(z_ref[rows, c * d:(c + 1) * d].astype(_F32))
            o_ref[rows, c * d:(c + 1) * d] = (acc_ref[c] * gate).astype(o_ref.dtype)
        return carry

    lax.fori_loop(0, blocks_per_step, query_block, 0)


def _flash_buffers(chains, n):
    t = SM_BLOCK
    return [pltpu.VMEM((chains, t, n), _F32), pltpu.VMEM((chains, t, n), _F32),
            pltpu.VMEM((chains, t, n), _BF), pltpu.VMEM((chains, t, n), _BF),
            pltpu.VMEM((chains, V_ROWS, n), _F32)]


def _odd_mixer(proj, bsz, seq, d_inner):
    t = STICK_BLOCK
    w = STICK_HEADS * HEAD_DIM
    groups = d_inner // w
    rows = min(STICK_ROWS, seq)
    nr = seq // rows
    tile = lambda col0: pl.BlockSpec((rows, w), lambda b, g, r: (b * nr + r, col0 + g))
    whole = lambda col0: pl.BlockSpec((seq, w), lambda b, g, r: (b, col0 + g))
    return pl.pallas_call(
        _stick_kernel,
        grid=(bsz, groups, nr),
        in_specs=[tile(0), whole(groups), whole(2 * groups), tile(3 * groups)],
        out_specs=tile(0),
        out_shape=jax.ShapeDtypeStruct((bsz * seq, d_inner), _BF),
        scratch_shapes=[pltpu.VMEM((STICK_HEADS, t, HEAD_DIM), _F32)],
        compiler_params=_cparams(("arbitrary", "arbitrary", "arbitrary")),
        name="stick_breaking",
    )(proj, proj, proj, proj)


def _even_mixer(proj, bias_a, bias_b, lam_vec, subln_g, lam_init, bsz, seq, d_inner):
    d_a = d_inner // 2
    h_a = d_a // HEAD_DIM
    h_b = (d_inner - d_a) // HEAD_DIM
    t = SM_BLOCK
    n_off_a = bias_a.shape[2] // t - 1
    n_off_b = bias_b.shape[2] // t - 1
    out_shape = jax.ShapeDtypeStruct((bsz * seq, d_a), _BF)
    params = _cparams(("arbitrary", "arbitrary"))
    z0 = 3 * d_inner // HEAD_DIM
    col = lambda col0, per_block=1: pl.BlockSpec((seq, HEAD_DIM), lambda b, h: (b, col0 + h // per_block))
    bias_spec = lambda n_off: pl.BlockSpec((1, t, (n_off + 1) * t), lambda b, h: (h, 0, 0))

    o_a = pl.pallas_call(
        functools.partial(_dilated_kernel, n_off=n_off_a),
        grid=(bsz, h_a),
        in_specs=[col(0), col(h_a), col(2 * h_a), col(z0), bias_spec(n_off_a)],
        out_specs=col(0),
        out_shape=out_shape,
        scratch_shapes=[pltpu.VMEM((V_ROWS, seq), _BF)] + _flash_buffers(2, t // 2),
        compiler_params=params,
        name="dilated_attention",
    )(proj, proj, proj, proj, bias_a)

    c0 = 3 * h_a
    half = h_b // 2
    o_b = pl.pallas_call(
        functools.partial(_diff_kernel, lam_init=lam_init),
        grid=(bsz, h_b),
        in_specs=[col(c0, 2), col(c0 + half, 2), col(c0 + 2 * half, 2), col(c0 + 3 * half, 2),
                  col(c0 + 4 * half), col(z0 + h_a), bias_spec(n_off_b),
                  pl.BlockSpec((4, DIFF_QK_DIM), lambda b, h: (0, 0)),
                  pl.BlockSpec((1, HEAD_DIM), lambda b, h: (0, 0))],
        out_specs=col(0),
        out_shape=out_shape,
        scratch_shapes=[pltpu.VMEM((V_ROWS, seq), _BF)] + _flash_buffers(2, t),
        compiler_params=params,
        name="diff_attention",
    )(proj, proj, proj, proj, proj, proj, bias_b, lam_vec, subln_g.reshape(1, HEAD_DIM))
    return o_a, o_b


def kernel(x, c, norm_g, w_mod, b_mod, w_in, w_out, rel_bias, diff_lambda, diff_subln_g, final_norm_g):
    bsz, seq, d = x.shape
    depth = w_in.shape[0]
    d_inner = w_out.shape[1]
    h_a = d_inner // 2 // HEAD_DIM
    n_blk = seq // SM_BLOCK

    mod = _modulation(c, w_mod, b_mod)

    table = rel_bias[_t5_bucket(jnp.arange(seq))].T
    n_off_a = min(n_blk, DILATED_PATTERNS[-1][0] // SM_BLOCK + 1)
    vec_a = table[:h_a, : n_off_a * SM_BLOCK] + _dilated_log_multiplicity(n_off_a * SM_BLOCK)[None]
    bias_a = _skewed_bias(vec_a * LOG2E, n_off_a)
    bias_b = _skewed_bias(table[h_a:] * LOG2E, n_blk)

    h = x.reshape(bsz * seq, d)
    for layer in range(depth):
        shift = mod[layer, :, 0:d].reshape(bsz, 1, d)
        scale = mod[layer, :, d:2 * d].reshape(bsz, 1, d)
        gate = mod[layer, :, 2 * d:].reshape(bsz, 1, d)
        u = _norm_mod(h, norm_g[layer], scale, shift, seq)
        proj = _in_proj(u, w_in, layer)
        if layer % 2 == 0:
            e = layer // 2
            lam_init = 0.8 - 0.6 * math.exp(-0.3 * layer)
            mixed = _even_mixer(proj, bias_a, bias_b, diff_lambda[e], diff_subln_g[e], lam_init,
                                bsz, seq, d_inner)
        else:
            mixed = (_odd_mixer(proj, bsz, seq, d_inner),)
        h = _out_proj(mixed, w_out, layer, h, gate, seq)
    return _final_norm(h, final_norm_g).reshape(bsz, seq, d)
```

```python
import functools
import math

import jax
import jax.numpy as jnp
from jax import lax
from jax.experimental import pallas as pl
from jax.experimental.pallas import tpu as pltpu

HEAD_DIM = 128
DIFF_QK_DIM = 64
N_BUCKETS = 32
MAX_DISTANCE = 2048
DILATED_PATTERNS = ((128, 1), (512, 4), (2048, 16))
EPS = 1e-6
SM_BLOCK = 512
STICK_BLOCK = 256
STICK_HEADS = 8
STICK_ROWS = 1024
STICK_SKIP_LOG2 = -150.0
NEG_BIG = -1e30
LOG2E = math.log2(math.e)
SUBLANES = 8
V_ROWS = HEAD_DIM + 2 * SUBLANES
VMEM_LIMIT = 56 * 1024 * 1024
ROW_TILE = 1024
MOD_TILE_N = 1024
PROJ_TILE = (1024, 1024)
OUT_TILE = (256, 1024)

_BF = jnp.bfloat16
_F32 = jnp.float32


def _cparams(sem):
    return pltpu.CompilerParams(dimension_semantics=sem, vmem_limit_bytes=VMEM_LIMIT)


def _dot(a, b):
    return jnp.dot(a, b, preferred_element_type=_F32)


def _dot_nt(a, b):
    return lax.dot_general(a, b, (((1,), (1,)), ((), ())), preferred_element_type=_F32)


def _split_hi_lo(x):
    hi = x.astype(_BF)
    lo = (x - hi.astype(_F32)).astype(_BF)
    return hi, lo


def _silu(z):
    return z / (1.0 + jnp.exp(-z))


def _mod_kernel(c_ref, w_ref, b_ref, o_ref):
    a_hi, a_lo = _split_hi_lo(_silu(c_ref[...]))
    w_hi, w_lo = _split_hi_lo(w_ref[0])
    acc = _dot(a_hi, w_hi) + _dot(a_lo, w_hi) + _dot(a_hi, w_lo)
    o_ref[0] = acc + b_ref[0]


def _modulation(c, w_mod, b_mod):
    depth, d, n = w_mod.shape
    bsz = c.shape[0]
    rows = SUBLANES
    tn = MOD_TILE_N if n % MOD_TILE_N == 0 else n
    c_pad = jnp.zeros((rows, d), _F32).at[:bsz].set(c)
    out = pl.pallas_call(
        _mod_kernel,
        grid=(depth, n // tn),
        in_specs=[
            pl.BlockSpec((rows, d), lambda l, j: (0, 0)),
            pl.BlockSpec((1, d, tn), lambda l, j: (l, 0, j)),
            pl.BlockSpec((1, 1, tn), lambda l, j: (l, 0, j)),
        ],
        out_specs=pl.BlockSpec((1, rows, tn), lambda l, j: (l, 0, j)),
        out_shape=jax.ShapeDtypeStruct((depth, rows, n), _F32),
        compiler_params=_cparams(("arbitrary", "arbitrary")),
        name="modulation",
    )(c_pad, w_mod, b_mod.reshape(depth, 1, n))
    return out[:, :bsz]


def _norm_mod_kernel(h_ref, g_ref, scale_ref, shift_ref, o_ref):
    h = h_ref[...]
    inv = lax.rsqrt(jnp.mean(h * h, axis=-1, keepdims=True) + EPS)
    gain = g_ref[...] * (1.0 + scale_ref[0])
    o_ref[...] = (h * inv * gain + shift_ref[0]).astype(o_ref.dtype)


def _norm_mod(h, g, scale, shift, seq):
    m, d = h.shape
    tm = ROW_TILE
    per_b = seq // tm
    return pl.pallas_call(
        _norm_mod_kernel,
        grid=(m // tm,),
        in_specs=[
            pl.BlockSpec((tm, d), lambda i: (i, 0)),
            pl.BlockSpec((1, d), lambda i: (0, 0)),
            pl.BlockSpec((1, 1, d), lambda i: (i // per_b, 0, 0)),
            pl.BlockSpec((1, 1, d), lambda i: (i // per_b, 0, 0)),
        ],
        out_specs=pl.BlockSpec((tm, d), lambda i: (i, 0)),
        out_shape=jax.ShapeDtypeStruct((m, d), _BF),
        compiler_params=_cparams(("arbitrary",)),
        name="norm_mod",
    )(h, g.reshape(1, d), scale, shift)


def _final_norm_kernel(h_ref, g_ref, o_ref):
    h = h_ref[...]
    inv = lax.rsqrt(jnp.mean(h * h, axis=-1, keepdims=True) + EPS)
    o_ref[...] = h * inv * g_ref[...]


def _final_norm(h, g):
    m, d = h.shape
    tm = ROW_TILE
    return pl.pallas_call(
        _final_norm_kernel,
        grid=(m // tm,),
        in_specs=[pl.BlockSpec((tm, d), lambda i: (i, 0)), pl.BlockSpec((1, d), lambda i: (0, 0))],
        out_specs=pl.BlockSpec((tm, d), lambda i: (i, 0)),
        out_shape=jax.ShapeDtypeStruct((m, d), _F32),
        compiler_params=_cparams(("arbitrary",)),
        name="final_norm",
    )(h, g.reshape(1, d))


def _proj_kernel(u_ref, w_ref, o_ref, wbf_ref):
    @pl.when(pl.program_id(1) == 0)
    def _():
        wbf_ref[...] = w_ref[0].astype(_BF)

    o_ref[...] = _dot(u_ref[...], wbf_ref[...]).astype(o_ref.dtype)


def _in_proj(u, w_in, layer):
    m, d = u.shape
    p = w_in.shape[2]
    tm, tn = min(PROJ_TILE[0], m), min(PROJ_TILE[1], p)
    return pl.pallas_call(
        _proj_kernel,
        grid=(p // tn, m // tm),
        in_specs=[
            pl.BlockSpec((tm, d), lambda j, i: (i, 0)),
            pl.BlockSpec((1, d, tn), lambda j, i: (layer, 0, j)),
        ],
        out_specs=pl.BlockSpec((tm, tn), lambda j, i: (i, j)),
        out_shape=jax.ShapeDtypeStruct((m, p), _BF),
        scratch_shapes=[pltpu.VMEM((d, tn), _BF)],
        compiler_params=_cparams(("arbitrary", "arbitrary")),
        name="in_proj",
    )(u, w_in)


def _out_kernel(*refs):
    *a_refs, w_ref, h_ref, gate_ref, o_ref, wbf_ref = refs

    @pl.when(pl.program_id(1) == 0)
    def _():
        wbf_ref[...] = w_ref[0].astype(_BF)

    y, k0 = None, 0
    for a_ref in a_refs:
        k1 = k0 + a_ref.shape[1]
        part = _dot(a_ref[...], wbf_ref[k0:k1, :])
        y = part if y is None else y + part
        k0 = k1
    o_ref[...] = h_ref[...] + gate_ref[0] * y


def _out_proj(parts, w_out, layer, h, gate, seq):
    m = parts[0].shape[0]
    k, d = w_out.shape[1], w_out.shape[2]
    tm, tn = OUT_TILE[0], min(OUT_TILE[1], d)
    per_b = seq // tm
    return pl.pallas_call(
        _out_kernel,
        grid=(d // tn, m // tm),
        in_specs=[pl.BlockSpec((tm, p.shape[1]), lambda j, i: (i, 0)) for p in parts] + [
            pl.BlockSpec((1, k, tn), lambda j, i: (layer, 0, j)),
            pl.BlockSpec((tm, tn), lambda j, i: (i, j)),
            pl.BlockSpec((1, 1, tn), lambda j, i: (i // per_b, 0, j)),
        ],
        out_specs=pl.BlockSpec((tm, tn), lambda j, i: (i, j)),
        out_shape=jax.ShapeDtypeStruct((m, d), _F32),
        scratch_shapes=[pltpu.VMEM((k, tn), _BF)],
        compiler_params=_cparams(("arbitrary", "arbitrary")),
        name="out_proj",
    )(*parts, w_out, h, gate)


def _t5_bucket(dist):
    max_exact = N_BUCKETS // 2
    d = jnp.maximum(dist, 1).astype(_F32)
    large = max_exact + (jnp.log(d / max_exact) / math.log(MAX_DISTANCE / max_exact)
                         * (N_BUCKETS - max_exact)).astype(jnp.int32)
    large = jnp.minimum(large, N_BUCKETS - 1)
    return jnp.where(dist < max_exact, dist, large)


def _dilated_log_multiplicity(n):
    delta = jnp.arange(n)
    mult = jnp.zeros((n,), _F32)
    for window, dil in DILATED_PATTERNS:
        mult = mult + ((delta % dil == 0) & (delta <= window)).astype(_F32)
    return jnp.where(mult > 0, jnp.log(jnp.maximum(mult, 1.0)), NEG_BIG)


def _skewed_bias(bias_vec, n_off):
    t = SM_BLOCK
    h = bias_vec.shape[0]
    c = n_off * t
    width = c + t
    rr = jnp.take(bias_vec, jnp.clip(jnp.arange(width) - t, 0, c - 1), axis=1)
    return pl.pallas_call(
        _skew_kernel,
        grid=(h,),
        in_specs=[pl.BlockSpec((1, 1, width), lambda i: (i, 0, 0))],
        out_specs=pl.BlockSpec((1, t, width), lambda i: (i, 0, 0)),
        out_shape=jax.ShapeDtypeStruct((h, t, width), _F32),
        compiler_params=_cparams(("arbitrary",)),
        name="bias_skew",
    )(rr.reshape(h, 1, width))


def _skew_kernel(rr_ref, o_ref):
    _, t, width = o_ref.shape
    rows = SUBLANES
    base = pltpu.roll(jnp.broadcast_to(rr_ref[0], (rows, width)), 0, 1, stride=1, stride_axis=0)
    for g in range(t // rows):
        o_ref[0, g * rows:(g + 1) * rows, :] = pltpu.roll(base, g * rows, 1)


def _flash_attend(qts, k_refs, col0s, bias_ref, vt_ref, bufs, scale2, lo, qi, finish, scores_first=False):
    t = SM_BLOCK
    raw_a, raw_b, p_a, p_b, acc_ref = bufs
    chains = range(len(qts))
    n = qts[0].shape[1]

    def scores_into(ki, s_ref):
        start = pl.multiple_of(ki * t, t)
        bias = _bias_tile(bias_ref, qi - ki)
        tops = []
        for c in chains:
            s = _dot(k_refs[c][pl.ds(start, t), :], qts[c]) * scale2 + bias[:, col0s[c]:col0s[c] + n]
            s_ref[c] = s
            tops.append(jnp.max(s, axis=0, keepdims=True))
        return tuple(tops)

    def softmax_from(s_ref, tops, causal, stats, p_ref):
        new_stats, alphas, ps = [], [], []
        for c in chains:
            m = stats[c]
            s = s_ref[c]
            top = tops[c]
            if causal:
                key = lax.broadcasted_iota(jnp.int32, s.shape, 0)
                qry = lax.broadcasted_iota(jnp.int32, s.shape, 1) + col0s[c]
                s = jnp.where(key <= qry, s, NEG_BIG)
                top = jnp.max(s, axis=0, keepdims=True)
            m_new = jnp.maximum(m, top)
            alphas.append(jnp.exp2(m - m_new))
            p = jnp.exp2(s - m_new)
            new_stats.append(m_new)
            if p_ref is None:
                ps.append(p.astype(_BF))
            else:
                p_ref[c] = p.astype(_BF)
        return tuple(new_stats), tuple(alphas), ps

    def add_weighted_values(ki, p_src, alphas):
        vt_blk = vt_ref[:, pl.ds(pl.multiple_of(ki * t, t), t)]
        for c in chains:
            acc_ref[c] = acc_ref[c] * alphas[c] + _dot(vt_blk, p_src[c])

    def half_step(ki, s_cur, s_next, p_cur, p_prev, tops, stats, alphas):
        if scores_first:
            tops_next = scores_into(ki + 1, s_next)
        add_weighted_values(jnp.maximum(ki - 1, lo), p_prev, alphas)
        if not scores_first:
            tops_next = scores_into(ki + 1, s_next)
        stats, alphas, _ = softmax_from(s_cur, tops, False, stats, p_cur)
        return tops_next, stats, alphas

    def last_steps(s_ref, p_prev, tops, stats, alphas):
        add_weighted_values(jnp.maximum(qi - 1, lo), p_prev, alphas)
        stats, alphas, ps = softmax_from(s_ref, tops, True, stats, None)
        add_weighted_values(qi, ps, alphas)
        finish(tuple((acc_ref[c, HEAD_DIM:HEAD_DIM + 1, :], acc_ref[c, :HEAD_DIM, :]) for c in chains))

    for c in chains:
        acc_ref[c] = jnp.zeros(acc_ref.shape[1:], _F32)
        p_b[c] = jnp.zeros(p_b.shape[1:], _BF)
    tops = scores_into(lo, raw_a)
    stats = tuple(jnp.full((1, n), NEG_BIG, _F32) for _ in chains)
    alphas = tuple(jnp.ones((1, n), _F32) for _ in chains)
    n_full = qi - lo

    def pair(it, carry):
        k0 = lo + 2 * it
        carry = half_step(k0, raw_a, raw_b, p_a, p_b, *carry)
        return half_step(k0 + 1, raw_b, raw_a, p_b, p_a, *carry)

    tops, stats, alphas = lax.fori_loop(0, n_full // 2, pair, (tops, stats, alphas))

    @pl.when(n_full % 2 == 1)
    def _():
        last_steps(raw_b, p_a, *half_step(qi - 1, raw_a, raw_b, p_a, p_b, tops, stats, alphas))

    @pl.when(n_full % 2 == 0)
    def _():
        last_steps(raw_a, p_b, tops, stats, alphas)


def _bias_tile(bias_ref, off):
    t = SM_BLOCK
    return bias_ref[0, :, pl.ds(pl.multiple_of((off + 1) * t, t), t)]


def _fill_v_transposed(v_ref, vt_ref):
    t = SM_BLOCK
    pad = lax.broadcasted_iota(jnp.int32, (V_ROWS - HEAD_DIM, t), 0)
    for c in range(v_ref.shape[0] // t):
        vt_ref[:HEAD_DIM, c * t:(c + 1) * t] = v_ref[c * t:(c + 1) * t, :].astype(_F32).T.astype(_BF)
        vt_ref[HEAD_DIM:, c * t:(c + 1) * t] = jnp.where(pad == 0, 1.0, 0.0).astype(_BF)


def _dilated_kernel(q_ref, k_ref, v_ref, z_ref, bias_ref, o_ref, vt_ref, *bufs, n_off):
    t = SM_BLOCK
    half = t // 2
    scale2 = HEAD_DIM ** -0.5 * LOG2E
    _fill_v_transposed(v_ref, vt_ref)

    def query_block(qi, carry):
        rows = pl.ds(pl.multiple_of(qi * t, t), t)
        qt = q_ref[rows, :].astype(_F32).T.astype(_BF)

        def finish(outs):
            o = jnp.concatenate([acc / l for l, acc in outs], axis=1).T
            o_ref[rows, :] = (o * _silu(z_ref[rows, :].astype(_F32))).astype(o_ref.dtype)

        _flash_attend((qt[:, :half], qt[:, half:]), (k_ref, k_ref), (0, half), bias_ref, vt_ref, bufs, scale2,
                      jnp.maximum(qi - (n_off - 1), 0), qi, finish)
        return carry

    lax.fori_loop(0, q_ref.shape[0] // t, query_block, 0)


def _diff_kernel(q1_ref, q2_ref, k1_ref, k2_ref, v_ref, z_ref, bias_ref, lam_ref, g_ref, o_ref, vt_ref, *bufs,
                 lam_init):
    t = SM_BLOCK
    h = pl.program_id(1)
    scale2 = DIFF_QK_DIM ** -0.5 * LOG2E
    dim = lax.broadcasted_iota(jnp.int32, (HEAD_DIM, t), 0)
    mine = (dim // DIFF_QK_DIM) == (h % 2)
    _fill_v_transposed(v_ref, vt_ref)
    lv = lam_ref[...]
    lam = (jnp.exp(jnp.sum(lv[0:1] * lv[1:2], axis=1, keepdims=True))
           - jnp.exp(jnp.sum(lv[2:3] * lv[3:4], axis=1, keepdims=True)) + lam_init)

    def query_block(qi, carry):
        rows = pl.ds(pl.multiple_of(qi * t, t), t)
        q1t = jnp.where(mine, q1_ref[rows, :].astype(_F32).T, 0.0).astype(_BF)
        q2t = jnp.where(mine, q2_ref[rows, :].astype(_F32).T, 0.0).astype(_BF)

        def finish(outs):
            (l1, a1), (l2, a2) = outs
            o = a1 / l1 - lam * (a2 / l2)
            o = (o * lax.rsqrt(jnp.mean(o * o, axis=0, keepdims=True) + EPS)).T
            o = o * (g_ref[...] * (1.0 - lam_init))
            o_ref[rows, :] = (o * _silu(z_ref[rows, :].astype(_F32))).astype(o_ref.dtype)

        _flash_attend((q1t, q2t), (k1_ref, k2_ref), (0, 0), bias_ref, vt_ref, bufs, scale2, 0, qi, finish,
                      scores_first=True)
        return carry

    lax.fori_loop(0, q1_ref.shape[0] // t, query_block, 0)


def _stick_kernel(q_ref, k_ref, v_ref, z_ref, o_ref, acc_ref):
    t = STICK_BLOCK
    d = HEAD_DIM
    heads = range(q_ref.shape[1] // d)
    blocks_per_step = q_ref.shape[0] // t
    scale2 = d ** -0.5 * LOG2E
    row = lax.broadcasted_iota(jnp.int32, (t, t), 0)
    col = lax.broadcasted_iota(jnp.int32, (t, t), 1)
    later = jnp.where(row > col, 1.0, 0.0).astype(_BF)
    strict = col < row

    def query_block(j, carry):
        qi = pl.program_id(2) * blocks_per_step + j
        rows = pl.ds(pl.multiple_of(j * t, t), t)
        qs = [q_ref[rows, c * d:(c + 1) * d] for c in heads]

        def step(ki, tails, diagonal):
            start = pl.multiple_of(ki * t, t)
            zs = [_dot_nt(qs[c], k_ref[pl.ds(start, t), c * d:(c + 1) * d]) for c in heads]
            log_betas, parts = [], []
            for c in heads:
                z = zs[c] * scale2
                softplus = jnp.log(1.0 + jnp.exp2(-jnp.abs(z))) * LOG2E
                log_beta = jnp.minimum(z, 0.0) - softplus
                log_1m = log_beta - z
                if diagonal:
                    log_1m = jnp.where(strict, log_1m, 0.0)
                log_betas.append(log_beta)
                parts.append(_split_hi_lo(log_1m) + (log_1m[:, 0:1],))
            sums = [_dot(hi, later) + _dot(lo, later) for hi, lo, _ in parts]
            weights = []
            for c in heads:
                a = jnp.exp2(log_betas[c] + (tails[c] + sums[c]))
                if diagonal:
                    a = jnp.where(strict, a, 0.0)
                weights.append(a.astype(_BF))
            for c in heads:
                acc_ref[c] = acc_ref[c] + _dot(weights[c], v_ref[pl.ds(start, t), c * d:(c + 1) * d])
            return tuple(tails[c] + (sums[c][:, 0:1] + parts[c][2]) for c in heads)

        for c in heads:
            acc_ref[c] = jnp.zeros((t, d), _F32)
        tails = step(qi, tuple(jnp.zeros((t, 1), _F32) for _ in heads), True)

        def live(tails):
            worst = functools.reduce(jnp.maximum, tails)
            return (jnp.max(worst) > STICK_SKIP_LOG2).astype(jnp.int32)

        def cond(state):
            i, _, alive = state
            return jnp.logical_and(i < qi, alive > 0)

        def body(state):
            i, tails, _ = state
            tails = step(qi - 1 - i, tails, False)
            return i + 1, tails, live(tails)

        lax.while_loop(cond, body, (jnp.int32(0), tails, live(tails)))
        for c in heads:
            gate = _silu(z_ref[rows, c * d:(c + 1) * d].astype(_F32))
            o_ref[rows, c * d:(c + 1) * d] = (acc_ref[c] * gate).astype(o_ref.dtype)
        return carry

    lax.fori_loop(0, blocks_per_step, query_block, 0)


def _flash_buffers(chains, n):
    t = SM_BLOCK
    return [pltpu.VMEM((chains, t, n), _F32), pltpu.VMEM((chains, t, n), _F32),
            pltpu.VMEM((chains, t, n), _BF), pltpu.VMEM((chains, t, n), _BF),
            pltpu.VMEM((chains, V_ROWS, n), _F32)]


def _odd_mixer(proj, bsz, seq, d_inner):
    t = STICK_BLOCK
    w = STICK_HEADS * HEAD_DIM
    groups = d_inner // w
    rows = min(STICK_ROWS, seq)
    nr = seq // rows
    tile = lambda col0: pl.BlockSpec((rows, w), lambda b, g, r: (b * nr + r, col0 + g))
    whole = lambda col0: pl.BlockSpec((seq, w), lambda b, g, r: (b, col0 + g))
    return pl.pallas_call(
        _stick_kernel,
        grid=(bsz, groups, nr),
        in_specs=[tile(0), whole(groups), whole(2 * groups), tile(3 * groups)],
        out_specs=tile(0),
        out_shape=jax.ShapeDtypeStruct((bsz * seq, d_inner), _BF),
        scratch_shapes=[pltpu.VMEM((STICK_HEADS, t, HEAD_DIM), _F32)],
        compiler_params=_cparams(("arbitrary", "arbitrary", "arbitrary")),
        name="stick_breaking",
    )(proj, proj, proj, proj)


def _even_mixer(proj, bias_a, bias_b, lam_vec, subln_g, lam_init, bsz, seq, d_inner):
    d_a = d_inner // 2
    h_a = d_a // HEAD_DIM
    h_b = (d_inner - d_a) // HEAD_DIM
    t = SM_BLOCK
    n_off_a = bias_a.shape[2] // t - 1
    n_off_b = bias_b.shape[2] // t - 1
    out_shape = jax.ShapeDtypeStruct((bsz * seq, d_a), _BF)
    params = _cparams(("arbitrary", "arbitrary"))
    z0 = 3 * d_inner // HEAD_DIM
    col = lambda col0, per_block=1: pl.BlockSpec((seq, HEAD_DIM), lambda b, h: (b, col0 + h // per_block))
    bias_spec = lambda n_off: pl.BlockSpec((1, t, (n_off + 1) * t), lambda b, h: (h, 0, 0))

    o_a = pl.pallas_call(
        functools.partial(_dilated_kernel, n_off=n_off_a),
        grid=(bsz, h_a),
        in_specs=[col(0), col(h_a), col(2 * h_a), col(z0), bias_spec(n_off_a)],
        out_specs=col(0),
        out_shape=out_shape,
        scratch_shapes=[pltpu.VMEM((V_ROWS, seq), _BF)] + _flash_buffers(2, t // 2),
        compiler_params=params,
        name="dilated_attention",
    )(proj, proj, proj, proj, bias_a)

    c0 = 3 * h_a
    half = h_b // 2
    o_b = pl.pallas_call(
        functools.partial(_diff_kernel, lam_init=lam_init),
        grid=(bsz, h_b),
        in_specs=[col(c0, 2), col(c0 + half, 2), col(c0 + 2 * half, 2), col(c0 + 3 * half, 2),
                  col(c0 + 4 * half), col(z0 + h_a), bias_spec(n_off_b),
                  pl.BlockSpec((4, DIFF_QK_DIM), lambda b, h: (0, 0)),
                  pl.BlockSpec((1, HEAD_DIM), lambda b, h: (0, 0))],
        out_specs=col(0),
        out_shape=out_shape,
        scratch_shapes=[pltpu.VMEM((V_ROWS, seq), _BF)] + _flash_buffers(2, t),
        compiler_params=params,
        name="diff_attention",
    )(proj, proj, proj, proj, proj, proj, bias_b, lam_vec, subln_g.reshape(1, HEAD_DIM))
    return o_a, o_b


def kernel(x, c, norm_g, w_mod, b_mod, w_in, w_out, rel_bias, diff_lambda, diff_subln_g, final_norm_g):
    bsz, seq, d = x.shape
    depth = w_in.shape[0]
    d_inner = w_out.shape[1]
    h_a = d_inner // 2 // HEAD_DIM
    n_blk = seq // SM_BLOCK

    mod = _modulation(c, w_mod, b_mod)

    table = rel_bias[_t5_bucket(jnp.arange(seq))].T
    n_off_a = min(n_blk, DILATED_PATTERNS[-1][0] // SM_BLOCK + 1)
    vec_a = table[:h_a, : n_off_a * SM_BLOCK] + _dilated_log_multiplicity(n_off_a * SM_BLOCK)[None]
    bias_a = _skewed_bias(vec_a * LOG2E, n_off_a)
    bias_b = _skewed_bias(table[h_a:] * LOG2E, n_blk)

    h = x.reshape(bsz * seq, d)
    for layer in range(depth):
        shift = mod[layer, :, 0:d].reshape(bsz, 1, d)
        scale = mod[layer, :, d:2 * d].reshape(bsz, 1, d)
        gate = mod[layer, :, 2 * d:].reshape(bsz, 1, d)
        u = _norm_mod(h, norm_g[layer], scale, shift, seq)
        proj = _in_proj(u, w_in, layer)
        if layer % 2 == 0:
            e = layer // 2
            lam_init = 0.8 - 0.6 * math.exp(-0.3 * layer)
            mixed = _even_mixer(proj, bias_a, bias_b, diff_lambda[e], diff_subln_g[e], lam_init,
                                bsz, seq, d_inner)
        else:
            mixed = (_odd_mixer(proj, bsz, seq, d_inner),)
        h = _out_proj(mixed, w_out, layer, h, gate, seq)
    return _final_norm(h, final_norm_g).reshape(bsz, seq, d)
```

```python
import functools
import math

import jax
import jax.numpy as jnp
from jax import lax
from jax.experimental import pallas as pl
from jax.experimental.pallas import tpu as pltpu

HEAD_DIM = 128
DIFF_QK_DIM = 64
N_BUCKETS = 32
MAX_DISTANCE = 2048
DILATED_PATTERNS = ((128, 1), (512, 4), (2048, 16))
EPS = 1e-6
SM_BLOCK = 512
STICK_BLOCK = 256
STICK_HEADS = 8
STICK_ROWS = 1024
STICK_SKIP_LOG2 = -150.0
NEG_BIG = -1e30
LOG2E = math.log2(math.e)
SUBLANES = 8
V_ROWS = HEAD_DIM + 2 * SUBLANES
VMEM_LIMIT = 56 * 1024 * 1024
ROW_TILE = 1024
MOD_TILE_N = 1024
PROJ_TILE = (2048, 1024)
OUT_TILE = (256, 1024)

_BF = jnp.bfloat16
_F32 = jnp.float32


def _cparams(sem):
    return pltpu.CompilerParams(dimension_semantics=sem, vmem_limit_bytes=VMEM_LIMIT)


def _dot(a, b):
    return jnp.dot(a, b, preferred_element_type=_F32)


def _dot_nt(a, b):
    return lax.dot_general(a, b, (((1,), (1,)), ((), ())), preferred_element_type=_F32)


def _split_hi_lo(x):
    hi = x.astype(_BF)
    lo = (x - hi.astype(_F32)).astype(_BF)
    return hi, lo


def _silu(z):
    return z / (1.0 + jnp.exp(-z))


def _mod_kernel(c_ref, w_ref, b_ref, o_ref):
    a_hi, a_lo = _split_hi_lo(_silu(c_ref[...]))
    w_hi, w_lo = _split_hi_lo(w_ref[0])
    acc = _dot(a_hi, w_hi) + _dot(a_lo, w_hi) + _dot(a_hi, w_lo)
    o_ref[0] = acc + b_ref[0]


def _modulation(c, w_mod, b_mod):
    depth, d, n = w_mod.shape
    bsz = c.shape[0]
    rows = SUBLANES
    tn = MOD_TILE_N if n % MOD_TILE_N == 0 else n
    c_pad = jnp.zeros((rows, d), _F32).at[:bsz].set(c)
    out = pl.pallas_call(
        _mod_kernel,
        grid=(depth, n // tn),
        in_specs=[
            pl.BlockSpec((rows, d), lambda l, j: (0, 0)),
            pl.BlockSpec((1, d, tn), lambda l, j: (l, 0, j)),
            pl.BlockSpec((1, 1, tn), lambda l, j: (l, 0, j)),
        ],
        out_specs=pl.BlockSpec((1, rows, tn), lambda l, j: (l, 0, j)),
        out_shape=jax.ShapeDtypeStruct((depth, rows, n), _F32),
        compiler_params=_cparams(("arbitrary", "arbitrary")),
        name="modulation",
    )(c_pad, w_mod, b_mod.reshape(depth, 1, n))
    return out[:, :bsz]


def _norm_mod_kernel(h_ref, g_ref, scale_ref, shift_ref, o_ref):
    h = h_ref[...]
    inv = lax.rsqrt(jnp.mean(h * h, axis=-1, keepdims=True) + EPS)
    gain = g_ref[...] * (1.0 + scale_ref[0])
    o_ref[...] = (h * inv * gain + shift_ref[0]).astype(o_ref.dtype)


def _norm_mod(h, g, scale, shift, seq):
    m, d = h.shape
    tm = ROW_TILE
    per_b = seq // tm
    return pl.pallas_call(
        _norm_mod_kernel,
        grid=(m // tm,),
        in_specs=[
            pl.BlockSpec((tm, d), lambda i: (i, 0)),
            pl.BlockSpec((1, d), lambda i: (0, 0)),
            pl.BlockSpec((1, 1, d), lambda i: (i // per_b, 0, 0)),
            pl.BlockSpec((1, 1, d), lambda i: (i // per_b, 0, 0)),
        ],
        out_specs=pl.BlockSpec((tm, d), lambda i: (i, 0)),
        out_shape=jax.ShapeDtypeStruct((m, d), _BF),
        compiler_params=_cparams(("arbitrary",)),
        name="norm_mod",
    )(h, g.reshape(1, d), scale, shift)


def _final_norm_kernel(h_ref, g_ref, o_ref):
    h = h_ref[...]
    inv = lax.rsqrt(jnp.mean(h * h, axis=-1, keepdims=True) + EPS)
    o_ref[...] = h * inv * g_ref[...]


def _final_norm(h, g):
    m, d = h.shape
    tm = ROW_TILE
    return pl.pallas_call(
        _final_norm_kernel,
        grid=(m // tm,),
        in_specs=[pl.BlockSpec((tm, d), lambda i: (i, 0)), pl.BlockSpec((1, d), lambda i: (0, 0))],
        out_specs=pl.BlockSpec((tm, d), lambda i: (i, 0)),
        out_shape=jax.ShapeDtypeStruct((m, d), _F32),
        compiler_params=_cparams(("arbitrary",)),
        name="final_norm",
    )(h, g.reshape(1, d))


def _proj_kernel(u_ref, w_ref, o_ref, wbf_ref):
    @pl.when(pl.program_id(1) == 0)
    def _():
        wbf_ref[...] = w_ref[0].astype(_BF)

    o_ref[...] = _dot(u_ref[...], wbf_ref[...]).astype(o_ref.dtype)


def _in_proj(u, w_in, layer):
    m, d = u.shape
    p = w_in.shape[2]
    tm, tn = min(PROJ_TILE[0], m), min(PROJ_TILE[1], p)
    return pl.pallas_call(
        _proj_kernel,
        grid=(p // tn, m // tm),
        in_specs=[
            pl.BlockSpec((tm, d), lambda j, i: (i, 0)),
            pl.BlockSpec((1, d, tn), lambda j, i: (layer, 0, j)),
        ],
        out_specs=pl.BlockSpec((tm, tn), lambda j, i: (i, j)),
        out_shape=jax.ShapeDtypeStruct((m, p), _BF),
        scratch_shapes=[pltpu.VMEM((d, tn), _BF)],
        compiler_params=_cparams(("arbitrary", "arbitrary")),
        name="in_proj",
    )(u, w_in)


def _out_kernel(*refs):
    *a_refs, w_ref, h_ref, gate_ref, o_ref, wbf_ref = refs

    @pl.when(pl.program_id(1) == 0)
    def _():
        wbf_ref[...] = w_ref[0].astype(_BF)

    y, k0 = None, 0
    for a_ref in a_refs:
        k1 = k0 + a_ref.shape[1]
        part = _dot(a_ref[...], wbf_ref[k0:k1, :])
        y = part if y is None else y + part
        k0 = k1
    o_ref[...] = h_ref[...] + gate_ref[0] * y


def _out_proj(parts, w_out, layer, h, gate, seq):
    m = parts[0].shape[0]
    k, d = w_out.shape[1], w_out.shape[2]
    tm, tn = OUT_TILE[0], min(OUT_TILE[1], d)
    per_b = seq // tm
    return pl.pallas_call(
        _out_kernel,
        grid=(d // tn, m // tm),
        in_specs=[pl.BlockSpec((tm, p.shape[1]), lambda j, i: (i, 0)) for p in parts] + [
            pl.BlockSpec((1, k, tn), lambda j, i: (layer, 0, j)),
            pl.BlockSpec((tm, tn), lambda j, i: (i, j)),
            pl.BlockSpec((1, 1, tn), lambda j, i: (i // per_b, 0, j)),
        ],
        out_specs=pl.BlockSpec((tm, tn), lambda j, i: (i, j)),
        out_shape=jax.ShapeDtypeStruct((m, d), _F32),
        scratch_shapes=[pltpu.VMEM((k, tn), _BF)],
        compiler_params=_cparams(("arbitrary", "arbitrary")),
        name="out_proj",
    )(*parts, w_out, h, gate)


def _t5_bucket(dist):
    max_exact = N_BUCKETS // 2
    d = jnp.maximum(dist, 1).astype(_F32)
    large = max_exact + (jnp.log(d / max_exact) / math.log(MAX_DISTANCE / max_exact)
                         * (N_BUCKETS - max_exact)).astype(jnp.int32)
    large = jnp.minimum(large, N_BUCKETS - 1)
    return jnp.where(dist < max_exact, dist, large)


def _dilated_log_multiplicity(n):
    delta = jnp.arange(n)
    mult = jnp.zeros((n,), _F32)
    for window, dil in DILATED_PATTERNS:
        mult = mult + ((delta % dil == 0) & (delta <= window)).astype(_F32)
    return jnp.where(mult > 0, jnp.log(jnp.maximum(mult, 1.0)), NEG_BIG)


def _skewed_bias(bias_vec, n_off):
    t = SM_BLOCK
    h = bias_vec.shape[0]
    c = n_off * t
    width = c + t
    rr = jnp.take(bias_vec, jnp.clip(jnp.arange(width) - t, 0, c - 1), axis=1)
    return pl.pallas_call(
        _skew_kernel,
        grid=(h,),
        in_specs=[pl.BlockSpec((1, 1, width), lambda i: (i, 0, 0))],
        out_specs=pl.BlockSpec((1, t, width), lambda i: (i, 0, 0)),
        out_shape=jax.ShapeDtypeStruct((h, t, width), _F32),
        compiler_params=_cparams(("arbitrary",)),
        name="bias_skew",
    )(rr.reshape(h, 1, width))


def _skew_kernel(rr_ref, o_ref):
    _, t, width = o_ref.shape
    rows = SUBLANES
    base = pltpu.roll(jnp.broadcast_to(rr_ref[0], (rows, width)), 0, 1, stride=1, stride_axis=0)
    for g in range(t // rows):
        o_ref[0, g * rows:(g + 1) * rows, :] = pltpu.roll(base, g * rows, 1)


def _flash_attend(qts, k_refs, col0s, bias_ref, vt_ref, bufs, scale2, lo, qi, finish, scores_first=False):
    t = SM_BLOCK
    raw_a, raw_b, p_a, p_b, acc_ref = bufs
    chains = range(len(qts))
    n = qts[0].shape[1]

    def scores_into(ki, s_ref):
        start = pl.multiple_of(ki * t, t)
        bias = _bias_tile(bias_ref, qi - ki)
        tops = []
        for c in chains:
            s = _dot(k_refs[c][pl.ds(start, t), :], qts[c]) * scale2 + bias[:, col0s[c]:col0s[c] + n]
            s_ref[c] = s
            tops.append(jnp.max(s, axis=0, keepdims=True))
        return tuple(tops)

    def softmax_from(s_ref, tops, causal, stats, p_ref):
        new_stats, alphas, ps = [], [], []
        for c in chains:
            m = stats[c]
            s = s_ref[c]
            top = tops[c]
            if causal:
                key = lax.broadcasted_iota(jnp.int32, s.shape, 0)
                qry = lax.broadcasted_iota(jnp.int32, s.shape, 1) + col0s[c]
                s = jnp.where(key <= qry, s, NEG_BIG)
                top = jnp.max(s, axis=0, keepdims=True)
            m_new = jnp.maximum(m, top)
            alphas.append(jnp.exp2(m - m_new))
            p = jnp.exp2(s - m_new)
            new_stats.append(m_new)
            if p_ref is None:
                ps.append(p.astype(_BF))
            else:
                p_ref[c] = p.astype(_BF)
        return tuple(new_stats), tuple(alphas), ps

    def add_weighted_values(ki, p_src, alphas):
        vt_blk = vt_ref[:, pl.ds(pl.multiple_of(ki * t, t), t)]
        for c in chains:
            acc_ref[c] = acc_ref[c] * alphas[c] + _dot(vt_blk, p_src[c])

    def half_step(ki, s_cur, s_next, p_cur, p_prev, tops, stats, alphas):
        if scores_first:
            tops_next = scores_into(ki + 1, s_next)
        add_weighted_values(jnp.maximum(ki - 1, lo), p_prev, alphas)
        if not scores_first:
            tops_next = scores_into(ki + 1, s_next)
        stats, alphas, _ = softmax_from(s_cur, tops, False, stats, p_cur)
        return tops_next, stats, alphas

    def last_steps(s_ref, p_prev, tops, stats, alphas):
        add_weighted_values(jnp.maximum(qi - 1, lo), p_prev, alphas)
        stats, alphas, ps = softmax_from(s_ref, tops, True, stats, None)
        add_weighted_values(qi, ps, alphas)
        finish(tuple((acc_ref[c, HEAD_DIM:HEAD_DIM + 1, :], acc_ref[c, :HEAD_DIM, :]) for c in chains))

    for c in chains:
        acc_ref[c] = jnp.zeros(acc_ref.shape[1:], _F32)
        p_b[c] = jnp.zeros(p_b.shape[1:], _BF)
    tops = scores_into(lo, raw_a)
    stats = tuple(jnp.full((1, n), NEG_BIG, _F32) for _ in chains)
    alphas = tuple(jnp.ones((1, n), _F32) for _ in chains)
    n_full = qi - lo

    def pair(it, carry):
        k0 = lo + 2 * it
        carry = half_step(k0, raw_a, raw_b, p_a, p_b, *carry)
        return half_step(k0 + 1, raw_b, raw_a, p_b, p_a, *carry)

    tops, stats, alphas = lax.fori_loop(0, n_full // 2, pair, (tops, stats, alphas))

    @pl.when(n_full % 2 == 1)
    def _():
        last_steps(raw_b, p_a, *half_step(qi - 1, raw_a, raw_b, p_a, p_b, tops, stats, alphas))

    @pl.when(n_full % 2 == 0)
    def _():
        last_steps(raw_a, p_b, tops, stats, alphas)


def _bias_tile(bias_ref, off):
    t = SM_BLOCK
    return bias_ref[0, :, pl.ds(pl.multiple_of((off + 1) * t, t), t)]


def _fill_v_transposed(v_ref, vt_ref):
    t = SM_BLOCK
    pad = lax.broadcasted_iota(jnp.int32, (V_ROWS - HEAD_DIM, t), 0)
    for c in range(v_ref.shape[0] // t):
        vt_ref[:HEAD_DIM, c * t:(c + 1) * t] = v_ref[c * t:(c + 1) * t, :].astype(_F32).T.astype(_BF)
        vt_ref[HEAD_DIM:, c * t:(c + 1) * t] = jnp.where(pad == 0, 1.0, 0.0).astype(_BF)


def _dilated_kernel(q_ref, k_ref, v_ref, z_ref, bias_ref, o_ref, vt_ref, *bufs, n_off):
    t = SM_BLOCK
    half = t // 2
    scale2 = HEAD_DIM ** -0.5 * LOG2E
    _fill_v_transposed(v_ref, vt_ref)

    def query_block(qi, carry):
        rows = pl.ds(pl.multiple_of(qi * t, t), t)
        qt = q_ref[rows, :].astype(_F32).T.astype(_BF)

        def finish(outs):
            o = jnp.concatenate([acc / l for l, acc in outs], axis=1).T
            o_ref[rows, :] = (o * _silu(z_ref[rows, :].astype(_F32))).astype(o_ref.dtype)

        _flash_attend((qt[:, :half], qt[:, half:]), (k_ref, k_ref), (0, half), bias_ref, vt_ref, bufs, scale2,
                      jnp.maximum(qi - (n_off - 1), 0), qi, finish)
        return carry

    lax.fori_loop(0, q_ref.shape[0] // t, query_block, 0)


def _diff_kernel(q1_ref, q2_ref, k1_ref, k2_ref, v_ref, z_ref, bias_ref, lam_ref, g_ref, o_ref, vt_ref, *bufs,
                 lam_init):
    t = SM_BLOCK
    h = pl.program_id(1)
    scale2 = DIFF_QK_DIM ** -0.5 * LOG2E
    dim = lax.broadcasted_iota(jnp.int32, (HEAD_DIM, t), 0)
    mine = (dim // DIFF_QK_DIM) == (h % 2)
    _fill_v_transposed(v_ref, vt_ref)
    lv = lam_ref[...]
    lam = (jnp.exp(jnp.sum(lv[0:1] * lv[1:2], axis=1, keepdims=True))
           - jnp.exp(jnp.sum(lv[2:3] * lv[3:4], axis=1, keepdims=True)) + lam_init)

    def query_block(qi, carry):
        rows = pl.ds(pl.multiple_of(qi * t, t), t)
        q1t = jnp.where(mine, q1_ref[rows, :].astype(_F32).T, 0.0).astype(_BF)
        q2t = jnp.where(mine, q2_ref[rows, :].astype(_F32).T, 0.0).astype(_BF)

        def finish(outs):
            (l1, a1), (l2, a2) = outs
            o = a1 / l1 - lam * (a2 / l2)
            o = (o * lax.rsqrt(jnp.mean(o * o, axis=0, keepdims=True) + EPS)).T
            o = o * (g_ref[...] * (1.0 - lam_init))
            o_ref[rows, :] = (o * _silu(z_ref[rows, :].astype(_F32))).astype(o_ref.dtype)

        _flash_attend((q1t, q2t), (k1_ref, k2_ref), (0, 0), bias_ref, vt_ref, bufs, scale2, 0, qi, finish,
                      scores_first=True)
        return carry

    lax.fori_loop(0, q1_ref.shape[0] // t, query_block, 0)


def _stick_kernel(q_ref, k_ref, v_ref, z_ref, o_ref, acc_ref):
    t = STICK_BLOCK
    d = HEAD_DIM
    heads = range(q_ref.shape[1] // d)
    blocks_per_step = q_ref.shape[0] // t
    scale2 = d ** -0.5 * LOG2E
    row = lax.broadcasted_iota(jnp.int32, (t, t), 0)
    col = lax.broadcasted_iota(jnp.int32, (t, t), 1)
    later = jnp.where(row > col, 1.0, 0.0).astype(_BF)
    strict = col < row

    def query_block(j, carry):
        qi = pl.program_id(2) * blocks_per_step + j
        rows = pl.ds(pl.multiple_of(j * t, t), t)
        qs = [q_ref[rows, c * d:(c + 1) * d] for c in heads]

        def step(ki, tails, diagonal):
            start = pl.multiple_of(ki * t, t)
            zs = [_dot_nt(qs[c], k_ref[pl.ds(start, t), c * d:(c + 1) * d]) for c in heads]
            log_betas, parts = [], []
            for c in heads:
                z = zs[c] * scale2
                softplus = jnp.log(1.0 + jnp.exp2(-jnp.abs(z))) * LOG2E
                log_beta = jnp.minimum(z, 0.0) - softplus
                log_1m = log_beta - z
                if diagonal:
                    log_1m = jnp.where(strict, log_1m, 0.0)
                log_betas.append(log_beta)
                parts.append(_split_hi_lo(log_1m) + (log_1m[:, 0:1],))
            sums = [_dot(hi, later) + _dot(lo, later) for hi, lo, _ in parts]
            weights = []
            for c in heads:
                a = jnp.exp2(log_betas[c] + (tails[c] + sums[c]))
                if diagonal:
                    a = jnp.where(strict, a, 0.0)
                weights.append(a.astype(_BF))
            for c in heads:
                acc_ref[c] = acc_ref[c] + _dot(weights[c], v_ref[pl.ds(start, t), c * d:(c + 1) * d])
            return tuple(tails[c] + (sums[c][:, 0:1] + parts[c][2]) for c in heads)

        for c in heads:
            acc_ref[c] = jnp.zeros((t, d), _F32)
        tails = step(qi, tuple(jnp.zeros((t, 1), _F32) for _ in heads), True)

        def live(tails):
            worst = functools.reduce(jnp.maximum, tails)
            return (jnp.max(worst) > STICK_SKIP_LOG2).astype(jnp.int32)

        def cond(state):
            i, _, alive = state
            return jnp.logical_and(i < qi, alive > 0)

        def body(state):
            i, tails, _ = state
            tails = step(qi - 1 - i, tails, False)
            return i + 1, tails, live(tails)

        lax.while_loop(cond, body, (jnp.int32(0), tails, live(tails)))
        for c in heads:
            gate = _silu(z_ref[rows, c * d:(c + 1) * d].astype(_F32))
            o_ref[rows, c * d:(c + 1) * d] = (acc_ref[c] * gate).astype(o_ref.dtype)
        return carry

    lax.fori_loop(0, blocks_per_step, query_block, 0)


def _flash_buffers(chains, n):
    t = SM_BLOCK
    return [pltpu.VMEM((chains, t, n), _F32), pltpu.VMEM((chains, t, n), _F32),
            pltpu.VMEM((chains, t, n), _BF), pltpu.VMEM((chains, t, n), _BF),
            pltpu.VMEM((chains, V_ROWS, n), _F32)]


def _odd_mixer(proj, bsz, seq, d_inner):
    t = STICK_BLOCK
    w = STICK_HEADS * HEAD_DIM
    groups = d_inner // w
    rows = min(STICK_ROWS, seq)
    nr = seq // rows
    tile = lambda col0: pl.BlockSpec((rows, w), lambda b, g, r: (b * nr + r, col0 + g))
    whole = lambda col0: pl.BlockSpec((seq, w), lambda b, g, r: (b, col0 + g))
    return pl.pallas_call(
        _stick_kernel,
        grid=(bsz, groups, nr),
        in_specs=[tile(0), whole(groups), whole(2 * groups), tile(3 * groups)],
        out_specs=tile(0),
        out_shape=jax.ShapeDtypeStruct((bsz * seq, d_inner), _BF),
        scratch_shapes=[pltpu.VMEM((STICK_HEADS, t, HEAD_DIM), _F32)],
        compiler_params=_cparams(("arbitrary", "arbitrary", "arbitrary")),
        name="stick_breaking",
    )(proj, proj, proj, proj)


def _even_mixer(proj, bias_a, bias_b, lam_vec, subln_g, lam_init, bsz, seq, d_inner):
    d_a = d_inner // 2
    h_a = d_a // HEAD_DIM
    h_b = (d_inner - d_a) // HEAD_DIM
    t = SM_BLOCK
    n_off_a = bias_a.shape[2] // t - 1
    n_off_b = bias_b.shape[2] // t - 1
    out_shape = jax.ShapeDtypeStruct((bsz * seq, d_a), _BF)
    params = _cparams(("arbitrary", "arbitrary"))
    z0 = 3 * d_inner // HEAD_DIM
    col = lambda col0, per_block=1: pl.BlockSpec((seq, HEAD_DIM), lambda b, h: (b, col0 + h // per_block))
    bias_spec = lambda n_off: pl.BlockSpec((1, t, (n_off + 1) * t), lambda b, h: (h, 0, 0))

    o_a = pl.pallas_call(
        functools.partial(_dilated_kernel, n_off=n_off_a),
        grid=(bsz, h_a),
        in_specs=[col(0), col(h_a), col(2 * h_a), col(z0), bias_spec(n_off_a)],
        out_specs=col(0),
        out_shape=out_shape,
        scratch_shapes=[pltpu.VMEM((V_ROWS, seq), _BF)] + _flash_buffers(2, t // 2),
        compiler_params=params,
        name="dilated_attention",
    )(proj, proj, proj, proj, bias_a)

    c0 = 3 * h_a
    half = h_b // 2
    o_b = pl.pallas_call(
        functools.partial(_diff_kernel, lam_init=lam_init),
        grid=(bsz, h_b),
        in_specs=[col(c0, 2), col(c0 + half, 2), col(c0 + 2 * half, 2), col(c0 + 3 * half, 2),
                  col(c0 + 4 * half), col(z0 + h_a), bias_spec(n_off_b),
                  pl.BlockSpec((4, DIFF_QK_DIM), lambda b, h: (0, 0)),
                  pl.BlockSpec((1, HEAD_DIM), lambda b, h: (0, 0))],
        out_specs=col(0),
        out_shape=out_shape,
        scratch_shapes=[pltpu.VMEM((V_ROWS, seq), _BF)] + _flash_buffers(2, t),
        compiler_params=params,
        name="diff_attention",
    )(proj, proj, proj, proj, proj, proj, bias_b, lam_vec, subln_g.reshape(1, HEAD_DIM))
    return o_a, o_b


def kernel(x, c, norm_g, w_mod, b_mod, w_in, w_out, rel_bias, diff_lambda, diff_subln_g, final_norm_g):
    bsz, seq, d = x.shape
    depth = w_in.shape[0]
    d_inner = w_out.shape[1]
    h_a = d_inner // 2 // HEAD_DIM
    n_blk = seq // SM_BLOCK

    mod = _modulation(c, w_mod, b_mod)

    table = rel_bias[_t5_bucket(jnp.arange(seq))].T
    n_off_a = min(n_blk, DILATED_PATTERNS[-1][0] // SM_BLOCK + 1)
    vec_a = table[:h_a, : n_off_a * SM_BLOCK] + _dilated_log_multiplicity(n_off_a * SM_BLOCK)[None]
    bias_a = _skewed_bias(vec_a * LOG2E, n_off_a)
    bias_b = _skewed_bias(table[h_a:] * LOG2E, n_blk)

    h = x.reshape(bsz * seq, d)
    for layer in range(depth):
        shift = mod[layer, :, 0:d].reshape(bsz, 1, d)
        scale = mod[layer, :, d:2 * d].reshape(bsz, 1, d)
        gate = mod[layer, :, 2 * d:].reshape(bsz, 1, d)
        u = _norm_mod(h, norm_g[layer], scale, shift, seq)
        proj = _in_proj(u, w_in, layer)
        if layer % 2 == 0:
            e = layer // 2
            lam_init = 0.8 - 0.6 * math.exp(-0.3 * layer)
            mixed = _even_mixer(proj, bias_a, bias_b, diff_lambda[e], diff_subln_g[e], lam_init,
                                bsz, seq, d_inner)
        else:
            mixed = (_odd_mixer(proj, bsz, seq, d_inner),)
        h = _out_proj(mixed, w_out, layer, h, gate, seq)
    return _final_norm(h, final_norm_g).reshape(bsz, seq, d)
```

```python
import functools
import math

import jax
import jax.numpy as jnp
from jax import lax
from jax.experimental import pallas as pl
from jax.experimental.pallas import tpu as pltpu

HEAD_DIM = 128
DIFF_QK_DIM = 64
N_BUCKETS = 32
MAX_DISTANCE = 2048
DILATED_PATTERNS = ((128, 1), (512, 4), (2048, 16))
EPS = 1e-6
SM_BLOCK = 512
STICK_BLOCK = 256
STICK_HEADS = 8
STICK_ROWS = 1024
STICK_SKIP_LOG2 = -150.0
NEG_BIG = -1e30
LOG2E = math.log2(math.e)
SUBLANES = 8
V_ROWS = HEAD_DIM + 2 * SUBLANES
VMEM_LIMIT = 56 * 1024 * 1024
ROW_TILE = 1024
MOD_TILE_N = 1024
PROJ_TILE = (2048, 1024)
OUT_TILE = (1024, 512)

_BF = jnp.bfloat16
_F32 = jnp.float32


def _cparams(sem):
    return pltpu.CompilerParams(dimension_semantics=sem, vmem_limit_bytes=VMEM_LIMIT)


def _dot(a, b):
    return jnp.dot(a, b, preferred_element_type=_F32)


def _dot_nt(a, b):
    return lax.dot_general(a, b, (((1,), (1,)), ((), ())), preferred_element_type=_F32)


def _split_hi_lo(x):
    hi = x.astype(_BF)
    lo = (x - hi.astype(_F32)).astype(_BF)
    return hi, lo


def _silu(z):
    return z / (1.0 + jnp.exp(-z))


def _mod_kernel(c_ref, w_ref, b_ref, o_ref):
    a_hi, a_lo = _split_hi_lo(_silu(c_ref[...]))
    w_hi, w_lo = _split_hi_lo(w_ref[0])
    acc = _dot(a_hi, w_hi) + _dot(a_lo, w_hi) + _dot(a_hi, w_lo)
    o_ref[0] = acc + b_ref[0]


def _modulation(c, w_mod, b_mod):
    depth, d, n = w_mod.shape
    bsz = c.shape[0]
    rows = SUBLANES
    tn = MOD_TILE_N if n % MOD_TILE_N == 0 else n
    c_pad = jnp.zeros((rows, d), _F32).at[:bsz].set(c)
    out = pl.pallas_call(
        _mod_kernel,
        grid=(depth, n // tn),
        in_specs=[
            pl.BlockSpec((rows, d), lambda l, j: (0, 0)),
            pl.BlockSpec((1, d, tn), lambda l, j: (l, 0, j)),
            pl.BlockSpec((1, 1, tn), lambda l, j: (l, 0, j)),
        ],
        out_specs=pl.BlockSpec((1, rows, tn), lambda l, j: (l, 0, j)),
        out_shape=jax.ShapeDtypeStruct((depth, rows, n), _F32),
        compiler_params=_cparams(("arbitrary", "arbitrary")),
        name="modulation",
    )(c_pad, w_mod, b_mod.reshape(depth, 1, n))
    return out[:, :bsz]


def _norm_mod_kernel(h_ref, g_ref, scale_ref, shift_ref, o_ref):
    h = h_ref[...]
    inv = lax.rsqrt(jnp.mean(h * h, axis=-1, keepdims=True) + EPS)
    gain = g_ref[...] * (1.0 + scale_ref[0])
    o_ref[...] = (h * inv * gain + shift_ref[0]).astype(o_ref.dtype)


def _norm_mod(h, g, scale, shift, seq):
    m, d = h.shape
    tm = ROW_TILE
    per_b = seq // tm
    return pl.pallas_call(
        _norm_mod_kernel,
        grid=(m // tm,),
        in_specs=[
            pl.BlockSpec((tm, d), lambda i: (i, 0)),
            pl.BlockSpec((1, d), lambda i: (0, 0)),
            pl.BlockSpec((1, 1, d), lambda i: (i // per_b, 0, 0)),
            pl.BlockSpec((1, 1, d), lambda i: (i // per_b, 0, 0)),
        ],
        out_specs=pl.BlockSpec((tm, d), lambda i: (i, 0)),
        out_shape=jax.ShapeDtypeStruct((m, d), _BF),
        compiler_params=_cparams(("arbitrary",)),
        name="norm_mod",
    )(h, g.reshape(1, d), scale, shift)


def _final_norm_kernel(h_ref, g_ref, o_ref):
    h = h_ref[...]
    inv = lax.rsqrt(jnp.mean(h * h, axis=-1, keepdims=True) + EPS)
    o_ref[...] = h * inv * g_ref[...]


def _final_norm(h, g):
    m, d = h.shape
    tm = ROW_TILE
    return pl.pallas_call(
        _final_norm_kernel,
        grid=(m // tm,),
        in_specs=[pl.BlockSpec((tm, d), lambda i: (i, 0)), pl.BlockSpec((1, d), lambda i: (0, 0))],
        out_specs=pl.BlockSpec((tm, d), lambda i: (i, 0)),
        out_shape=jax.ShapeDtypeStruct((m, d), _F32),
        compiler_params=_cparams(("arbitrary",)),
        name="final_norm",
    )(h, g.reshape(1, d))


def _proj_kernel(u_ref, w_ref, o_ref, wbf_ref):
    @pl.when(pl.program_id(1) == 0)
    def _():
        wbf_ref[...] = w_ref[0].astype(_BF)

    o_ref[...] = _dot(u_ref[...], wbf_ref[...]).astype(o_ref.dtype)


def _in_proj(u, w_in, layer):
    m, d = u.shape
    p = w_in.shape[2]
    tm, tn = min(PROJ_TILE[0], m), min(PROJ_TILE[1], p)
    return pl.pallas_call(
        _proj_kernel,
        grid=(p // tn, m // tm),
        in_specs=[
            pl.BlockSpec((tm, d), lambda j, i: (i, 0)),
            pl.BlockSpec((1, d, tn), lambda j, i: (layer, 0, j)),
        ],
        out_specs=pl.BlockSpec((tm, tn), lambda j, i: (i, j)),
        out_shape=jax.ShapeDtypeStruct((m, p), _BF),
        scratch_shapes=[pltpu.VMEM((d, tn), _BF)],
        compiler_params=_cparams(("arbitrary", "arbitrary")),
        name="in_proj",
    )(u, w_in)


def _out_kernel(*refs):
    *a_refs, w_ref, h_ref, gate_ref, o_ref, wbf_ref = refs

    @pl.when(pl.program_id(1) == 0)
    def _():
        wbf_ref[...] = w_ref[0].astype(_BF)

    y, k0 = None, 0
    for a_ref in a_refs:
        k1 = k0 + a_ref.shape[1]
        part = _dot(a_ref[...], wbf_ref[k0:k1, :])
        y = part if y is None else y + part
        k0 = k1
    o_ref[...] = h_ref[...] + gate_ref[0] * y


def _out_proj(parts, w_out, layer, h, gate, seq):
    m = parts[0].shape[0]
    k, d = w_out.shape[1], w_out.shape[2]
    tm, tn = OUT_TILE[0], min(OUT_TILE[1], d)
    per_b = seq // tm
    return pl.pallas_call(
        _out_kernel,
        grid=(d // tn, m // tm),
        in_specs=[pl.BlockSpec((tm, p.shape[1]), lambda j, i: (i, 0)) for p in parts] + [
            pl.BlockSpec((1, k, tn), lambda j, i: (layer, 0, j)),
            pl.BlockSpec((tm, tn), lambda j, i: (i, j)),
            pl.BlockSpec((1, 1, tn), lambda j, i: (i // per_b, 0, j)),
        ],
        out_specs=pl.BlockSpec((tm, tn), lambda j, i: (i, j)),
        out_shape=jax.ShapeDtypeStruct((m, d), _F32),
        scratch_shapes=[pltpu.VMEM((k, tn), _BF)],
        compiler_params=_cparams(("arbitrary", "arbitrary")),
        name="out_proj",
    )(*parts, w_out, h, gate)


def _t5_bucket(dist):
    max_exact = N_BUCKETS // 2
    d = jnp.maximum(dist, 1).astype(_F32)
    large = max_exact + (jnp.log(d / max_exact) / math.log(MAX_DISTANCE / max_exact)
                         * (N_BUCKETS - max_exact)).astype(jnp.int32)
    large = jnp.minimum(large, N_BUCKETS - 1)
    return jnp.where(dist < max_exact, dist, large)


def _dilated_log_multiplicity(n):
    delta = jnp.arange(n)
    mult = jnp.zeros((n,), _F32)
    for window, dil in DILATED_PATTERNS:
        mult = mult + ((delta % dil == 0) & (delta <= window)).astype(_F32)
    return jnp.where(mult > 0, jnp.log(jnp.maximum(mult, 1.0)), NEG_BIG)


def _skewed_bias(bias_vec, n_off):
    t = SM_BLOCK
    h = bias_vec.shape[0]
    c = n_off * t
    width = c + t
    rr = jnp.take(bias_vec, jnp.clip(jnp.arange(width) - t, 0, c - 1), axis=1)
    return pl.pallas_call(
        _skew_kernel,
        grid=(h,),
        in_specs=[pl.BlockSpec((1, 1, width), lambda i: (i, 0, 0))],
        out_specs=pl.BlockSpec((1, t, width), lambda i: (i, 0, 0)),
        out_shape=jax.ShapeDtypeStruct((h, t, width), _F32),
        compiler_params=_cparams(("arbitrary",)),
        name="bias_skew",
    )(rr.reshape(h, 1, width))


def _skew_kernel(rr_ref, o_ref):
    _, t, width = o_ref.shape
    rows = SUBLANES
    base = pltpu.roll(jnp.broadcast_to(rr_ref[0], (rows, width)), 0, 1, stride=1, stride_axis=0)
    for g in range(t // rows):
        o_ref[0, g * rows:(g + 1) * rows, :] = pltpu.roll(base, g * rows, 1)


def _flash_attend(qts, k_refs, col0s, bias_ref, vt_ref, bufs, scale2, lo, qi, finish, scores_first=False):
    t = SM_BLOCK
    raw_a, raw_b, p_a, p_b, acc_ref = bufs
    chains = range(len(qts))
    n = qts[0].shape[1]

    def scores_into(ki, s_ref):
        start = pl.multiple_of(ki * t, t)
        bias = _bias_tile(bias_ref, qi - ki)
        tops = []
        for c in chains:
            s = _dot(k_refs[c][pl.ds(start, t), :], qts[c]) * scale2 + bias[:, col0s[c]:col0s[c] + n]
            s_ref[c] = s
            tops.append(jnp.max(s, axis=0, keepdims=True))
        return tuple(tops)

    def softmax_from(s_ref, tops, causal, stats, p_ref):
        new_stats, alphas, ps = [], [], []
        for c in chains:
            m = stats[c]
            s = s_ref[c]
            top = tops[c]
            if causal:
                key = lax.broadcasted_iota(jnp.int32, s.shape, 0)
                qry = lax.broadcasted_iota(jnp.int32, s.shape, 1) + col0s[c]
                s = jnp.where(key <= qry, s, NEG_BIG)
                top = jnp.max(s, axis=0, keepdims=True)
            m_new = jnp.maximum(m, top)
            alphas.append(jnp.exp2(m - m_new))
            p = jnp.exp2(s - m_new)
            new_stats.append(m_new)
            if p_ref is None:
                ps.append(p.astype(_BF))
            else:
                p_ref[c] = p.astype(_BF)
        return tuple(new_stats), tuple(alphas), ps

    def add_weighted_values(ki, p_src, alphas):
        vt_blk = vt_ref[:, pl.ds(pl.multiple_of(ki * t, t), t)]
        for c in chains:
            acc_ref[c] = acc_ref[c] * alphas[c] + _dot(vt_blk, p_src[c])

    def half_step(ki, s_cur, s_next, p_cur, p_prev, tops, stats, alphas):
        if scores_first:
            tops_next = scores_into(ki + 1, s_next)
        add_weighted_values(jnp.maximum(ki - 1, lo), p_prev, alphas)
        if not scores_first:
            tops_next = scores_into(ki + 1, s_next)
        stats, alphas, _ = softmax_from(s_cur, tops, False, stats, p_cur)
        return tops_next, stats, alphas

    def last_steps(s_ref, p_prev, tops, stats, alphas):
        add_weighted_values(jnp.maximum(qi - 1, lo), p_prev, alphas)
        stats, alphas, ps = softmax_from(s_ref, tops, True, stats, None)
        add_weighted_values(qi, ps, alphas)
        finish(tuple((acc_ref[c, HEAD_DIM:HEAD_DIM + 1, :], acc_ref[c, :HEAD_DIM, :]) for c in chains))

    for c in chains:
        acc_ref[c] = jnp.zeros(acc_ref.shape[1:], _F32)
        p_b[c] = jnp.zeros(p_b.shape[1:], _BF)
    tops = scores_into(lo, raw_a)
    stats = tuple(jnp.full((1, n), NEG_BIG, _F32) for _ in chains)
    alphas = tuple(jnp.ones((1, n), _F32) for _ in chains)
    n_full = qi - lo

    def pair(it, carry):
        k0 = lo + 2 * it
        carry = half_step(k0, raw_a, raw_b, p_a, p_b, *carry)
        return half_step(k0 + 1, raw_b, raw_a, p_b, p_a, *carry)

    tops, stats, alphas = lax.fori_loop(0, n_full // 2, pair, (tops, stats, alphas))

    @pl.when(n_full % 2 == 1)
    def _():
        last_steps(raw_b, p_a, *half_step(qi - 1, raw_a, raw_b, p_a, p_b, tops, stats, alphas))

    @pl.when(n_full % 2 == 0)
    def _():
        last_steps(raw_a, p_b, tops, stats, alphas)


def _bias_tile(bias_ref, off):
    t = SM_BLOCK
    return bias_ref[0, :, pl.ds(pl.multiple_of((off + 1) * t, t), t)]


def _fill_v_transposed(v_ref, vt_ref):
    t = SM_BLOCK
    pad = lax.broadcasted_iota(jnp.int32, (V_ROWS - HEAD_DIM, t), 0)
    for c in range(v_ref.shape[0] // t):
        vt_ref[:HEAD_DIM, c * t:(c + 1) * t] = v_ref[c * t:(c + 1) * t, :].astype(_F32).T.astype(_BF)
        vt_ref[HEAD_DIM:, c * t:(c + 1) * t] = jnp.where(pad == 0, 1.0, 0.0).astype(_BF)


def _dilated_kernel(q_ref, k_ref, v_ref, z_ref, bias_ref, o_ref, vt_ref, *bufs, n_off):
    t = SM_BLOCK
    half = t // 2
    scale2 = HEAD_DIM ** -0.5 * LOG2E
    _fill_v_transposed(v_ref, vt_ref)

    def query_block(qi, carry):
        rows = pl.ds(pl.multiple_of(qi * t, t), t)
        qt = q_ref[rows, :].astype(_F32).T.astype(_BF)

        def finish(outs):
            o = jnp.concatenate([acc / l for l, acc in outs], axis=1).T
            o_ref[rows, :] = (o * _silu(z_ref[rows, :].astype(_F32))).astype(o_ref.dtype)

        _flash_attend((qt[:, :half], qt[:, half:]), (k_ref, k_ref), (0, half), bias_ref, vt_ref, bufs, scale2,
                      jnp.maximum(qi - (n_off - 1), 0), qi, finish)
        return carry

    lax.fori_loop(0, q_ref.shape[0] // t, query_block, 0)


def _diff_kernel(q1_ref, q2_ref, k1_ref, k2_ref, v_ref, z_ref, bias_ref, lam_ref, g_ref, o_ref, vt_ref, *bufs,
                 lam_init):
    t = SM_BLOCK
    h = pl.program_id(1)
    scale2 = DIFF_QK_DIM ** -0.5 * LOG2E
    dim = lax.broadcasted_iota(jnp.int32, (HEAD_DIM, t), 0)
    mine = (dim // DIFF_QK_DIM) == (h % 2)
    _fill_v_transposed(v_ref, vt_ref)
    lv = lam_ref[...]
    lam = (jnp.exp(jnp.sum(lv[0:1] * lv[1:2], axis=1, keepdims=True))
           - jnp.exp(jnp.sum(lv[2:3] * lv[3:4], axis=1, keepdims=True)) + lam_init)

    def query_block(qi, carry):
        rows = pl.ds(pl.multiple_of(qi * t, t), t)
        q1t = jnp.where(mine, q1_ref[rows, :].astype(_F32).T, 0.0).astype(_BF)
        q2t = jnp.where(mine, q2_ref[rows, :].astype(_F32).T, 0.0).astype(_BF)

        def finish(outs):
            (l1, a1), (l2, a2) = outs
            o = a1 / l1 - lam * (a2 / l2)
            o = (o * lax.rsqrt(jnp.mean(o * o, axis=0, keepdims=True) + EPS)).T
            o = o * (g_ref[...] * (1.0 - lam_init))
            o_ref[rows, :] = (o * _silu(z_ref[rows, :].astype(_F32))).astype(o_ref.dtype)

        _flash_attend((q1t, q2t), (k1_ref, k2_ref), (0, 0), bias_ref, vt_ref, bufs, scale2, 0, qi, finish,
                      scores_first=True)
        return carry

    lax.fori_loop(0, q1_ref.shape[0] // t, query_block, 0)


def _stick_kernel(q_ref, k_ref, v_ref, z_ref, o_ref, acc_ref):
    t = STICK_BLOCK
    d = HEAD_DIM
    heads = range(q_ref.shape[1] // d)
    blocks_per_step = q_ref.shape[0] // t
    scale2 = d ** -0.5 * LOG2E
    row = lax.broadcasted_iota(jnp.int32, (t, t), 0)
    col = lax.broadcasted_iota(jnp.int32, (t, t), 1)
    later = jnp.where(row > col, 1.0, 0.0).astype(_BF)
    strict = col < row

    def query_block(j, carry):
        qi = pl.program_id(2) * blocks_per_step + j
        rows = pl.ds(pl.multiple_of(j * t, t), t)
        qs = [q_ref[rows, c * d:(c + 1) * d] for c in heads]

        def step(ki, tails, diagonal):
            start = pl.multiple_of(ki * t, t)
            zs = [_dot_nt(qs[c], k_ref[pl.ds(start, t), c * d:(c + 1) * d]) for c in heads]
            log_betas, parts = [], []
            for c in heads:
                z = zs[c] * scale2
                softplus = jnp.log(1.0 + jnp.exp2(-jnp.abs(z))) * LOG2E
                log_beta = jnp.minimum(z, 0.0) - softplus
                log_1m = log_beta - z
                if diagonal:
                    log_1m = jnp.where(strict, log_1m, 0.0)
                log_betas.append(log_beta)
                parts.append(_split_hi_lo(log_1m) + (log_1m[:, 0:1],))
            sums = [_dot(hi, later) + _dot(lo, later) for hi, lo, _ in parts]
            weights = []
            for c in heads:
                a = jnp.exp2(log_betas[c] + (tails[c] + sums[c]))
                if diagonal:
                    a = jnp.where(strict, a, 0.0)
                weights.append(a.astype(_BF))
            for c in heads:
                acc_ref[c] = acc_ref[c] + _dot(weights[c], v_ref[pl.ds(start, t), c * d:(c + 1) * d])
            return tuple(tails[c] + (sums[c][:, 0:1] + parts[c][2]) for c in heads)

        for c in heads:
            acc_ref[c] = jnp.zeros((t, d), _F32)
        tails = step(qi, tuple(jnp.zeros((t, 1), _F32) for _ in heads), True)

        def live(tails):
            worst = functools.reduce(jnp.maximum, tails)
            return (jnp.max(worst) > STICK_SKIP_LOG2).astype(jnp.int32)

        def cond(state):
            i, _, alive = state
            return jnp.logical_and(i < qi, alive > 0)

        def body(state):
            i, tails, _ = state
            tails = step(qi - 1 - i, tails, False)
            return i + 1, tails, live(tails)

        lax.while_loop(cond, body, (jnp.int32(0), tails, live(tails)))
        for c in heads:
            gate = _silu(z_ref[rows, c * d:(c + 1) * d].astype(_F32))
            o_ref[rows, c * d:(c + 1) * d] = (acc_ref[c] * gate).astype(o_ref.dtype)
        return carry

    lax.fori_loop(0, blocks_per_step, query_block, 0)


def _flash_buffers(chains, n):
    t = SM_BLOCK
    return [pltpu.VMEM((chains, t, n), _F32), pltpu.VMEM((chains, t, n), _F32),
            pltpu.VMEM((chains, t, n), _BF), pltpu.VMEM((chains, t, n), _BF),
            pltpu.VMEM((chains, V_ROWS, n), _F32)]


def _odd_mixer(proj, bsz, seq, d_inner):
    t = STICK_BLOCK
    w = STICK_HEADS * HEAD_DIM
    groups = d_inner // w
    rows = min(STICK_ROWS, seq)
    nr = seq // rows
    tile = lambda col0: pl.BlockSpec((rows, w), lambda b, g, r: (b * nr + r, col0 + g))
    whole = lambda col0: pl.BlockSpec((seq, w), lambda b, g, r: (b, col0 + g))
    return pl.pallas_call(
        _stick_kernel,
        grid=(bsz, groups, nr),
        in_specs=[tile(0), whole(groups), whole(2 * groups), tile(3 * groups)],
        out_specs=tile(0),
        out_shape=jax.ShapeDtypeStruct((bsz * seq, d_inner), _BF),
        scratch_shapes=[pltpu.VMEM((STICK_HEADS, t, HEAD_DIM), _F32)],
        compiler_params=_cparams(("arbitrary", "arbitrary", "arbitrary")),
        name="stick_breaking",
    )(proj, proj, proj, proj)


def _even_mixer(proj, bias_a, bias_b, lam_vec, subln_g, lam_init, bsz, seq, d_inner):
    d_a = d_inner // 2
    h_a = d_a // HEAD_DIM
    h_b = (d_inner - d_a) // HEAD_DIM
    t = SM_BLOCK
    n_off_a = bias_a.shape[2] // t - 1
    n_off_b = bias_b.shape[2] // t - 1
    out_shape = jax.ShapeDtypeStruct((bsz * seq, d_a), _BF)
    params = _cparams(("arbitrary", "arbitrary"))
    z0 = 3 * d_inner // HEAD_DIM
    col = lambda col0, per_block=1: pl.BlockSpec((seq, HEAD_DIM), lambda b, h: (b, col0 + h // per_block))
    bias_spec = lambda n_off: pl.BlockSpec((1, t, (n_off + 1) * t), lambda b, h: (h, 0, 0))

    o_a = pl.pallas_call(
        functools.partial(_dilated_kernel, n_off=n_off_a),
        grid=(bsz, h_a),
        in_specs=[col(0), col(h_a), col(2 * h_a), col(z0), bias_spec(n_off_a)],
        out_specs=col(0),
        out_shape=out_shape,
        scratch_shapes=[pltpu.VMEM((V_ROWS, seq), _BF)] + _flash_buffers(2, t // 2),
        compiler_params=params,
        name="dilated_attention",
    )(proj, proj, proj, proj, bias_a)

    c0 = 3 * h_a
    half = h_b // 2
    o_b = pl.pallas_call(
        functools.partial(_diff_kernel, lam_init=lam_init),
        grid=(bsz, h_b),
        in_specs=[col(c0, 2), col(c0 + half, 2), col(c0 + 2 * half, 2), col(c0 + 3 * half, 2),
                  col(c0 + 4 * half), col(z0 + h_a), bias_spec(n_off_b),
                  pl.BlockSpec((4, DIFF_QK_DIM), lambda b, h: (0, 0)),
                  pl.BlockSpec((1, HEAD_DIM), lambda b, h: (0, 0))],
        out_specs=col(0),
        out_shape=out_shape,
        scratch_shapes=[pltpu.VMEM((V_ROWS, seq), _BF)] + _flash_buffers(2, t),
        compiler_params=params,
        name="diff_attention",
    )(proj, proj, proj, proj, proj, proj, bias_b, lam_vec, subln_g.reshape(1, HEAD_DIM))
    return o_a, o_b


def kernel(x, c, norm_g, w_mod, b_mod, w_in, w_out, rel_bias, diff_lambda, diff_subln_g, final_norm_g):
    bsz, seq, d = x.shape
    depth = w_in.shape[0]
    d_inner = w_out.shape[1]
    h_a = d_inner // 2 // HEAD_DIM
    n_blk = seq // SM_BLOCK

    mod = _modulation(c, w_mod, b_mod)

    table = rel_bias[_t5_bucket(jnp.arange(seq))].T
    n_off_a = min(n_blk, DILATED_PATTERNS[-1][0] // SM_BLOCK + 1)
    vec_a = table[:h_a, : n_off_a * SM_BLOCK] + _dilated_log_multiplicity(n_off_a * SM_BLOCK)[None]
    bias_a = _skewed_bias(vec_a * LOG2E, n_off_a)
    bias_b = _skewed_bias(table[h_a:] * LOG2E, n_blk)

    h = x.reshape(bsz * seq, d)
    for layer in range(depth):
        shift = mod[layer, :, 0:d].reshape(bsz, 1, d)
        scale = mod[layer, :, d:2 * d].reshape(bsz, 1, d)
        gate = mod[layer, :, 2 * d:].reshape(bsz, 1, d)
        u = _norm_mod(h, norm_g[layer], scale, shift, seq)
        proj = _in_proj(u, w_in, layer)
        if layer % 2 == 0:
            e = layer // 2
            lam_init = 0.8 - 0.6 * math.exp(-0.3 * layer)
            mixed = _even_mixer(proj, bias_a, bias_b, diff_lambda[e], diff_subln_g[e], lam_init,
                                bsz, seq, d_inner)
        else:
            mixed = (_odd_mixer(proj, bsz, seq, d_inner),)
        h = _out_proj(mixed, w_out, layer, h, gate, seq)
    return _final_norm(h, final_norm_g).reshape(bsz, seq, d)
```
